```python
import jax
import jax.numpy as jnp
from jax import lax
import numpy as np

D_MODEL = 1024
BATCH = 32
SEQ = 2048
DEPTH = 4

CHUNK = 64
N_MEM = 256
ROPE_THETA = 10000.0
NORM_EPS = 1e-6
NEG_INF = -1e30

RWKV_HEADS = 8
RWKV_HEAD_DIM = 64
RWKV_DIM = RWKV_HEADS * RWKV_HEAD_DIM
RWKV_DECAY_RANK = 64
RWKV_A_RANK = 64
RWKV_V_RANK = 32
RWKV_GATE_RANK = 128
RWKV_LNX_EPS = 1e-5 * RWKV_HEAD_DIM
RWKV_SIZES = (RWKV_DIM, RWKV_DIM, RWKV_DIM, RWKV_DECAY_RANK, RWKV_A_RANK, RWKV_GATE_RANK)
RWKV_COLS = sum(RWKV_SIZES)

DSA_HEADS = 8
DSA_HEAD_DIM = 64
DSA_DIM = DSA_HEADS * DSA_HEAD_DIM
IDX_HEADS = 8
IDX_DIM = 64
DSA_TOPK_MAX = 256
Q_BLOCK = 128
DSA_SIZES = (DSA_DIM, DSA_HEAD_DIM, DSA_HEAD_DIM, IDX_HEADS * IDX_DIM, IDX_DIM, IDX_HEADS)
DSA_COLS = sum(DSA_SIZES)

IN_SIZES = (RWKV_COLS, DSA_COLS, D_MODEL, D_MODEL)
N_IN = sum(IN_SIZES)

MEM_HEADS = 4
MEM_HEAD_DIM = 128
MEM_DIM = MEM_HEADS * MEM_HEAD_DIM

FFN_DIM = 2816
N_EXPERTS = 8
TOP_K = 2
EXPERT_DIM = 1408
N_DENSE = (DEPTH + 1) // 2
N_MOE = DEPTH // 2

kernel_name = 'hybrid_rwkv7_dsa_memxattn_moe_trunk'


def split_cols(t, sizes):
    return jnp.split(t, np.cumsum(sizes)[:-1].tolist(), axis=-1)


def rms_norm(x, g):
    x32 = x.astype(jnp.float32)
    y = x32 * lax.rsqrt(jnp.mean(x32 * x32, axis=-1, keepdims=True) + NORM_EPS)
    return (y * g.astype(jnp.float32)).astype(x.dtype)


def rope_tables(positions, dim):
    inv_freq = 1.0 / (ROPE_THETA ** (jnp.arange(0, dim, 2, dtype=jnp.float32) / dim))
    ang = positions.astype(jnp.float32)[..., None] * inv_freq
    return jnp.cos(ang), jnp.sin(ang)


def apply_rope(x, cos, sin):
    shape = cos.shape[:2] + (1,) * (x.ndim - 3) + cos.shape[2:]
    c, s = cos.reshape(shape), sin.reshape(shape)
    x1, x2 = jnp.split(x.astype(jnp.float32), 2, axis=-1)
    return jnp.concatenate([x1 * c - x2 * s, x2 * c + x1 * s], axis=-1).astype(x.dtype)


def token_shift(t):
    return jnp.pad(t[:, :-1], ((0, 0), (1, 0), (0, 0)))


def rwkv7_scan(r, w, k, v, a, b):
    bsz, _, h, n = r.shape
    xs = tuple(jnp.moveaxis(t, 1, 0) for t in (r, w, k, v, a, b))

    def step(state, inp):
        r_t, w_t, k_t, v_t, a_t, b_t = inp
        sa = jnp.einsum('bhvk,bhk->bhv', state, a_t)
        state = (state * w_t[:, :, None, :] + sa[..., None] * b_t[:, :, None, :]
                 + v_t[..., None] * k_t[:, :, None, :])
        return state, jnp.einsum('bhvk,bhk->bhv', state, r_t)

    _, ys = lax.scan(step, jnp.zeros((bsz, h, n, n), jnp.float32), xs)
    return jnp.moveaxis(ys, 0, 1)


def rwkv7_branch(cols, v_first, mu, w0, w2, a0, a2, g2, v_mix, k_k, k_a, r_k, lnx_g, lnx_b):
    bsz, seqlen, _ = cols.shape
    cols = cols + (token_shift(cols) - cols) * mu
    r, k, v, wd, ad, gd = split_cols(cols, RWKV_SIZES)
    w_log = -jax.nn.softplus(-(w0 + jnp.tanh(wd) @ w2)) - 0.5
    a = jax.nn.sigmoid(a0 + ad @ a2)
    g = jax.nn.sigmoid(gd) @ g2
    if v_mix is None:
        v_first = v
    else:
        v0, v1, v2 = v_mix
        v = v + (v_first - v) * jax.nn.sigmoid(v0 + (v @ v1) @ v2)
    heads = lambda t: t.astype(jnp.float32).reshape(bsz, seqlen, RWKV_HEADS, RWKV_HEAD_DIM)
    kk = heads(k * k_k)
    kk = kk / jnp.maximum(jnp.sqrt(jnp.sum(kk * kk, axis=-1, keepdims=True)), 1e-12)
    k = k * (1.0 + (a - 1.0) * k_a)
    rh, kh, vh, ah = heads(r), heads(k), heads(v), heads(a)
    decay = jnp.exp(-jnp.exp(heads(w_log)))
    y = rwkv7_scan(rh, decay, kh, vh, -kk, kk * ah)
    mean = jnp.mean(y, axis=-1, keepdims=True)
    var = jnp.mean(jnp.square(y - mean), axis=-1, keepdims=True)
    y = (y - mean) * lax.rsqrt(var + RWKV_LNX_EPS)
    y = y.reshape(bsz, seqlen, RWKV_DIM) * lnx_g.astype(jnp.float32) + lnx_b.astype(jnp.float32)
    bonus = jnp.sum(rh * kh * r_k.astype(jnp.float32), axis=-1, keepdims=True) * vh
    y = y + bonus.reshape(bsz, seqlen, RWKV_DIM)
    return (y * g.astype(jnp.float32)).astype(cols.dtype), v_first


def dsa_attention(q, k, v, q_idx, k_idx, w_idx):
    length = q.shape[1]
    top_k = min(DSA_TOPK_MAX, length // 4)
    key_pos = jnp.arange(length)
    gather = jax.vmap(lambda t, i: t[i])
    k_idx32 = k_idx.astype(jnp.float32)

    def query_block(i):
        start = i * Q_BLOCK
        qb = lax.dynamic_slice_in_dim(q, start, Q_BLOCK, axis=1)
        qib = lax.dynamic_slice_in_dim(q_idx, start, Q_BLOCK, axis=1).astype(jnp.float32)
        wb = lax.dynamic_slice_in_dim(w_idx, start, Q_BLOCK, axis=1).astype(jnp.float32)
        limit = ((start + jnp.arange(Q_BLOCK)) // CHUNK + 1) * CHUNK
        admissible = key_pos[None, :] < limit[:, None]
        logits = jnp.einsum('bqhd,bsd->bqhs', qib, k_idx32) * (IDX_DIM ** -0.5)
        score = jnp.einsum('bqhs,bqh->bqs', jax.nn.relu(logits), wb) * (IDX_HEADS ** -0.5)
        score = jnp.where(admissible[None], score, NEG_INF)
        _, sel = lax.top_k(score, top_k)
        valid = sel < limit[None, :, None]
        ks, vs = gather(k, sel), gather(v, sel)
        s = jnp.einsum('bqhd,bqkd->bqhk', qb, ks).astype(jnp.float32) * (DSA_HEAD_DIM ** -0.5)
        p = jax.nn.softmax(jnp.where(valid[:, :, None, :], s, NEG_INF), axis=-1)
        return jnp.einsum('bqhk,bqkd->bqhd', p.astype(vs.dtype), vs)

    out = lax.map(query_block, jnp.arange(length // Q_BLOCK))
    return jnp.moveaxis(out, 0, 1).reshape(q.shape)


def dsa_branch(cols, cos_q, sin_q, cos_i, sin_i, q_norm, k_norm, idx_k_norm):
    bsz, seqlen, _ = cols.shape
    q, k, v, iq, ik, iw = split_cols(cols, DSA_SIZES)
    q = apply_rope(rms_norm(q.reshape(bsz, seqlen, DSA_HEADS, DSA_HEAD_DIM), q_norm), cos_q, sin_q)
    k = apply_rope(rms_norm(k, k_norm), cos_q, sin_q)
    iq = apply_rope(iq.reshape(bsz, seqlen, IDX_HEADS, IDX_DIM), cos_i, sin_i)
    ik = apply_rope(rms_norm(ik, idx_k_norm), cos_i, sin_i)
    return dsa_attention(q, k, v, iq, ik, iw).reshape(bsz, seqlen, DSA_DIM)


def memory_attention(h, mem_n, wq, wkv, q_norm, k_norm, wo):
    bsz, seqlen, _ = h.shape
    n_mem = mem_n.shape[1]
    q = rms_norm((h @ wq).reshape(bsz, seqlen, MEM_HEADS, MEM_HEAD_DIM), q_norm)
    k, v = split_cols(mem_n @ wkv, (MEM_DIM, MEM_DIM))
    k = rms_norm(k.reshape(bsz, n_mem, MEM_HEADS, MEM_HEAD_DIM), k_norm)
    v = v.reshape(bsz, n_mem, MEM_HEADS, MEM_HEAD_DIM)
    s = jnp.einsum('bshd,bmhd->bhsm', q, k).astype(jnp.float32) * (MEM_HEAD_DIM ** -0.5)
    p = jax.nn.softmax(s, axis=-1).astype(v.dtype)
    o = jnp.einsum('bhsm,bmhd->bshd', p, v).reshape(bsz, seqlen, MEM_DIM)
    return o @ wo


def swiglu(h, wg, wu, wd):
    return (jax.nn.silu(h @ wg) * (h @ wu)) @ wd


def moe_swiglu(h, router, bias, wg, wu, wd):
    logits = (h @ router).astype(jnp.float32) + bias.astype(jnp.float32)
    top_logit, top_idx = lax.top_k(logits, TOP_K)
    top_w = jax.nn.softmax(top_logit, axis=-1)
    gates = jnp.sum(jax.nn.one_hot(top_idx, N_EXPERTS, dtype=jnp.float32) * top_w[..., None], axis=-2)
    gates = gates.astype(h.dtype)
    out = jnp.zeros_like(h)
    for e in range(N_EXPERTS):
        out = out + gates[..., e:e + 1] * swiglu(h, wg[e], wu[e], wd[e])
    return out


def setup_inputs(seed: int = 0) -> dict:
    keys = iter(jax.random.split(jax.random.key(seed), 48))
    f32 = jnp.float32

    def normal(shape, scale):
        return jax.random.normal(next(keys), shape, f32) * scale

    def gain(shape):
        return 1.0 + normal(shape, 0.02)

    x = normal((BATCH, SEQ, D_MODEL), 1.0)
    mem = normal((BATCH, N_MEM, D_MODEL), 1.0)
    offsets = jax.random.randint(next(keys), (BATCH, 1), 0, 64) * CHUNK
    positions = (offsets + jnp.arange(SEQ)[None, :]).astype(jnp.int32)
    nv = DEPTH - 1
    return {
        'x': x,
        'mem': mem,
        'positions': positions,
        'norm_mix': gain((DEPTH, D_MODEL)),
        'w_in': normal((DEPTH, D_MODEL, N_IN), D_MODEL ** -0.5),
        'rwkv_mu': jax.random.uniform(next(keys), (DEPTH, RWKV_COLS), f32, 0.0, 1.0),
        'rwkv_w0': jax.random.uniform(next(keys), (DEPTH, RWKV_DIM), f32, -6.0, -1.0),
        'rwkv_w2': normal((DEPTH, RWKV_DECAY_RANK, RWKV_DIM), 0.5 * RWKV_DECAY_RANK ** -0.5),
        'rwkv_a0': normal((DEPTH, RWKV_DIM), 0.1),
        'rwkv_a2': normal((DEPTH, RWKV_A_RANK, RWKV_DIM), RWKV_A_RANK ** -0.5),
        'rwkv_g2': normal((DEPTH, RWKV_GATE_RANK, RWKV_DIM), RWKV_GATE_RANK ** -0.5),
        'rwkv_v0': normal((nv, RWKV_DIM), 0.5),
        'rwkv_v1': normal((nv, RWKV_DIM, RWKV_V_RANK), RWKV_DIM ** -0.5),
        'rwkv_v2': normal((nv, RWKV_V_RANK, RWKV_DIM), RWKV_V_RANK ** -0.5),
        'rwkv_kk': 0.85 + normal((DEPTH, RWKV_DIM), 0.02),
        'rwkv_ka': gain((DEPTH, RWKV_DIM)),
        'rwkv_rk': normal((DEPTH, RWKV_HEADS, RWKV_HEAD_DIM), 0.1),
        'rwkv_lnx_g': gain((DEPTH, RWKV_DIM)),
        'rwkv_lnx_b': normal((DEPTH, RWKV_DIM), 0.02),
        'dsa_q_norm': gain((DEPTH, DSA_HEAD_DIM)),
        'dsa_k_norm': gain((DEPTH, DSA_HEAD_DIM)),
        'idx_k_norm': gain((DEPTH, IDX_DIM)),
        'w_branch_a': normal((DEPTH, RWKV_DIM, D_MODEL), RWKV_DIM ** -0.5),
        'w_branch_b': normal((DEPTH, DSA_DIM, D_MODEL), DSA_DIM ** -0.5),
        'w_out': normal((DEPTH, D_MODEL, D_MODEL), 0.5 * D_MODEL ** -0.5),
        'norm_mem': gain((DEPTH, D_MODEL)),
        'mem_tok_norm': gain((DEPTH, D_MODEL)),
        'mem_wq': normal((DEPTH, D_MODEL, MEM_DIM), D_MODEL ** -0.5),
        'mem_wkv': normal((DEPTH, D_MODEL, 2 * MEM_DIM), D_MODEL ** -0.5),
        'mem_q_norm': gain((DEPTH, MEM_HEAD_DIM)),
        'mem_k_norm': gain((DEPTH, MEM_HEAD_DIM)),
        'mem_wo': normal((DEPTH, MEM_DIM, D_MODEL), 0.5 * MEM_DIM ** -0.5),
        'norm_ffn': gain((DEPTH, D_MODEL)),
        'ffn_wg': normal((N_DENSE, D_MODEL, FFN_DIM), D_MODEL ** -0.5),
        'ffn_wu': normal((N_DENSE, D_MODEL, FFN_DIM), D_MODEL ** -0.5),
        'ffn_wd': normal((N_DENSE, FFN_DIM, D_MODEL), 0.5 * FFN_DIM ** -0.5),
        'moe_router': normal((N_MOE, D_MODEL, N_EXPERTS), D_MODEL ** -0.5),
        'moe_bias': normal((N_MOE, N_EXPERTS), 0.01),
        'moe_wg': normal((N_MOE, N_EXPERTS, D_MODEL, EXPERT_DIM), D_MODEL ** -0.5),
        'moe_wu': normal((N_MOE, N_EXPERTS, D_MODEL, EXPERT_DIM), D_MODEL ** -0.5),
        'moe_wd': normal((N_MOE, N_EXPERTS, EXPERT_DIM, D_MODEL), 0.5 * EXPERT_DIM ** -0.5),
    }


def reference(x, mem, positions, norm_mix, w_in, rwkv_mu, rwkv_w0, rwkv_w2, rwkv_a0, rwkv_a2,
              rwkv_g2, rwkv_v0, rwkv_v1, rwkv_v2, rwkv_kk, rwkv_ka, rwkv_rk, rwkv_lnx_g, rwkv_lnx_b,
              dsa_q_norm, dsa_k_norm, idx_k_norm, w_branch_a, w_branch_b, w_out, norm_mem,
              mem_tok_norm, mem_wq, mem_wkv, mem_q_norm, mem_k_norm, mem_wo, norm_ffn, ffn_wg,
              ffn_wu, ffn_wd, moe_router, moe_bias, moe_wg, moe_wu, moe_wd):
    cos_q, sin_q = rope_tables(positions, DSA_HEAD_DIM)
    cos_i, sin_i = rope_tables(positions, IDX_DIM)
    v_first = None
    for l in range(DEPTH):
        h = rms_norm(x, norm_mix[l])
        cols_a, cols_b, gate_a, gate_b = split_cols(h @ w_in[l], IN_SIZES)
        v_mix = None if l == 0 else (rwkv_v0[l - 1], rwkv_v1[l - 1], rwkv_v2[l - 1])
        y_a, v_first = rwkv7_branch(cols_a, v_first, rwkv_mu[l], rwkv_w0[l], rwkv_w2[l], rwkv_a0[l],
                                    rwkv_a2[l], rwkv_g2[l], v_mix, rwkv_kk[l], rwkv_ka[l], rwkv_rk[l],
                                    rwkv_lnx_g[l], rwkv_lnx_b[l])
        y_b = dsa_branch(cols_b, cos_q, sin_q, cos_i, sin_i, dsa_q_norm[l], dsa_k_norm[l], idx_k_norm[l])
        merged = (jax.nn.sigmoid(gate_a) * (y_a @ w_branch_a[l])
                  + jax.nn.sigmoid(gate_b) * (y_b @ w_branch_b[l]))
        x = x + merged @ w_out[l]
        x = x + memory_attention(rms_norm(x, norm_mem[l]), rms_norm(mem, mem_tok_norm[l]), mem_wq[l],
                                 mem_wkv[l], mem_q_norm[l], mem_k_norm[l], mem_wo[l])
        h = rms_norm(x, norm_ffn[l])
        j = l // 2
        if l % 2 == 0:
            x = x + swiglu(h, ffn_wg[j], ffn_wu[j], ffn_wd[j])
        else:
            x = x + moe_swiglu(h, moe_router[j], moe_bias[j], moe_wg[j], moe_wu[j], moe_wd[j])
    return x
```

```python
import functools

import jax
import jax.numpy as jnp
import numpy as np
from jax import lax
from jax.experimental import pallas as pl
from jax.experimental.pallas import tpu as pltpu

F32 = jnp.float32
BF16 = jnp.bfloat16
HIGHEST = lax.Precision.HIGHEST

NORM_EPS = 1e-6
NEG_INF = -1e30
ROPE_THETA = 10000.0

CHUNK = 64
Q_BLOCK = 128
DSA_TOPK_MAX = 256
HEAD_DIM = 64
N_HEADS = 8
MIX_DIM = N_HEADS * HEAD_DIM
RWKV_LNX_EPS = 1e-5 * HEAD_DIM
MEM_HEADS = 4
MEM_HEAD_DIM = 128
N_EXPERTS = 8

RWKV_CHUNK = 64
RWKV_BLOCK = 256
ROW_TILE = 512
VMEM_LIMIT = 48 * 1024 * 1024


def _params(*sem):
    return pltpu.CompilerParams(dimension_semantics=sem, vmem_limit_bytes=VMEM_LIMIT)


def _full(shape):
    nd = len(shape)
    return pl.BlockSpec(shape, lambda *_: (0,) * nd)


def _dot(a, b, precision=None):
    return jnp.dot(a, b, preferred_element_type=F32, precision=precision)


def _dot_nt(a, b, precision=None):
    return lax.dot_general(a, b, (((1,), (1,)), ((), ())), preferred_element_type=F32,
                           precision=precision)


def _dot_tn(a, b, precision=None):
    return lax.dot_general(a, b, (((0,), (0,)), ((), ())), preferred_element_type=F32,
                           precision=precision)


def _bdot(a, b):
    return _dot(a.astype(BF16), b.astype(BF16))


def _rms(x, g):
    return x * lax.rsqrt(jnp.mean(x * x, axis=-1, keepdims=True) + NORM_EPS) * g


def _sigmoid(x):
    return 1.0 / (1.0 + jnp.exp(-x))


def _head_ones(width, head):
    r = lax.broadcasted_iota(jnp.int32, (width, width), 0) // head
    c = lax.broadcasted_iota(jnp.int32, (width, width), 1) // head
    return (r == c).astype(F32)


def _norm_matmul_kernel(x_ref, g_ref, w_ref, o_ref):
    h = _rms(x_ref[...], g_ref[...]).astype(BF16)
    o_ref[...] = _dot(h, w_ref[...]).astype(o_ref.dtype)


def _norm_matmul(x, g, w, tm=ROW_TILE):
    t, d = x.shape
    n = w.shape[1]
    tm = min(tm, t)
    return pl.pallas_call(
        _norm_matmul_kernel,
        grid=(t // tm,),
        in_specs=[pl.BlockSpec((tm, d), lambda i: (i, 0)), _full((1, d)), _full((d, n))],
        out_specs=pl.BlockSpec((tm, n), lambda i: (i, 0)),
        out_shape=jax.ShapeDtypeStruct((t, n), F32),
        compiler_params=_params("parallel"),
        name="norm_matmul",
    )(x, g.reshape(1, d), w)


def _rwkv_kernel(has_vmix, *refs):
    if has_vmix:
        (cols_ref, vf_ref, mu_ref, w0_ref, a0_ref, wa2_ref, g2_ref, kk_ref, ka_ref, rk_ref,
         lg_ref, lb_ref, v0_ref, v1_ref, v2_ref,
         y_ref, st_ref, prev_ref, at_ref, rt_ref, bt_ref, kt_ref, bh_ref, kh_ref, vv_ref,
         ll_ref, yy_ref) = refs
        vout_ref = None
    else:
        (cols_ref, mu_ref, w0_ref, a0_ref, wa2_ref, g2_ref, kk_ref, ka_ref, rk_ref,
         lg_ref, lb_ref,
         y_ref, vout_ref, st_ref, prev_ref, at_ref, rt_ref, bt_ref, kt_ref, bh_ref, kh_ref,
         vv_ref, ll_ref, yy_ref) = refs

    tb = cols_ref.shape[1]
    n_chunks = tb // RWKV_CHUNK
    c = RWKV_CHUNK

    @pl.when(pl.program_id(1) == 0)
    def _():
        st_ref[...] = jnp.zeros_like(st_ref)
        prev_ref[...] = jnp.zeros_like(prev_ref)

    cols = cols_ref[0]
    row = lax.broadcasted_iota(jnp.int32, cols.shape, 0)
    shifted = jnp.where(row == 0, prev_ref[...], pltpu.roll(cols, 1, 0))
    prev_ref[...] = cols[tb - 1:tb, :]
    mixed = cols + (shifted - cols) * mu_ref[...]

    r = mixed[:, 0:MIX_DIM]
    k = mixed[:, MIX_DIM:2 * MIX_DIM]
    v = mixed[:, 2 * MIX_DIM:3 * MIX_DIM]
    wa = mixed[:, 3 * MIX_DIM:3 * MIX_DIM + 128]
    gd = mixed[:, 3 * MIX_DIM + 128:3 * MIX_DIM + 256]
    lane = lax.broadcasted_iota(jnp.int32, wa.shape, 1)
    wa = jnp.where(lane < 64, jnp.tanh(wa), wa)
    wa_out = _dot(wa, wa2_ref[...], HIGHEST)
    u_dec = w0_ref[...] + wa_out[:, 0:MIX_DIM]
    a = _sigmoid(a0_ref[...] + wa_out[:, MIX_DIM:2 * MIX_DIM])
    g = _dot(_sigmoid(gd), g2_ref[...], HIGHEST)
    if has_vmix:
        vm = _dot(_dot(v, v1_ref[...], HIGHEST), v2_ref[...], HIGHEST)
        v = v + (vf_ref[0] - v) * _sigmoid(v0_ref[...] + vm)
    else:
        vout_ref[0] = v

    hb = _head_ones(MIX_DIM, HEAD_DIM)
    kk = k * kk_ref[...]
    kk = kk / jnp.maximum(jnp.sqrt(_dot(kk * kk, hb, HIGHEST)), 1e-12)
    k = k * (1.0 + (a - 1.0) * ka_ref[...])
    ll = -_sigmoid(u_dec) * float(np.exp(-0.5))

    ri = lax.broadcasted_iota(jnp.int32, (tb, tb), 0)
    ci = lax.broadcasted_iota(jnp.int32, (tb, tb), 1)
    same = (ri // c) == (ci // c)
    cum = _dot((same & (ci <= ri)).astype(F32), ll, HIGHEST)
    tot = _dot(same.astype(F32), ll, HIGHEST)
    g_in = jnp.exp(cum)
    g_inv = jnp.exp(-cum)
    g_tail = jnp.exp(tot - cum)
    beta = kk * a
    at_ref[...] = -kk * jnp.exp(cum - ll)
    rt_ref[...] = r * g_in
    bt_ref[...] = beta * g_inv
    kt_ref[...] = k * g_inv
    bh_ref[...] = beta * g_tail
    kh_ref[...] = k * g_tail
    vv_ref[...] = v
    ll_ref[...] = ll

    ti = lax.broadcasted_iota(jnp.int32, (c, c), 0)
    tj = lax.broadcasted_iota(jnp.int32, (c, c), 1)
    strict = ti > tj
    incl = ti >= tj
    eye = (ti == tj).astype(F32)
    ones_c = jnp.ones((c, HEAD_DIM), F32)

    def chunk_body(ci_, carry):
        rows = pl.ds(pl.multiple_of(ci_ * c, c), c)
        for h in range(N_HEADS):
            cs = slice(h * HEAD_DIM, (h + 1) * HEAD_DIM)
            a_t = at_ref[rows, cs].astype(BF16)
            r_t = rt_ref[rows, cs].astype(BF16)
            b_t = bt_ref[rows, cs].astype(BF16)
            k_t = kt_ref[rows, cs].astype(BF16)
            b_h = bh_ref[rows, cs].astype(BF16)
            k_h = kh_ref[rows, cs].astype(BF16)
            v_c = vv_ref[rows, cs].astype(BF16)
            st = st_ref[h]
            n_mat = jnp.where(strict, _dot_nt(a_t, b_t), 0.0)
            m_mat = jnp.where(strict, _dot_nt(a_t, k_t), 0.0)
            rb = jnp.where(incl, _dot_nt(r_t, b_t), 0.0)
            rk = jnp.where(incl, _dot_nt(r_t, k_t), 0.0)
            p = n_mat
            inv = eye + n_mat
            for _ in range(int(np.log2(c)) - 1):
                p = _bdot(p, p)
                inv = inv + _bdot(inv, p)
            st_b = st.astype(BF16)
            x1 = _dot(a_t, st_b) + _bdot(m_mat, v_c)
            u = _bdot(inv, x1)
            u_b = u.astype(BF16)
            y = _dot(r_t, st_b) + _bdot(rb, u_b) + _bdot(rk, v_c)
            yy_ref[rows, cs] = y
            g_col = jnp.exp(_dot_tn(ll_ref[rows, cs], ones_c, HIGHEST))
            st_ref[h] = st * g_col + _dot_tn(b_h, u_b) + _dot_tn(k_h, v_c)
        return carry

    lax.fori_loop(0, n_chunks, chunk_body, 0)

    y = yy_ref[...]
    mean = _dot(y, hb, HIGHEST) * (1.0 / HEAD_DIM)
    yc = y - mean
    var = _dot(yc * yc, hb, HIGHEST) * (1.0 / HEAD_DIM)
    yn = yc * lax.rsqrt(var + RWKV_LNX_EPS) * lg_ref[...] + lb_ref[...]
    bonus = _dot(r * k * rk_ref[...], hb, HIGHEST) * v
    y_ref[0] = ((yn + bonus) * g).astype(y_ref.dtype)


def _rwkv_params(w, l):
    p = {k: w["rwkv_" + k][l] for k in ("mu", "w0", "w2", "a0", "a2", "g2", "kk", "ka", "rk", "lnx_g", "lnx_b")}
    if l > 0:
        p.update({k: w["rwkv_" + k][l - 1] for k in ("v0", "v1", "v2")})
    return p


def _rwkv_branch(cols, v_first, p):
    b, s, nc = cols.shape
    tb = min(RWKV_BLOCK, s)
    has_vmix = v_first is not None
    row = lambda a: a.reshape(1, -1)
    tok = lambda w: pl.BlockSpec((1, tb, w), lambda i, j: (i, j, 0))
    wa2 = jnp.zeros((128, 2 * MIX_DIM), F32)
    wa2 = wa2.at[0:64, 0:MIX_DIM].set(p["w2"]).at[64:128, MIX_DIM:].set(p["a2"])
    ins = [cols] + ([v_first] if has_vmix else []) + [
        row(p["mu"]), row(p["w0"]), row(p["a0"]), wa2, p["g2"], row(p["kk"]), row(p["ka"]),
        row(p["rk"]), row(p["lnx_g"]), row(p["lnx_b"])]
    if has_vmix:
        ins += [row(p["v0"]), p["v1"], p["v2"]]
    in_specs = [tok(nc)] + ([tok(MIX_DIM)] if has_vmix else []) + [_full(a.shape) for a in ins[1 + has_vmix:]]
    y_shape = jax.ShapeDtypeStruct((b, s, MIX_DIM), F32)
    out_shape = y_shape if has_vmix else (y_shape, y_shape)
    out_specs = tok(MIX_DIM) if has_vmix else (tok(MIX_DIM), tok(MIX_DIM))
    tok_scratch = pltpu.VMEM((tb, MIX_DIM), F32)
    scratch = [pltpu.VMEM((N_HEADS, HEAD_DIM, HEAD_DIM), F32), pltpu.VMEM((1, nc), F32)] + [tok_scratch] * 9
    out = pl.pallas_call(
        functools.partial(_rwkv_kernel, has_vmix),
        grid=(b, s // tb),
        in_specs=in_specs,
        out_specs=out_specs,
        out_shape=out_shape,
        scratch_shapes=scratch,
        compiler_params=_params("parallel", "arbitrary"),
        name="rwkv7_vmix" if has_vmix else "rwkv7_first",
    )(*ins)
    return (out, v_first) if has_vmix else out


DSA_COLS_PAD = 1280
_DSA_Q, _DSA_IQ, _DSA_KIK, _DSA_VW = 0, 512, 1024, 1152


def _rope_table_kernel(pos_ref, freq_ref, cos_ref, sin_ref):
    ang = pos_ref[...] * freq_ref[...]
    lane = lax.broadcasted_iota(jnp.int32, ang.shape, 1)
    cos_ref[...] = jnp.cos(ang)
    sin_ref[...] = jnp.where(lane % HEAD_DIM < HEAD_DIM // 2, -jnp.sin(ang), jnp.sin(ang))


def _rope_tables(positions):
    t = positions.size
    tm = min(ROW_TILE, t)
    inv_freq = 1.0 / (ROPE_THETA ** (jnp.arange(0, HEAD_DIM, 2, dtype=F32) / HEAD_DIM))
    freq = jnp.tile(inv_freq, 4).reshape(1, 128)
    pos = positions.reshape(t, 1).astype(F32)
    spec = pl.BlockSpec((tm, 128), lambda i: (i, 0))
    return pl.pallas_call(
        _rope_table_kernel,
        grid=(t // tm,),
        in_specs=[pl.BlockSpec((tm, 1), lambda i: (i, 0)), _full((1, 128))],
        out_specs=(spec, spec),
        out_shape=(jax.ShapeDtypeStruct((t, 128), F32),) * 2,
        compiler_params=_params("parallel"),
        name="rope_tables",
    )(pos, freq)


def _rope128(x, cos, sin):
    lane = lax.broadcasted_iota(jnp.int32, x.shape, 1)
    half = HEAD_DIM // 2
    rot = jnp.where(lane % HEAD_DIM < half, pltpu.roll(x, 128 - half, 1), pltpu.roll(x, half, 1))
    return x * cos + rot * sin


def _dsa_prep_kernel(cols_ref, cos_ref, sin_ref, qn_ref, kn_ref, q_ref, iq_ref, kik_ref, vw_ref):
    cos, sin = cos_ref[...], sin_ref[...]
    q = cols_ref[:, _DSA_Q:_DSA_Q + MIX_DIM]
    ss = _dot(q * q, _head_ones(MIX_DIM, HEAD_DIM), HIGHEST) * (1.0 / HEAD_DIM)
    q = q * lax.rsqrt(ss + NORM_EPS) * qn_ref[...]
    kik = cols_ref[:, _DSA_KIK:_DSA_KIK + 128]
    ss = _dot(kik * kik, _head_ones(128, HEAD_DIM), HIGHEST) * (1.0 / HEAD_DIM)
    kik_ref[...] = _rope128(kik * lax.rsqrt(ss + NORM_EPS) * kn_ref[...], cos, sin)
    for j in range(MIX_DIM // 128):
        sl = slice(j * 128, (j + 1) * 128)
        q_ref[:, sl] = _rope128(q[:, sl], cos, sin) * (HEAD_DIM ** -0.5)
        iq_ref[:, sl] = _rope128(cols_ref[:, _DSA_IQ + j * 128:_DSA_IQ + (j + 1) * 128], cos, sin) * (HEAD_DIM ** -0.5)
    vw_ref[...] = cols_ref[:, _DSA_VW:_DSA_VW + 128]


def _dsa_prep(cols, cos, sin, q_norm, k_norm, idx_k_norm):
    t = cols.shape[0]
    tm = min(ROW_TILE, t)
    qn = jnp.tile(q_norm, N_HEADS).reshape(1, MIX_DIM)
    kn = jnp.concatenate([k_norm, idx_k_norm]).reshape(1, 128)
    tok = lambda w: pl.BlockSpec((tm, w), lambda i: (i, 0))
    sds = lambda w: jax.ShapeDtypeStruct((t, w), F32)
    return pl.pallas_call(
        _dsa_prep_kernel,
        grid=(t // tm,),
        in_specs=[tok(DSA_COLS_PAD), tok(128), tok(128), _full((1, MIX_DIM)), _full((1, 128))],
        out_specs=(tok(MIX_DIM), tok(MIX_DIM), tok(128), tok(128)),
        out_shape=(sds(MIX_DIM), sds(MIX_DIM), sds(128), sds(128)),
        compiler_params=_params("parallel"),
        name="dsa_prep",
    )(cols, cos, sin, qn, kn)


def _count(mask):
    return jnp.sum(jnp.where(mask, 1.0, 0.0), axis=1, keepdims=True)


def _dsa_attn_kernel(top_k, q_ref, iq_ref, vwq_ref, kik_ref, vw_ref, o_ref):
    s_len = kik_ref.shape[1]
    qb = q_ref.shape[1]
    start = pl.program_id(1) * qb
    k = kik_ref[0, :, 0:HEAD_DIM].astype(BF16)
    ik = kik_ref[0, :, HEAD_DIM:2 * HEAD_DIM].astype(BF16)
    v = vw_ref[0, :, 0:HEAD_DIM].astype(BF16)
    iw = vwq_ref[0, :, HEAD_DIM:HEAD_DIM + N_HEADS] * (N_HEADS ** -0.5)

    score = jnp.zeros((qb, s_len), F32)
    for h in range(N_HEADS):
        lg = _dot_nt(iq_ref[0, :, h * HEAD_DIM:(h + 1) * HEAD_DIM].astype(BF16), ik)
        score = score + jnp.maximum(lg, 0.0) * iw[:, h:h + 1]
    key_pos = lax.broadcasted_iota(jnp.int32, (qb, s_len), 1)
    q_pos = start + lax.broadcasted_iota(jnp.int32, (qb, 1), 0)
    limit = (q_pos // CHUNK + 1) * CHUNK
    adm = key_pos < limit
    score = jnp.where(adm, score, NEG_INF)
    score = jnp.where(score == 0.0, 0.0, score)

    bits = pltpu.bitcast(score, jnp.int32)
    key = bits ^ ((bits >> 31) & jnp.int32(0x7FFFFFFF))
    kf = float(top_k)
    theta = jnp.where(_count(key >= 0) >= kf, jnp.int32(0), jnp.int32(-2 ** 31))

    def theta_body(i, th):
        cand = th + (jnp.int32(1) << (30 - i))
        return jnp.where(_count(key >= cand) >= kf, cand, th)

    theta = lax.fori_loop(0, 31, theta_body, theta)
    gt = key > theta
    eq = key == theta
    need = kf - _count(gt)
    n_bits = int(np.ceil(np.log2(s_len))) + 1

    def tie_body(i, jb):
        cand = jb + (jnp.int32(1) << (n_bits - 1 - i))
        ok = (cand <= s_len) & (_count(eq & (key_pos < cand)) <= need)
        return jnp.where(ok, cand, jb)

    bound = lax.fori_loop(0, n_bits, tie_body, jnp.zeros((qb, 1), jnp.int32))
    mask = (gt | (eq & (key_pos < bound))) & adm

    for h in range(N_HEADS):
        s = _dot_nt(q_ref[0, :, h * HEAD_DIM:(h + 1) * HEAD_DIM].astype(BF16), k)
        s = jnp.where(mask, s, NEG_INF)
        p = jnp.exp(s - jnp.max(s, axis=1, keepdims=True))
        p = p / jnp.sum(p, axis=1, keepdims=True)
        o_ref[0, :, h * HEAD_DIM:(h + 1) * HEAD_DIM] = _dot(p.astype(BF16), v)


def _dsa_attention(q, iq, kik, vw, b, s):
    top_k = min(DSA_TOPK_MAX, s // 4)
    qb = min(Q_BLOCK, s)
    r3 = lambda a: a.reshape(b, s, a.shape[-1])
    blk = lambda w: pl.BlockSpec((1, qb, w), lambda i, j: (i, j, 0))
    seq = lambda w: pl.BlockSpec((1, s, w), lambda i, j: (i, 0, 0))
    return pl.pallas_call(
        functools.partial(_dsa_attn_kernel, top_k),
        grid=(b, s // qb),
        in_specs=[blk(MIX_DIM), blk(MIX_DIM), blk(128), seq(128), seq(128)],
        out_specs=blk(MIX_DIM),
        out_shape=jax.ShapeDtypeStruct((b, s, MIX_DIM), F32),
        compiler_params=_params("parallel", "arbitrary"),
        name="dsa_attention",
    )(r3(q), r3(iq), r3(vw), r3(kik), r3(vw))


def _dsa_weight(w_dsa):
    d = w_dsa.shape[0]
    q, k, v, iq, ik, iw = jnp.split(w_dsa, np.cumsum([512, 64, 64, 512, 64, 8])[:-1].tolist(), axis=1)
    pad = jnp.zeros((d, DSA_COLS_PAD - 1224), w_dsa.dtype)
    return jnp.concatenate([q, iq, k, ik, v, iw, pad], axis=1)


def _merge_kernel(x_ref, g_ref, ya_ref, yb_ref, wa_ref, wb_ref, wo_ref, o_ref):
    d = x_ref.shape[1]
    ga = _sigmoid(g_ref[:, 0:d])
    gb = _sigmoid(g_ref[:, d:2 * d])
    merged = ga * _bdot(ya_ref[...], wa_ref[...]) + gb * _bdot(yb_ref[...], wb_ref[...])
    o_ref[...] = x_ref[...] + _bdot(merged, wo_ref[...])


def _merge(x, gates, ya, yb, wa, wb, wo):
    t, d = x.shape
    tm = min(ROW_TILE, t)
    tok = lambda w: pl.BlockSpec((tm, w), lambda i: (i, 0))
    return pl.pallas_call(
        _merge_kernel,
        grid=(t // tm,),
        in_specs=[tok(d), tok(2 * d), tok(MIX_DIM), tok(MIX_DIM), _full(wa.shape), _full(wb.shape), _full(wo.shape)],
        out_specs=tok(d),
        out_shape=jax.ShapeDtypeStruct((t, d), F32),
        compiler_params=_params("parallel"),
        name="merge_out",
    )(x, gates, ya, yb, wa, wb, wo)


def _mem_kv_kernel(mem_ref, g_ref, wkv_ref, kn_ref, k_ref, v_ref):
    dm = k_ref.shape[2]
    kv = _dot(_rms(mem_ref[0], g_ref[...]).astype(BF16), wkv_ref[...])
    k = kv[:, 0:dm]
    ss = _dot(k * k, _head_ones(dm, MEM_HEAD_DIM), HIGHEST) * (1.0 / MEM_HEAD_DIM)
    k_ref[0] = k * lax.rsqrt(ss + NORM_EPS) * kn_ref[...]
    v_ref[0] = kv[:, dm:2 * dm]


def _mem_kv(mem, g, wkv, k_norm):
    b, m, d = mem.shape
    dm = MEM_HEADS * MEM_HEAD_DIM
    kn = jnp.tile(k_norm, MEM_HEADS).reshape(1, dm)
    spec = pl.BlockSpec((1, m, dm), lambda i: (i, 0, 0))
    return pl.pallas_call(
        _mem_kv_kernel,
        grid=(b,),
        in_specs=[pl.BlockSpec((1, m, d), lambda i: (i, 0, 0)), _full((1, d)), _full(wkv.shape), _full((1, dm))],
        out_specs=(spec, spec),
        out_shape=(jax.ShapeDtypeStruct((b, m, dm), F32),) * 2,
        compiler_params=_params("parallel"),
        name="mem_kv",
    )(mem, g.reshape(1, d), wkv, kn)


def _mem_attn_kernel(x_ref, g_ref, wq_ref, qn_ref, k_ref, v_ref, wo_ref, o_ref):
    x = x_ref[0]
    dm = wq_ref.shape[1]
    q = _dot(_rms(x, g_ref[...]).astype(BF16), wq_ref[...])
    ss = _dot(q * q, _head_ones(dm, MEM_HEAD_DIM), HIGHEST) * (1.0 / MEM_HEAD_DIM)
    q = q * lax.rsqrt(ss + NORM_EPS) * qn_ref[...] * (MEM_HEAD_DIM ** -0.5)
    outs = []
    for h in range(MEM_HEADS):
        sl = slice(h * MEM_HEAD_DIM, (h + 1) * MEM_HEAD_DIM)
        s = _dot_nt(q[:, sl].astype(BF16), k_ref[0, :, sl].astype(BF16))
        p = jnp.exp(s - jnp.max(s, axis=1, keepdims=True))
        p = p / jnp.sum(p, axis=1, keepdims=True)
        outs.append(_dot(p.astype(BF16), v_ref[0, :, sl].astype(BF16)))
    o = jnp.concatenate(outs, axis=1)
    o_ref[0] = x + _bdot(o, wo_ref[...])


def _mem_attention(x, g, wq, q_norm, k, v, wo):
    b, s, d = x.shape
    m, dm = k.shape[1], k.shape[2]
    tm = min(ROW_TILE, s)
    qn = jnp.tile(q_norm, MEM_HEADS).reshape(1, dm)
    tok = pl.BlockSpec((1, tm, d), lambda i, j: (i, j, 0))
    kv = pl.BlockSpec((1, m, dm), lambda i, j: (i, 0, 0))
    return pl.pallas_call(
        _mem_attn_kernel,
        grid=(b, s // tm),
        in_specs=[tok, _full((1, d)), _full(wq.shape), _full((1, dm)), kv, kv, _full(wo.shape)],
        out_specs=tok,
        out_shape=jax.ShapeDtypeStruct((b, s, d), F32),
        compiler_params=_params("parallel", "parallel"),
        name="mem_attention",
    )(x, g.reshape(1, d), wq, qn, k, v, wo)


def _silu(x):
    return x * _sigmoid(x)


def _ffn_kernel(x_ref, g_ref, wg_ref, wu_ref, wd_ref, o_ref, h_ref):
    j = pl.program_id(1)

    @pl.when(j == 0)
    def _():
        x = x_ref[...]
        h_ref[...] = _rms(x, g_ref[...]).astype(BF16)
        o_ref[...] = x

    h = h_ref[...]
    act = _silu(_dot(h, wg_ref[...])) * _dot(h, wu_ref[...])
    o_ref[...] += _bdot(act, wd_ref[...])


def _ffn(x, g, wg, wu, wd, n_f=2):
    t, d = x.shape
    f = wg.shape[1]
    tm = min(ROW_TILE, t)
    tf = f // n_f
    tok = pl.BlockSpec((tm, d), lambda i, j: (i, 0))
    return pl.pallas_call(
        _ffn_kernel,
        grid=(t // tm, n_f),
        in_specs=[tok, _full((1, d)), pl.BlockSpec((d, tf), lambda i, j: (0, j)),
                  pl.BlockSpec((d, tf), lambda i, j: (0, j)), pl.BlockSpec((tf, d), lambda i, j: (j, 0))],
        out_specs=tok,
        out_shape=jax.ShapeDtypeStruct((t, d), F32),
        scratch_shapes=[pltpu.VMEM((tm, d), BF16)],
        compiler_params=_params("parallel", "arbitrary"),
        name="ffn_swiglu",
    )(x, g.reshape(1, d), wg, wu, wd)


def _moe_kernel(x_ref, g_ref, r_ref, b_ref, wg_ref, wu_ref, wd_ref, o_ref, h_ref, gate_ref):
    e = pl.program_id(1)

    @pl.when(e == 0)
    def _():
        x = x_ref[...]
        hn = _rms(x, g_ref[...])
        h_ref[...] = hn.astype(BF16)
        o_ref[...] = x
        logits = _dot(hn, r_ref[...], HIGHEST) + b_ref[...]
        lane = lax.broadcasted_iota(jnp.int32, logits.shape, 1)
        logits = jnp.where(lane < N_EXPERTS, logits, -jnp.inf)
        m1 = jnp.max(logits, axis=1, keepdims=True)
        i1 = jnp.min(jnp.where(logits == m1, lane, 128), axis=1, keepdims=True)
        rest = jnp.where(lane == i1, -jnp.inf, logits)
        m2 = jnp.max(rest, axis=1, keepdims=True)
        i2 = jnp.min(jnp.where(rest == m2, lane, 128), axis=1, keepdims=True)
        e2 = jnp.exp(m2 - m1)
        w1 = 1.0 / (1.0 + e2)
        w2 = e2 / (1.0 + e2)
        gate_ref[...] = jnp.where(lane == i1, w1, 0.0) + jnp.where(lane == i2, w2, 0.0)

    h = h_ref[...]
    lane = lax.broadcasted_iota(jnp.int32, gate_ref.shape, 1)
    gate = jnp.sum(jnp.where(lane == e, gate_ref[...], 0.0), axis=1, keepdims=True)
    act = _silu(_dot(h, wg_ref[0])) * _dot(h, wu_ref[0])
    o_ref[...] += gate * _bdot(act, wd_ref[0])


def _moe(x, g, router, bias, wg, wu, wd):
    t, d = x.shape
    n_e, _, f = wg.shape
    tm = min(ROW_TILE, t)
    r_pad = jnp.zeros((d, 128), F32).at[:, 0:n_e].set(router)
    b_pad = jnp.zeros((1, 128), F32).at[0, 0:n_e].set(bias)
    tok = pl.BlockSpec((tm, d), lambda i, j: (i, 0))
    return pl.pallas_call(
        _moe_kernel,
        grid=(t // tm, n_e),
        in_specs=[tok, _full((1, d)), _full((d, 128)), _full((1, 128)),
                  pl.BlockSpec((1, d, f), lambda i, j: (j, 0, 0)), pl.BlockSpec((1, d, f), lambda i, j: (j, 0, 0)),
                  pl.BlockSpec((1, f, d), lambda i, j: (j, 0, 0))],
        out_specs=tok,
        out_shape=jax.ShapeDtypeStruct((t, d), F32),
        scratch_shapes=[pltpu.VMEM((tm, d), BF16), pltpu.VMEM((tm, 128), F32)],
        compiler_params=_params("parallel", "arbitrary"),
        name="moe_swiglu",
    )(x, g.reshape(1, d), r_pad, b_pad, wg, wu, wd)


def kernel(x, mem, positions, norm_mix, w_in, rwkv_mu, rwkv_w0, rwkv_w2, rwkv_a0, rwkv_a2, rwkv_g2, rwkv_v0, rwkv_v1, rwkv_v2, rwkv_kk, rwkv_ka, rwkv_rk, rwkv_lnx_g, rwkv_lnx_b, dsa_q_norm, dsa_k_norm, idx_k_norm, w_branch_a, w_branch_b, w_out, norm_mem, mem_tok_norm, mem_wq, mem_wkv, mem_q_norm, mem_k_norm, mem_wo, norm_ffn, ffn_wg, ffn_wu, ffn_wd, moe_router, moe_bias, moe_wg, moe_wu, moe_wd):
    w = dict(rwkv_mu=rwkv_mu, rwkv_w0=rwkv_w0, rwkv_w2=rwkv_w2, rwkv_a0=rwkv_a0, rwkv_a2=rwkv_a2,
             rwkv_g2=rwkv_g2, rwkv_v0=rwkv_v0, rwkv_v1=rwkv_v1, rwkv_v2=rwkv_v2, rwkv_kk=rwkv_kk,
             rwkv_ka=rwkv_ka, rwkv_rk=rwkv_rk, rwkv_lnx_g=rwkv_lnx_g, rwkv_lnx_b=rwkv_lnx_b)
    b, s, d = x.shape
    t = b * s
    depth = w_in.shape[0]
    n_a = w_in.shape[2] - 1224 - 2 * d
    bf = lambda a: a.astype(BF16)
    cos, sin = _rope_tables(positions)
    x = x.reshape(t, d)
    v_first = None
    for l in range(depth):
        w_a = bf(w_in[l, :, 0:n_a])
        w_b = bf(_dsa_weight(w_in[l, :, n_a:n_a + 1224]))
        w_g = bf(w_in[l, :, n_a + 1224:])
        cols_a = _norm_matmul(x, norm_mix[l], w_a)
        cols_b = _norm_matmul(x, norm_mix[l], w_b)
        gates = _norm_matmul(x, norm_mix[l], w_g)
        y_a, v_first = _rwkv_branch(cols_a.reshape(b, s, n_a), v_first, _rwkv_params(w, l))
        q, iq, kik, vw = _dsa_prep(cols_b, cos, sin, dsa_q_norm[l], dsa_k_norm[l], idx_k_norm[l])
        y_b = _dsa_attention(q, iq, kik, vw, b, s)
        x = _merge(x, gates, y_a.reshape(t, MIX_DIM), y_b.reshape(t, MIX_DIM),
                   bf(w_branch_a[l]), bf(w_branch_b[l]), bf(w_out[l]))
        mk, mv = _mem_kv(mem, mem_tok_norm[l], bf(mem_wkv[l]), mem_k_norm[l])
        x = _mem_attention(x.reshape(b, s, d), norm_mem[l], bf(mem_wq[l]), mem_q_norm[l], mk, mv,
                           bf(mem_wo[l])).reshape(t, d)
        j = l // 2
        if l % 2 == 0:
            x = _ffn(x, norm_ffn[l], bf(ffn_wg[j]), bf(ffn_wu[j]), bf(ffn_wd[j]))
        else:
            x = _moe(x, norm_ffn[l], moe_router[j], moe_bias[j], bf(moe_wg[j]), bf(moe_wu[j]), bf(moe_wd[j]))
    return x.reshape(b, s, d)


def _dsa_test(cols_b, w, l):
    b, s, _ = cols_b.shape
    perm = _dsa_weight(jnp.eye(1224, dtype=F32))
    cols = (cols_b.reshape(b * s, 1224) @ perm)
    cos, sin = _rope_tables(w["positions"])
    q, iq, kik, vw = _dsa_prep(cols, cos, sin, w["dsa_q_norm"][l], w["dsa_k_norm"][l], w["idx_k_norm"][l])
    return _dsa_attention(q, iq, kik, vw, b, s)
```

```python
import functools

import jax
import jax.numpy as jnp
import numpy as np
from jax import lax
from jax.experimental import pallas as pl
from jax.experimental.pallas import tpu as pltpu

F32 = jnp.float32
BF16 = jnp.bfloat16
HIGHEST = lax.Precision.HIGHEST

NORM_EPS = 1e-6
NEG_INF = -1e30
ROPE_THETA = 10000.0
LOG2_E = float(np.log2(np.e))

CHUNK = 64
Q_BLOCK = 128
DSA_TOPK_MAX = 256
DSA_KEY_CLASSES = 4
HEAD_DIM = 64
N_HEADS = 8
MIX_DIM = N_HEADS * HEAD_DIM
RWKV_LNX_EPS = 1e-5 * HEAD_DIM
MEM_HEADS = 4
MEM_HEAD_DIM = 128
N_EXPERTS = 8

RWKV_CHUNK = 64
RWKV_BLOCK = 256
ROW_TILE = 512
VMEM_LIMIT = 48 * 1024 * 1024


def _params(*sem):
    return pltpu.CompilerParams(dimension_semantics=sem, vmem_limit_bytes=VMEM_LIMIT)


def _full(shape):
    nd = len(shape)
    return pl.BlockSpec(shape, lambda *_: (0,) * nd)


def _dot(a, b, precision=None):
    return jnp.dot(a, b, preferred_element_type=F32, precision=precision)


def _dot_nt(a, b, precision=None):
    return lax.dot_general(a, b, (((1,), (1,)), ((), ())), preferred_element_type=F32,
                           precision=precision)


def _dot_tn(a, b, precision=None):
    return lax.dot_general(a, b, (((0,), (0,)), ((), ())), preferred_element_type=F32,
                           precision=precision)


def _bdot(a, b):
    return _dot(a.astype(BF16), b.astype(BF16))


def _split(x):
    hi = x.astype(BF16)
    return hi, (x - hi.astype(F32)).astype(BF16)


def _dot_x2(x, m):
    hi, lo = _split(x)
    return _dot(hi, m) + _dot(lo, m)


def _dot_x3(x, w_hi, w_lo):
    hi, lo = _split(x)
    return _dot(hi, w_hi) + (_dot(lo, w_hi) + _dot(hi, w_lo))


def _rms(x, g):
    return x * lax.rsqrt(jnp.mean(x * x, axis=-1, keepdims=True) + NORM_EPS) * g


def _sigmoid(x):
    return 1.0 / (1.0 + jnp.exp(-x))


def _head_ones(width, head):
    r = lax.broadcasted_iota(jnp.int32, (width, width), 0) // head
    c = lax.broadcasted_iota(jnp.int32, (width, width), 1) // head
    return (r == c).astype(BF16)


def _norm_matmul_kernel(x_ref, g_ref, *refs):
    n = len(refs) // 2
    h = _rms(x_ref[...], g_ref[...]).astype(BF16)
    for w_ref, o_ref in zip(refs[:n], refs[n:]):
        o_ref[...] = _dot(h, w_ref[...])


def _norm_matmul(x, g, ws, tm=ROW_TILE // 2):
    t, d = x.shape
    tm = min(tm, t)
    return pl.pallas_call(
        _norm_matmul_kernel,
        grid=(t // tm,),
        in_specs=[pl.BlockSpec((tm, d), lambda i: (i, 0)), _full((1, d))] + [_full(w.shape) for w in ws],
        out_specs=tuple(pl.BlockSpec((tm, w.shape[1]), lambda i: (i, 0)) for w in ws),
        out_shape=tuple(jax.ShapeDtypeStruct((t, w.shape[1]), F32) for w in ws),
        compiler_params=_params("parallel"),
        name="norm_matmul",
    )(x, g.reshape(1, d), *ws)


def _rwkv_kernel(has_vmix, *refs):
    if has_vmix:
        (cols_ref, vf_ref, mu_ref, w0_ref, a0_ref, wah_ref, wal_ref, g2_ref, kk_ref, ka_ref,
         rk_ref, lg_ref, lb_ref, v0_ref, v1_ref, v2_ref,
         y_ref, st_ref, prev_ref, at_ref, rt_ref, bt_ref, kt_ref, bh_ref, kh_ref, vv_ref,
         gc_ref, bonus_ref, gate_ref, yy_ref) = refs
        vout_ref = None
    else:
        (cols_ref, mu_ref, w0_ref, a0_ref, wah_ref, wal_ref, g2_ref, kk_ref, ka_ref,
         rk_ref, lg_ref, lb_ref,
         y_ref, vout_ref, st_ref, prev_ref, at_ref, rt_ref, bt_ref, kt_ref, bh_ref, kh_ref, vv_ref,
         gc_ref, bonus_ref, gate_ref, yy_ref) = refs

    tb = cols_ref.shape[1]
    c = RWKV_CHUNK
    n_chunks = tb // c
    hd = HEAD_DIM

    @pl.when(pl.program_id(1) == 0)
    def _():
        st_ref[...] = jnp.zeros_like(st_ref)
        prev_ref[...] = jnp.zeros_like(prev_ref)

    cols = cols_ref[0]
    row = lax.broadcasted_iota(jnp.int32, cols.shape, 0)
    shifted = jnp.where(row == 0, prev_ref[...], pltpu.roll(cols, 1, 0))
    prev_ref[...] = cols[tb - 1:tb, :]
    mixed = cols + (shifted - cols) * mu_ref[...]

    r = mixed[:, 0:MIX_DIM]
    k = mixed[:, MIX_DIM:2 * MIX_DIM]
    v = mixed[:, 2 * MIX_DIM:3 * MIX_DIM]
    wa = mixed[:, 3 * MIX_DIM:3 * MIX_DIM + 128]
    gd = mixed[:, 3 * MIX_DIM + 128:3 * MIX_DIM + 256]
    lane = lax.broadcasted_iota(jnp.int32, wa.shape, 1)
    wa = jnp.where(lane < 64, jnp.tanh(wa), wa)
    wa_out = _dot_x3(wa, wah_ref[...], wal_ref[...])
    u_dec = w0_ref[...] + wa_out[:, 0:MIX_DIM]
    a = _sigmoid(a0_ref[...] + wa_out[:, MIX_DIM:2 * MIX_DIM])
    gate_ref[...] = _bdot(_sigmoid(gd), g2_ref[...])
    if has_vmix:
        vm = _bdot(_bdot(v, v1_ref[...]), v2_ref[...])
        v = v + (vf_ref[0] - v) * _sigmoid(v0_ref[...] + vm)
    else:
        vout_ref[0] = v

    hb = _head_ones(MIX_DIM, hd)
    kk = k * kk_ref[...]
    kk = kk / jnp.maximum(jnp.sqrt(_dot_x2(kk * kk, hb)), 1e-12)
    k = k * (1.0 + (a - 1.0) * ka_ref[...])
    bonus_ref[...] = _bdot(r * k * rk_ref[...], hb) * v
    ll = -_sigmoid(u_dec) * float(np.exp(-0.5))

    ri = lax.broadcasted_iota(jnp.int32, (tb, tb), 0)
    ci = lax.broadcasted_iota(jnp.int32, (tb, tb), 1)
    same = (ri // c) == (ci // c)
    strict_f = (same & (ri > ci)).astype(F32)
    incl_f = (same & (ri >= ci)).astype(F32)
    ll_hi, ll_lo = _split(ll)
    tri = incl_f.astype(BF16)
    blk = same.astype(BF16)
    cum = _dot(tri, ll_hi) + _dot(tri, ll_lo)
    tot = _dot(blk, ll_hi) + _dot(blk, ll_lo)
    er = lax.broadcasted_iota(jnp.int32, (tb, n_chunks * 128), 0) // c
    ec = lax.broadcasted_iota(jnp.int32, (tb, n_chunks * 128), 1) // 128
    sel = (er == ec).astype(BF16)
    gc_ref[...] = jnp.exp(_dot_tn(ll_hi, sel) + _dot_tn(ll_lo, sel))
    g_inv = jnp.exp(-cum)
    g_tail = jnp.exp(tot - cum)
    beta = kk * a
    at_ref[...] = (-kk * jnp.exp(cum - ll)).astype(BF16)
    rt_ref[...] = (r * jnp.exp(cum)).astype(BF16)
    bt_ref[...] = (beta * g_inv).astype(BF16)
    kt_ref[...] = (k * g_inv).astype(BF16)
    bh_ref[...] = (beta * g_tail).astype(BF16)
    kh_ref[...] = (k * g_tail).astype(BF16)
    vv_ref[...] = v.astype(BF16)

    n_levels = int(np.log2(c))
    for h in range(N_HEADS):
        cs = slice(h * hd, (h + 1) * hd)
        a_t = at_ref[:, cs]
        r_t = rt_ref[:, cs]
        b_t = bt_ref[:, cs]
        k_t = kt_ref[:, cs]
        v_c = vv_ref[:, cs]
        n_mat = (_dot_nt(a_t, b_t) * strict_f).astype(BF16)
        m_mat = (_dot_nt(a_t, k_t) * strict_f).astype(BF16)
        rb = (_dot_nt(r_t, b_t) * incl_f).astype(BF16)
        rk = (_dot_nt(r_t, k_t) * incl_f).astype(BF16)
        z = jnp.concatenate([a_t.astype(F32), _dot(m_mat, v_c)], axis=1)
        p = n_mat
        for lev in range(n_levels):
            z = z + _dot(p, z.astype(BF16))
            if lev < n_levels - 1:
                p = _dot(p, p).astype(BF16)
        z_b = z.astype(BF16)
        w = _dot(rb, z_b)
        r_bar = (r_t.astype(F32) + w[:, 0:hd]).astype(BF16)
        y0 = w[:, hd:2 * hd] + _dot(rk, v_c)
        st = st_ref[h]
        for ch in range(n_chunks):
            rows = slice(ch * c, (ch + 1) * c)
            pd = _dot_tn(bh_ref[rows, cs], z_b[rows])
            d2 = _dot_tn(kh_ref[rows, cs], v_c[rows])
            lhs = jnp.concatenate([r_bar[rows], pd[:, 0:hd].astype(BF16)], axis=0)
            res = _dot(lhs, st.astype(BF16))
            yy_ref[rows, cs] = res[0:c] + y0[rows]
            g_col = gc_ref[h * hd:(h + 1) * hd, ch * 128:ch * 128 + hd]
            st = st * g_col + res[c:2 * c] + pd[:, hd:2 * hd] + d2
        st_ref[h] = st

    y = yy_ref[...]
    mean = _bdot(y, hb) * (1.0 / hd)
    yc = y - mean
    var = _bdot(yc * yc, hb) * (1.0 / hd)
    yn = yc * lax.rsqrt(var + RWKV_LNX_EPS) * lg_ref[...] + lb_ref[...]
    y_ref[0] = ((yn + bonus_ref[...]) * gate_ref[...]).astype(y_ref.dtype)


def _rwkv_params(w, l):
    p = {k: w["rwkv_" + k][l] for k in ("mu", "w0", "w2", "a0", "a2", "g2", "kk", "ka", "rk", "lnx_g", "lnx_b")}
    if l > 0:
        p.update({k: w["rwkv_" + k][l - 1] for k in ("v0", "v1", "v2")})
    return p


def _rwkv_branch(cols, v_first, p):
    b, s, nc = cols.shape
    tb = min(RWKV_BLOCK, s)
    has_vmix = v_first is not None
    row = lambda a: a.reshape(1, -1)
    pair = lambda a: list(_split(a))
    tok = lambda w: pl.BlockSpec((1, tb, w), lambda i, j: (i, j, 0))
    wa2 = jnp.zeros((128, 2 * MIX_DIM), F32)
    wa2 = wa2.at[0:64, 0:MIX_DIM].set(p["w2"]).at[64:128, MIX_DIM:].set(p["a2"])
    ins = [cols] + ([v_first] if has_vmix else []) + [
        row(p["mu"]), row(p["w0"]), row(p["a0"])] + pair(wa2) + [p["g2"].astype(BF16)] + [
        row(p["kk"]), row(p["ka"]), row(p["rk"]), row(p["lnx_g"]), row(p["lnx_b"])]
    if has_vmix:
        ins += [row(p["v0"]), p["v1"].astype(BF16), p["v2"].astype(BF16)]
    in_specs = [tok(nc)] + ([tok(MIX_DIM)] if has_vmix else []) + [_full(a.shape) for a in ins[1 + has_vmix:]]
    y_shape = jax.ShapeDtypeStruct((b, s, MIX_DIM), F32)
    out_shape = y_shape if has_vmix else (y_shape, y_shape)
    out_specs = tok(MIX_DIM) if has_vmix else (tok(MIX_DIM), tok(MIX_DIM))
    tok_bf = pltpu.VMEM((tb, MIX_DIM), BF16)
    tok_f32 = pltpu.VMEM((tb, MIX_DIM), F32)
    scratch = ([pltpu.VMEM((N_HEADS, HEAD_DIM, HEAD_DIM), F32), pltpu.VMEM((1, nc), F32)] + [tok_bf] * 7
               + [pltpu.VMEM((MIX_DIM, (tb // RWKV_CHUNK) * 128), F32)] + [tok_f32] * 3)
    out = pl.pallas_call(
        functools.partial(_rwkv_kernel, has_vmix),
        grid=(b, s // tb),
        in_specs=in_specs,
        out_specs=out_specs,
        out_shape=out_shape,
        scratch_shapes=scratch,
        compiler_params=_params("parallel", "arbitrary"),
        name="rwkv7_vmix" if has_vmix else "rwkv7_first",
    )(*ins)
    return (out, v_first) if has_vmix else out


DSA_COLS_PAD = 1280
_DSA_Q, _DSA_IQ, _DSA_KIK, _DSA_VW = 0, 512, 1024, 1152


def _rope_table_kernel(pos_ref, freq_ref, cos_ref, sin_ref):
    ang = pos_ref[...] * freq_ref[...]
    lane = lax.broadcasted_iota(jnp.int32, ang.shape, 1)
    cos_ref[...] = jnp.cos(ang)
    sin_ref[...] = jnp.where(lane % HEAD_DIM < HEAD_DIM // 2, -jnp.sin(ang), jnp.sin(ang))


def _rope_tables(positions):
    t = positions.size
    tm = min(ROW_TILE, t)
    inv_freq = 1.0 / (ROPE_THETA ** (jnp.arange(0, HEAD_DIM, 2, dtype=F32) / HEAD_DIM))
    freq = jnp.tile(inv_freq, 4).reshape(1, 128)
    pos = positions.reshape(t, 1).astype(F32)
    spec = pl.BlockSpec((tm, 128), lambda i: (i, 0))
    return pl.pallas_call(
        _rope_table_kernel,
        grid=(t // tm,),
        in_specs=[pl.BlockSpec((tm, 1), lambda i: (i, 0)), _full((1, 128))],
        out_specs=(spec, spec),
        out_shape=(jax.ShapeDtypeStruct((t, 128), F32),) * 2,
        compiler_params=_params("parallel"),
        name="rope_tables",
    )(pos, freq)


def _rope128(x, cos, sin):
    lane = lax.broadcasted_iota(jnp.int32, x.shape, 1)
    half = HEAD_DIM // 2
    rot = jnp.where(lane % HEAD_DIM < half, pltpu.roll(x, 128 - half, 1), pltpu.roll(x, half, 1))
    return x * cos + rot * sin


def _dsa_prep_kernel(cols_ref, cos_ref, sin_ref, qn_ref, kn_ref, q_ref, iq_ref, kik_ref, vw_ref):
    cos, sin = cos_ref[...], sin_ref[...]
    q = cols_ref[:, _DSA_Q:_DSA_Q + MIX_DIM]
    ss = _dot_x2(q * q, _head_ones(MIX_DIM, HEAD_DIM)) * (1.0 / HEAD_DIM)
    q = q * lax.rsqrt(ss + NORM_EPS) * qn_ref[...]
    kik = cols_ref[:, _DSA_KIK:_DSA_KIK + 128]
    ss = _dot_x2(kik * kik, _head_ones(128, HEAD_DIM)) * (1.0 / HEAD_DIM)
    kik_ref[...] = _rope128(kik * lax.rsqrt(ss + NORM_EPS) * kn_ref[...], cos, sin)
    for j in range(MIX_DIM // 128):
        sl = slice(j * 128, (j + 1) * 128)
        q_ref[:, sl] = _rope128(q[:, sl], cos, sin) * (HEAD_DIM ** -0.5 * LOG2_E)
        iq_ref[:, sl] = _rope128(cols_ref[:, _DSA_IQ + j * 128:_DSA_IQ + (j + 1) * 128], cos, sin) * (HEAD_DIM ** -0.5)
    vw_ref[...] = cols_ref[:, _DSA_VW:_DSA_VW + 128]


def _dsa_prep(cols, cos, sin, q_norm, k_norm, idx_k_norm):
    t = cols.shape[0]
    tm = min(ROW_TILE, t)
    qn = jnp.tile(q_norm, N_HEADS).reshape(1, MIX_DIM)
    kn = jnp.concatenate([k_norm, idx_k_norm]).reshape(1, 128)
    tok = lambda w: pl.BlockSpec((tm, w), lambda i: (i, 0))
    sds = lambda w: jax.ShapeDtypeStruct((t, w), F32)
    return pl.pallas_call(
        _dsa_prep_kernel,
        grid=(t // tm,),
        in_specs=[tok(DSA_COLS_PAD), tok(128), tok(128), _full((1, MIX_DIM)), _full((1, 128))],
        out_specs=(tok(MIX_DIM), tok(MIX_DIM), tok(128), tok(128)),
        out_shape=(sds(MIX_DIM), sds(MIX_DIM), sds(128), sds(128)),
        compiler_params=_params("parallel"),
        name="dsa_prep",
    )(cols, cos, sin, qn, kn)


def _count(mask):
    return jnp.sum(jnp.where(mask, 1.0, 0.0), axis=1, keepdims=True)


def _dsa_block(top_k, n_keys, q_ref, iq_ref, vwq_ref, kik_ref, vw_ref, o_ref):
    qb = q_ref.shape[1]
    start = pl.program_id(1) * qb
    k = kik_ref[0, 0:n_keys, 0:HEAD_DIM].astype(BF16)
    ik = kik_ref[0, 0:n_keys, HEAD_DIM:2 * HEAD_DIM].astype(BF16)
    v = vw_ref[0, 0:n_keys, 0:HEAD_DIM].astype(BF16)
    iw = vwq_ref[0, :, HEAD_DIM:HEAD_DIM + N_HEADS] * (N_HEADS ** -0.5)

    score = jnp.zeros((qb, n_keys), F32)
    for h in range(N_HEADS):
        lg = _dot_nt(iq_ref[0, :, h * HEAD_DIM:(h + 1) * HEAD_DIM].astype(BF16), ik)
        score = score + jnp.maximum(lg, 0.0) * iw[:, h:h + 1]
    key_pos = lax.broadcasted_iota(jnp.int32, (qb, n_keys), 1)
    q_pos = start + lax.broadcasted_iota(jnp.int32, (qb, 1), 0)
    limit = (q_pos // CHUNK + 1) * CHUNK
    adm = key_pos < limit
    score = jnp.where(adm, score, NEG_INF)
    score = jnp.where(score == 0.0, 0.0, score)

    bits = pltpu.bitcast(score, jnp.int32)
    key = bits ^ ((bits >> 31) & jnp.int32(0x7FFFFFFF))
    kf = float(top_k)
    theta = jnp.where(_count(key >= 0) >= kf, jnp.int32(0), jnp.int32(-2 ** 31))

    def theta_body(i, th):
        cand = th + (jnp.int32(1) << (30 - i))
        return jnp.where(_count(key >= cand) >= kf, cand, th)

    theta = lax.fori_loop(0, 31, theta_body, theta)
    gt = key > theta
    eq = key == theta
    n_gt = _count(gt)
    need = kf - n_gt
    n_bits = int(np.ceil(np.log2(n_keys))) + 1

    def tie_search():
        def tie_body(i, jb):
            cand = jb + (jnp.int32(1) << (n_bits - 1 - i))
            ok = (cand <= n_keys) & (_count(eq & (key_pos < cand)) <= need)
            return jnp.where(ok, cand, jb)

        return lax.fori_loop(0, n_bits, tie_body, jnp.zeros((qb, 1), jnp.int32))

    has_tie = jnp.max(n_gt + _count(eq)) > kf
    bound = lax.cond(has_tie, tie_search, lambda: jnp.full((qb, 1), n_keys, jnp.int32))
    bias = jnp.where((gt | (eq & (key_pos < bound))) & adm, 0.0, NEG_INF)

    for h in range(N_HEADS):
        s = _dot_nt(q_ref[0, :, h * HEAD_DIM:(h + 1) * HEAD_DIM].astype(BF16), k) + bias
        p = jnp.exp2(s - jnp.max(s, axis=1, keepdims=True))
        o = _dot(p.astype(BF16), v) / jnp.sum(p, axis=1, keepdims=True)
        o_ref[0, :, h * HEAD_DIM:(h + 1) * HEAD_DIM] = o


def _dsa_attn_kernel(top_k, n_classes, q_ref, iq_ref, vwq_ref, kik_ref, vw_ref, o_ref):
    qb = q_ref.shape[1]
    per_class = (kik_ref.shape[1] // qb) // n_classes
    cls = pl.program_id(1) // per_class
    for n in range(n_classes):
        pl.when(cls == n)(functools.partial(
            _dsa_block, top_k, (n + 1) * per_class * qb, q_ref, iq_ref, vwq_ref, kik_ref, vw_ref, o_ref))


def _dsa_attention(q, iq, kik, vw, b, s):
    top_k = min(DSA_TOPK_MAX, s // 4)
    qb = min(Q_BLOCK, s)
    n_classes = min(DSA_KEY_CLASSES, s // qb)
    r3 = lambda a: a.reshape(b, s, a.shape[-1])
    blk = lambda w: pl.BlockSpec((1, qb, w), lambda i, j: (i, j, 0))
    seq = lambda w: pl.BlockSpec((1, s, w), lambda i, j: (i, 0, 0))
    return pl.pallas_call(
        functools.partial(_dsa_attn_kernel, top_k, n_classes),
        grid=(b, s // qb),
        in_specs=[blk(MIX_DIM), blk(MIX_DIM), blk(128), seq(128), seq(128)],
        out_specs=blk(MIX_DIM),
        out_shape=jax.ShapeDtypeStruct((b, s, MIX_DIM), F32),
        compiler_params=_params("parallel", "arbitrary"),
        name="dsa_attention",
    )(r3(q), r3(iq), r3(vw), r3(kik), r3(vw))


def _dsa_weight(w_dsa):
    d = w_dsa.shape[0]
    q, k, v, iq, ik, iw = jnp.split(w_dsa, np.cumsum([512, 64, 64, 512, 64, 8])[:-1].tolist(), axis=1)
    pad = jnp.zeros((d, DSA_COLS_PAD - 1224), w_dsa.dtype)
    return jnp.concatenate([q, iq, k, ik, v, iw, pad], axis=1)


def _merge_kernel(x_ref, g_ref, ya_ref, yb_ref, wa_ref, wb_ref, wo_ref, o_ref):
    d = x_ref.shape[1]
    ga = _sigmoid(g_ref[:, 0:d])
    gb = _sigmoid(g_ref[:, d:2 * d])
    merged = ga * _bdot(ya_ref[...], wa_ref[...]) + gb * _bdot(yb_ref[...], wb_ref[...])
    o_ref[...] = x_ref[...] + _bdot(merged, wo_ref[...])


def _merge(x, gates, ya, yb, wa, wb, wo):
    t, d = x.shape
    tm = min(ROW_TILE, t)
    tok = lambda w: pl.BlockSpec((tm, w), lambda i: (i, 0))
    return pl.pallas_call(
        _merge_kernel,
        grid=(t // tm,),
        in_specs=[tok(d), tok(2 * d), tok(MIX_DIM), tok(MIX_DIM), _full(wa.shape), _full(wb.shape), _full(wo.shape)],
        out_specs=tok(d),
        out_shape=jax.ShapeDtypeStruct((t, d), F32),
        compiler_params=_params("parallel"),
        name="merge_out",
    )(x, gates, ya, yb, wa, wb, wo)


def _mem_kv_kernel(mem_ref, g_ref, wkv_ref, kn_ref, k_ref, v_ref):
    dm = k_ref.shape[2]
    kv = _dot(_rms(mem_ref[0], g_ref[...]).astype(BF16), wkv_ref[...])
    k = kv[:, 0:dm]
    ss = _dot_x2(k * k, _head_ones(dm, MEM_HEAD_DIM)) * (1.0 / MEM_HEAD_DIM)
    k_ref[0] = k * lax.rsqrt(ss + NORM_EPS) * kn_ref[...]
    v_ref[0] = kv[:, dm:2 * dm]


def _mem_kv(mem, g, wkv, k_norm):
    b, m, d = mem.shape
    dm = MEM_HEADS * MEM_HEAD_DIM
    kn = jnp.tile(k_norm, MEM_HEADS).reshape(1, dm)
    spec = pl.BlockSpec((1, m, dm), lambda i: (i, 0, 0))
    return pl.pallas_call(
        _mem_kv_kernel,
        grid=(b,),
        in_specs=[pl.BlockSpec((1, m, d), lambda i: (i, 0, 0)), _full((1, d)), _full(wkv.shape), _full((1, dm))],
        out_specs=(spec, spec),
        out_shape=(jax.ShapeDtypeStruct((b, m, dm), F32),) * 2,
        compiler_params=_params("parallel"),
        name="mem_kv",
    )(mem, g.reshape(1, d), wkv, kn)


def _mem_attn_kernel(x_ref, g_ref, wq_ref, qn_ref, k_ref, v_ref, wo_ref, o_ref):
    x = x_ref[0]
    dm = wq_ref.shape[1]
    q = _dot(_rms(x, g_ref[...]).astype(BF16), wq_ref[...])
    ss = _dot_x2(q * q, _head_ones(dm, MEM_HEAD_DIM)) * (1.0 / MEM_HEAD_DIM)
    q = q * lax.rsqrt(ss + NORM_EPS) * qn_ref[...] * (MEM_HEAD_DIM ** -0.5)
    outs = []
    for h in range(MEM_HEADS):
        sl = slice(h * MEM_HEAD_DIM, (h + 1) * MEM_HEAD_DIM)
        s = _dot_nt(q[:, sl].astype(BF16), k_ref[0, :, sl].astype(BF16))
        p = jnp.exp(s - jnp.max(s, axis=1, keepdims=True))
        p = p / jnp.sum(p, axis=1, keepdims=True)
        outs.append(_dot(p.astype(BF16), v_ref[0, :, sl].astype(BF16)))
    o = jnp.concatenate(outs, axis=1)
    o_ref[0] = x + _bdot(o, wo_ref[...])


def _mem_attention(x, g, wq, q_norm, k, v, wo):
    b, s, d = x.shape
    m, dm = k.shape[1], k.shape[2]
    tm = min(ROW_TILE, s)
    qn = jnp.tile(q_norm, MEM_HEADS).reshape(1, dm)
    tok = pl.BlockSpec((1, tm, d), lambda i, j: (i, j, 0))
    kv = pl.BlockSpec((1, m, dm), lambda i, j: (i, 0, 0))
    return pl.pallas_call(
        _mem_attn_kernel,
        grid=(b, s // tm),
        in_specs=[tok, _full((1, d)), _full(wq.shape), _full((1, dm)), kv, kv, _full(wo.shape)],
        out_specs=tok,
        out_shape=jax.ShapeDtypeStruct((b, s, d), F32),
        compiler_params=_params("parallel", "parallel"),
        name="mem_attention",
    )(x, g.reshape(1, d), wq, qn, k, v, wo)


def _silu(x):
    return x * _sigmoid(x)


def _ffn_kernel(x_ref, g_ref, wg_ref, wu_ref, wd_ref, o_ref, h_ref):
    j = pl.program_id(1)

    @pl.when(j == 0)
    def _():
        x = x_ref[...]
        h_ref[...] = _rms(x, g_ref[...]).astype(BF16)
        o_ref[...] = x

    h = h_ref[...]
    act = _silu(_dot(h, wg_ref[...])) * _dot(h, wu_ref[...])
    o_ref[...] += _bdot(act, wd_ref[...])


def _ffn(x, g, wg, wu, wd, n_f=2):
    t, d = x.shape
    f = wg.shape[1]
    tm = min(ROW_TILE, t)
    tf = f // n_f
    tok = pl.BlockSpec((tm, d), lambda i, j: (i, 0))
    return pl.pallas_call(
        _ffn_kernel,
        grid=(t // tm, n_f),
        in_specs=[tok, _full((1, d)), pl.BlockSpec((d, tf), lambda i, j: (0, j)),
                  pl.BlockSpec((d, tf), lambda i, j: (0, j)), pl.BlockSpec((tf, d), lambda i, j: (j, 0))],
        out_specs=tok,
        out_shape=jax.ShapeDtypeStruct((t, d), F32),
        scratch_shapes=[pltpu.VMEM((tm, d), BF16)],
        compiler_params=_params("parallel", "arbitrary"),
        name="ffn_swiglu",
    )(x, g.reshape(1, d), wg, wu, wd)


def _moe_kernel(x_ref, g_ref, r_ref, b_ref, wg_ref, wu_ref, wd_ref, o_ref, h_ref, gate_ref):
    e = pl.program_id(1)

    @pl.when(e == 0)
    def _():
        x = x_ref[...]
        hn = _rms(x, g_ref[...])
        h_ref[...] = hn.astype(BF16)
        o_ref[...] = x
        logits = _dot(hn, r_ref[...], HIGHEST) + b_ref[...]
        lane = lax.broadcasted_iota(jnp.int32, logits.shape, 1)
        logits = jnp.where(lane < N_EXPERTS, logits, -jnp.inf)
        m1 = jnp.max(logits, axis=1, keepdims=True)
        i1 = jnp.min(jnp.where(logits == m1, lane, 128), axis=1, keepdims=True)
        rest = jnp.where(lane == i1, -jnp.inf, logits)
        m2 = jnp.max(rest, axis=1, keepdims=True)
        i2 = jnp.min(jnp.where(rest == m2, lane, 128), axis=1, keepdims=True)
        e2 = jnp.exp(m2 - m1)
        w1 = 1.0 / (1.0 + e2)
        w2 = e2 / (1.0 + e2)
        gate_ref[...] = jnp.where(lane == i1, w1, 0.0) + jnp.where(lane == i2, w2, 0.0)

    h = h_ref[...]
    lane = lax.broadcasted_iota(jnp.int32, gate_ref.shape, 1)
    gate = jnp.sum(jnp.where(lane == e, gate_ref[...], 0.0), axis=1, keepdims=True)
    act = _silu(_dot(h, wg_ref[0])) * _dot(h, wu_ref[0])
    o_ref[...] += gate * _bdot(act, wd_ref[0])


def _moe(x, g, router, bias, wg, wu, wd):
    t, d = x.shape
    n_e, _, f = wg.shape
    tm = min(ROW_TILE, t)
    r_pad = jnp.zeros((d, 128), F32).at[:, 0:n_e].set(router)
    b_pad = jnp.zeros((1, 128), F32).at[0, 0:n_e].set(bias)
    tok = pl.BlockSpec((tm, d), lambda i, j: (i, 0))
    return pl.pallas_call(
        _moe_kernel,
        grid=(t // tm, n_e),
        in_specs=[tok, _full((1, d)), _full((d, 128)), _full((1, 128)),
                  pl.BlockSpec((1, d, f), lambda i, j: (j, 0, 0)), pl.BlockSpec((1, d, f), lambda i, j: (j, 0, 0)),
                  pl.BlockSpec((1, f, d), lambda i, j: (j, 0, 0))],
        out_specs=tok,
        out_shape=jax.ShapeDtypeStruct((t, d), F32),
        scratch_shapes=[pltpu.VMEM((tm, d), BF16), pltpu.VMEM((tm, 128), F32)],
        compiler_params=_params("parallel", "arbitrary"),
        name="moe_swiglu",
    )(x, g.reshape(1, d), r_pad, b_pad, wg, wu, wd)


def kernel(x, mem, positions, norm_mix, w_in, rwkv_mu, rwkv_w0, rwkv_w2, rwkv_a0, rwkv_a2, rwkv_g2, rwkv_v0, rwkv_v1, rwkv_v2, rwkv_kk, rwkv_ka, rwkv_rk, rwkv_lnx_g, rwkv_lnx_b, dsa_q_norm, dsa_k_norm, idx_k_norm, w_branch_a, w_branch_b, w_out, norm_mem, mem_tok_norm, mem_wq, mem_wkv, mem_q_norm, mem_k_norm, mem_wo, norm_ffn, ffn_wg, ffn_wu, ffn_wd, moe_router, moe_bias, moe_wg, moe_wu, moe_wd):
    w = dict(rwkv_mu=rwkv_mu, rwkv_w0=rwkv_w0, rwkv_w2=rwkv_w2, rwkv_a0=rwkv_a0, rwkv_a2=rwkv_a2,
             rwkv_g2=rwkv_g2, rwkv_v0=rwkv_v0, rwkv_v1=rwkv_v1, rwkv_v2=rwkv_v2, rwkv_kk=rwkv_kk,
             rwkv_ka=rwkv_ka, rwkv_rk=rwkv_rk, rwkv_lnx_g=rwkv_lnx_g, rwkv_lnx_b=rwkv_lnx_b)
    b, s, d = x.shape
    t = b * s
    depth = w_in.shape[0]
    n_a = w_in.shape[2] - 1224 - 2 * d
    bf = lambda a: a.astype(BF16)
    cos, sin = _rope_tables(positions)
    x = x.reshape(t, d)
    v_first = None
    for l in range(depth):
        w_a = bf(w_in[l, :, 0:n_a])
        w_b = bf(_dsa_weight(w_in[l, :, n_a:n_a + 1224]))
        w_g = bf(w_in[l, :, n_a + 1224:])
        cols_a, cols_b, gates = _norm_matmul(x, norm_mix[l], (w_a, w_b, w_g))
        y_a, v_first = _rwkv_branch(cols_a.reshape(b, s, n_a), v_first, _rwkv_params(w, l))
        q, iq, kik, vw = _dsa_prep(cols_b, cos, sin, dsa_q_norm[l], dsa_k_norm[l], idx_k_norm[l])
        y_b = _dsa_attention(q, iq, kik, vw, b, s)
        x = _merge(x, gates, y_a.reshape(t, MIX_DIM), y_b.reshape(t, MIX_DIM),
                   bf(w_branch_a[l]), bf(w_branch_b[l]), bf(w_out[l]))
        mk, mv = _mem_kv(mem, mem_tok_norm[l], bf(mem_wkv[l]), mem_k_norm[l])
        x = _mem_attention(x.reshape(b, s, d), norm_mem[l], bf(mem_wq[l]), mem_q_norm[l], mk, mv,
                           bf(mem_wo[l])).reshape(t, d)
        j = l // 2
        if l % 2 == 0:
            x = _ffn(x, norm_ffn[l], bf(ffn_wg[j]), bf(ffn_wu[j]), bf(ffn_wd[j]))
        else:
            x = _moe(x, norm_ffn[l], moe_router[j], moe_bias[j], bf(moe_wg[j]), bf(moe_wu[j]), bf(moe_wd[j]))
    return x.reshape(b, s, d)
```

```python
import functools

import jax
import jax.numpy as jnp
import numpy as np
from jax import lax
from jax.experimental import pallas as pl
from jax.experimental.pallas import tpu as pltpu

F32 = jnp.float32
BF16 = jnp.bfloat16
HIGHEST = lax.Precision.HIGHEST

NORM_EPS = 1e-6
NEG_INF = -1e30
ROPE_THETA = 10000.0
LOG2_E = float(np.log2(np.e))

CHUNK = 64
Q_BLOCK = 128
DSA_TOPK_MAX = 256
DSA_KEY_CLASSES = 4
DSA_ROW_GROUPS = 4
HEAD_DIM = 64
N_HEADS = 8
MIX_DIM = N_HEADS * HEAD_DIM
RWKV_LNX_EPS = 1e-5 * HEAD_DIM
MEM_HEADS = 4
MEM_HEAD_DIM = 128
N_EXPERTS = 8

RWKV_CHUNK = 64
RWKV_BLOCK = 256
ROW_TILE = 512
VMEM_LIMIT = 48 * 1024 * 1024


def _params(*sem):
    return pltpu.CompilerParams(dimension_semantics=sem, vmem_limit_bytes=VMEM_LIMIT)


def _full(shape):
    nd = len(shape)
    return pl.BlockSpec(shape, lambda *_: (0,) * nd)


def _dot(a, b, precision=None):
    return jnp.dot(a, b, preferred_element_type=F32, precision=precision)


def _dot_nt(a, b, precision=None):
    return lax.dot_general(a, b, (((1,), (1,)), ((), ())), preferred_element_type=F32,
                           precision=precision)


def _dot_tn(a, b, precision=None):
    return lax.dot_general(a, b, (((0,), (0,)), ((), ())), preferred_element_type=F32,
                           precision=precision)


def _bdot(a, b):
    return _dot(a.astype(BF16), b.astype(BF16))


def _split(x):
    hi = x.astype(BF16)
    return hi, (x - hi.astype(F32)).astype(BF16)


def _dot_x2(x, m):
    hi, lo = _split(x)
    return _dot(hi, m) + _dot(lo, m)


def _dot_x3(x, w_hi, w_lo):
    hi, lo = _split(x)
    return _dot(hi, w_hi) + (_dot(lo, w_hi) + _dot(hi, w_lo))


def _rms(x, g):
    return x * lax.rsqrt(jnp.mean(x * x, axis=-1, keepdims=True) + NORM_EPS) * g


def _sigmoid(x):
    return 1.0 / (1.0 + jnp.exp(-x))


def _head_ones(width, head):
    r = lax.broadcasted_iota(jnp.int32, (width, width), 0) // head
    c = lax.broadcasted_iota(jnp.int32, (width, width), 1) // head
    return (r == c).astype(BF16)


def _norm_matmul_kernel(x_ref, g_ref, *refs):
    n = len(refs) // 2
    h = _rms(x_ref[...], g_ref[...]).astype(BF16)
    for w_ref, o_ref in zip(refs[:n], refs[n:]):
        o_ref[...] = _dot(h, w_ref[...])


def _norm_matmul(x, g, ws, tm=ROW_TILE // 2):
    t, d = x.shape
    tm = min(tm, t)
    return pl.pallas_call(
        _norm_matmul_kernel,
        grid=(t // tm,),
        in_specs=[pl.BlockSpec((tm, d), lambda i: (i, 0)), _full((1, d))] + [_full(w.shape) for w in ws],
        out_specs=tuple(pl.BlockSpec((tm, w.shape[1]), lambda i: (i, 0)) for w in ws),
        out_shape=tuple(jax.ShapeDtypeStruct((t, w.shape[1]), F32) for w in ws),
        compiler_params=_params("parallel"),
        name="norm_matmul",
    )(x, g.reshape(1, d), *ws)


def _rwkv_kernel(has_vmix, *refs):
    if has_vmix:
        (cols_ref, vf_ref, mu_ref, w0_ref, a0_ref, wah_ref, wal_ref, g2_ref, kk_ref, ka_ref,
         rk_ref, lg_ref, lb_ref, v0_ref, v1_ref, v2_ref,
         y_ref, st_ref, prev_ref, at_ref, rt_ref, bt_ref, kt_ref, bh_ref, kh_ref, vv_ref,
         gc_ref, bonus_ref, gate_ref, yy_ref, y0_ref, p_ref, rb_ref, z_ref, lhs_ref, dl_ref) = refs
        vout_ref = None
    else:
        (cols_ref, mu_ref, w0_ref, a0_ref, wah_ref, wal_ref, g2_ref, kk_ref, ka_ref,
         rk_ref, lg_ref, lb_ref,
         y_ref, vout_ref, st_ref, prev_ref, at_ref, rt_ref, bt_ref, kt_ref, bh_ref, kh_ref, vv_ref,
         gc_ref, bonus_ref, gate_ref, yy_ref, y0_ref, p_ref, rb_ref, z_ref, lhs_ref, dl_ref) = refs

    tb = cols_ref.shape[1]
    c = RWKV_CHUNK
    n_chunks = tb // c
    hd = HEAD_DIM

    @pl.when(pl.program_id(1) == 0)
    def _():
        st_ref[...] = jnp.zeros_like(st_ref)
        prev_ref[...] = jnp.zeros_like(prev_ref)

    cols = cols_ref[0]
    row = lax.broadcasted_iota(jnp.int32, cols.shape, 0)
    shifted = jnp.where(row == 0, prev_ref[...], pltpu.roll(cols, 1, 0))
    prev_ref[...] = cols[tb - 1:tb, :]
    mixed = cols + (shifted - cols) * mu_ref[...]

    r = mixed[:, 0:MIX_DIM]
    k = mixed[:, MIX_DIM:2 * MIX_DIM]
    v = mixed[:, 2 * MIX_DIM:3 * MIX_DIM]
    wa = mixed[:, 3 * MIX_DIM:3 * MIX_DIM + 128]
    gd = mixed[:, 3 * MIX_DIM + 128:3 * MIX_DIM + 256]
    lane = lax.broadcasted_iota(jnp.int32, wa.shape, 1)
    wa = jnp.where(lane < 64, jnp.tanh(wa), wa)
    wa_out = _dot_x3(wa, wah_ref[...], wal_ref[...])
    u_dec = w0_ref[...] + wa_out[:, 0:MIX_DIM]
    a = _sigmoid(a0_ref[...] + wa_out[:, MIX_DIM:2 * MIX_DIM])
    gate_ref[...] = _bdot(_sigmoid(gd), g2_ref[...])
    if has_vmix:
        vm = _bdot(_bdot(v, v1_ref[...]), v2_ref[...])
        v = v + (vf_ref[0] - v) * _sigmoid(v0_ref[...] + vm)
    else:
        vout_ref[0] = v

    hb = _head_ones(MIX_DIM, hd)
    kk = k * kk_ref[...]
    kk = kk / jnp.maximum(jnp.sqrt(_dot_x2(kk * kk, hb)), 1e-12)
    k = k * (1.0 + (a - 1.0) * ka_ref[...])
    bonus_ref[...] = _bdot(r * k * rk_ref[...], hb) * v
    ll = -_sigmoid(u_dec) * float(np.exp(-0.5))

    ri = lax.broadcasted_iota(jnp.int32, (tb, tb), 0)
    ci = lax.broadcasted_iota(jnp.int32, (tb, tb), 1)
    same = (ri // c) == (ci // c)
    strict_f = (same & (ri > ci)).astype(F32)
    incl_f = (same & (ri >= ci)).astype(F32)
    ll_hi, ll_lo = _split(ll)
    tri = incl_f.astype(BF16)
    blk = same.astype(BF16)
    cum = _dot(tri, ll_hi) + _dot(tri, ll_lo)
    tot = _dot(blk, ll_hi) + _dot(blk, ll_lo)
    er = lax.broadcasted_iota(jnp.int32, (tb, n_chunks * 128), 0) // c
    ec = lax.broadcasted_iota(jnp.int32, (tb, n_chunks * 128), 1) // 128
    sel = (er == ec).astype(BF16)
    gc_ref[...] = jnp.exp(_dot_tn(ll_hi, sel) + _dot_tn(ll_lo, sel))
    g_inv = jnp.exp(-cum)
    g_tail = jnp.exp(tot - cum)
    beta = kk * a
    at_ref[...] = (-kk * jnp.exp(cum - ll)).astype(BF16)
    rt_ref[...] = (r * jnp.exp(cum)).astype(BF16)
    bt_ref[...] = (beta * g_inv).astype(BF16)
    kt_ref[...] = (k * g_inv).astype(BF16)
    bh_ref[...] = (beta * g_tail).astype(BF16)
    kh_ref[...] = (k * g_tail).astype(BF16)
    vv_ref[...] = v.astype(BF16)

    n_levels = int(np.log2(c))
    heads = [slice(h * hd, (h + 1) * hd) for h in range(N_HEADS)]

    for h, cs in enumerate(heads):
        a_t = at_ref[:, cs]
        r_t = rt_ref[:, cs]
        b_t = bt_ref[:, cs]
        k_t = kt_ref[:, cs]
        v_c = vv_ref[:, cs]
        p_ref[h] = (_dot_nt(a_t, b_t) * strict_f).astype(BF16)
        m_mat = (_dot_nt(a_t, k_t) * strict_f).astype(BF16)
        rb_ref[h] = (_dot_nt(r_t, b_t) * incl_f).astype(BF16)
        rk = (_dot_nt(r_t, k_t) * incl_f).astype(BF16)
        y0_ref[:, cs] = _dot(rk, v_c)
        z_ref[h] = jnp.concatenate([a_t.astype(F32), _dot(m_mat, v_c)], axis=1)

    for lev in range(n_levels):
        for h in range(N_HEADS):
            p = p_ref[h]
            z = z_ref[h]
            z_ref[h] = z + _dot(p, z.astype(BF16))
            if lev < n_levels - 1:
                p_ref[h] = _dot(p, p).astype(BF16)

    for h, cs in enumerate(heads):
        z_b = z_ref[h].astype(BF16)
        w = _dot(rb_ref[h], z_b)
        r_bar = (rt_ref[:, cs].astype(F32) + w[:, 0:hd]).astype(BF16)
        y0_ref[:, cs] = y0_ref[:, cs] + w[:, hd:2 * hd]
        for ch in range(n_chunks):
            rows = slice(ch * c, (ch + 1) * c)
            pd = _dot_tn(bh_ref[rows, cs], z_b[rows])
            d2 = _dot_tn(kh_ref[rows, cs], vv_ref[rows, cs])
            lhs_ref[h, ch] = jnp.concatenate([r_bar[rows], pd[:, 0:hd].astype(BF16)], axis=0)
            dl_ref[h, ch] = pd[:, hd:2 * hd] + d2

    for ch in range(n_chunks):
        rows = slice(ch * c, (ch + 1) * c)
        for h, cs in enumerate(heads):
            st = st_ref[h]
            res = _dot(lhs_ref[h, ch], st.astype(BF16))
            yy_ref[rows, cs] = res[0:c] + y0_ref[rows, cs]
            g_col = gc_ref[h * hd:(h + 1) * hd, ch * 128:ch * 128 + hd]
            st_ref[h] = st * g_col + res[c:2 * c] + dl_ref[h, ch]

    y = yy_ref[...]
    mean = _bdot(y, hb) * (1.0 / hd)
    yc = y - mean
    var = _bdot(yc * yc, hb) * (1.0 / hd)
    yn = yc * lax.rsqrt(var + RWKV_LNX_EPS) * lg_ref[...] + lb_ref[...]
    y_ref[0] = ((yn + bonus_ref[...]) * gate_ref[...]).astype(y_ref.dtype)


def _rwkv_params(w, l):
    p = {k: w["rwkv_" + k][l] for k in ("mu", "w0", "w2", "a0", "a2", "g2", "kk", "ka", "rk", "lnx_g", "lnx_b")}
    if l > 0:
        p.update({k: w["rwkv_" + k][l - 1] for k in ("v0", "v1", "v2")})
    return p


def _rwkv_branch(cols, v_first, p):
    b, s, nc = cols.shape
    tb = min(RWKV_BLOCK, s)
    has_vmix = v_first is not None
    row = lambda a: a.reshape(1, -1)
    pair = lambda a: list(_split(a))
    tok = lambda w: pl.BlockSpec((1, tb, w), lambda i, j: (i, j, 0))
    wa2 = jnp.zeros((128, 2 * MIX_DIM), F32)
    wa2 = wa2.at[0:64, 0:MIX_DIM].set(p["w2"]).at[64:128, MIX_DIM:].set(p["a2"])
    ins = [cols] + ([v_first] if has_vmix else []) + [
        row(p["mu"]), row(p["w0"]), row(p["a0"])] + pair(wa2) + [p["g2"].astype(BF16)] + [
        row(p["kk"]), row(p["ka"]), row(p["rk"]), row(p["lnx_g"]), row(p["lnx_b"])]
    if has_vmix:
        ins += [row(p["v0"]), p["v1"].astype(BF16), p["v2"].astype(BF16)]
    in_specs = [tok(nc)] + ([tok(MIX_DIM)] if has_vmix else []) + [_full(a.shape) for a in ins[1 + has_vmix:]]
    y_shape = jax.ShapeDtypeStruct((b, s, MIX_DIM), F32)
    out_shape = y_shape if has_vmix else (y_shape, y_shape)
    out_specs = tok(MIX_DIM) if has_vmix else (tok(MIX_DIM), tok(MIX_DIM))
    tok_bf = pltpu.VMEM((tb, MIX_DIM), BF16)
    tok_f32 = pltpu.VMEM((tb, MIX_DIM), F32)
    n_ch = tb // RWKV_CHUNK
    scratch = ([pltpu.VMEM((N_HEADS, HEAD_DIM, HEAD_DIM), F32), pltpu.VMEM((1, nc), F32)] + [tok_bf] * 7
               + [pltpu.VMEM((MIX_DIM, n_ch * 128), F32)] + [tok_f32] * 4
               + [pltpu.VMEM((N_HEADS, tb, tb), BF16)] * 2 + [pltpu.VMEM((N_HEADS, tb, 2 * HEAD_DIM), F32),
                  pltpu.VMEM((N_HEADS, n_ch, 2 * RWKV_CHUNK, HEAD_DIM), BF16),
                  pltpu.VMEM((N_HEADS, n_ch, HEAD_DIM, HEAD_DIM), F32)])
    out = pl.pallas_call(
        functools.partial(_rwkv_kernel, has_vmix),
        grid=(b, s // tb),
        in_specs=in_specs,
        out_specs=out_specs,
        out_shape=out_shape,
        scratch_shapes=scratch,
        compiler_params=_params("parallel", "arbitrary"),
        name="rwkv7_vmix" if has_vmix else "rwkv7_first",
    )(*ins)
    return (out, v_first) if has_vmix else out


DSA_COLS_PAD = 1280
_DSA_Q, _DSA_IQ, _DSA_KIK, _DSA_VW = 0, 512, 1024, 1152


def _rope_table_kernel(pos_ref, freq_ref, cos_ref, sin_ref):
    ang = pos_ref[...] * freq_ref[...]
    lane = lax.broadcasted_iota(jnp.int32, ang.shape, 1)
    cos_ref[...] = jnp.cos(ang)
    sin_ref[...] = jnp.where(lane % HEAD_DIM < HEAD_DIM // 2, -jnp.sin(ang), jnp.sin(ang))


def _rope_tables(positions):
    t = positions.size
    tm = min(ROW_TILE, t)
    inv_freq = 1.0 / (ROPE_THETA ** (jnp.arange(0, HEAD_DIM, 2, dtype=F32) / HEAD_DIM))
    freq = jnp.tile(inv_freq, 4).reshape(1, 128)
    pos = positions.reshape(t, 1).astype(F32)
    spec = pl.BlockSpec((tm, 128), lambda i: (i, 0))
    return pl.pallas_call(
        _rope_table_kernel,
        grid=(t // tm,),
        in_specs=[pl.BlockSpec((tm, 1), lambda i: (i, 0)), _full((1, 128))],
        out_specs=(spec, spec),
        out_shape=(jax.ShapeDtypeStruct((t, 128), F32),) * 2,
        compiler_params=_params("parallel"),
        name="rope_tables",
    )(pos, freq)


def _rope128(x, cos, sin):
    lane = lax.broadcasted_iota(jnp.int32, x.shape, 1)
    half = HEAD_DIM // 2
    rot = jnp.where(lane % HEAD_DIM < half, pltpu.roll(x, 128 - half, 1), pltpu.roll(x, half, 1))
    return x * cos + rot * sin


def _dsa_prep_kernel(cols_ref, cos_ref, sin_ref, qn_ref, kn_ref, q_ref, iq_ref, kik_ref, vw_ref):
    cos, sin = cos_ref[...], sin_ref[...]
    q = cols_ref[:, _DSA_Q:_DSA_Q + MIX_DIM]
    ss = _dot_x2(q * q, _head_ones(MIX_DIM, HEAD_DIM)) * (1.0 / HEAD_DIM)
    q = q * lax.rsqrt(ss + NORM_EPS) * qn_ref[...]
    kik = cols_ref[:, _DSA_KIK:_DSA_KIK + 128]
    ss = _dot_x2(kik * kik, _head_ones(128, HEAD_DIM)) * (1.0 / HEAD_DIM)
    kik_ref[...] = _rope128(kik * lax.rsqrt(ss + NORM_EPS) * kn_ref[...], cos, sin)
    for j in range(MIX_DIM // 128):
        sl = slice(j * 128, (j + 1) * 128)
        q_ref[:, sl] = _rope128(q[:, sl], cos, sin) * (HEAD_DIM ** -0.5 * LOG2_E)
        iq_ref[:, sl] = _rope128(cols_ref[:, _DSA_IQ + j * 128:_DSA_IQ + (j + 1) * 128], cos, sin) * (HEAD_DIM ** -0.5)
    vw_ref[...] = cols_ref[:, _DSA_VW:_DSA_VW + 128]


def _dsa_prep(cols, cos, sin, q_norm, k_norm, idx_k_norm):
    t = cols.shape[0]
    tm = min(ROW_TILE, t)
    qn = jnp.tile(q_norm, N_HEADS).reshape(1, MIX_DIM)
    kn = jnp.concatenate([k_norm, idx_k_norm]).reshape(1, 128)
    tok = lambda w: pl.BlockSpec((tm, w), lambda i: (i, 0))
    sds = lambda w: jax.ShapeDtypeStruct((t, w), F32)
    return pl.pallas_call(
        _dsa_prep_kernel,
        grid=(t // tm,),
        in_specs=[tok(DSA_COLS_PAD), tok(128), tok(128), _full((1, MIX_DIM)), _full((1, 128))],
        out_specs=(tok(MIX_DIM), tok(MIX_DIM), tok(128), tok(128)),
        out_shape=(sds(MIX_DIM), sds(MIX_DIM), sds(128), sds(128)),
        compiler_params=_params("parallel"),
        name="dsa_prep",
    )(cols, cos, sin, qn, kn)


def _count(mask):
    return jnp.sum(jnp.where(mask, 1.0, 0.0), axis=1, keepdims=True)


def _dsa_block(top_k, n_keys, q_ref, iq_ref, vwq_ref, kik_ref, vw_ref, o_ref, key_ref):
    qb = q_ref.shape[1]
    start = pl.program_id(1) * qb
    k = kik_ref[0, 0:n_keys, 0:HEAD_DIM].astype(BF16)
    ik = kik_ref[0, 0:n_keys, HEAD_DIM:2 * HEAD_DIM].astype(BF16)
    v = vw_ref[0, 0:n_keys, 0:HEAD_DIM].astype(BF16)
    iw = vwq_ref[0, :, HEAD_DIM:HEAD_DIM + N_HEADS] * (N_HEADS ** -0.5)

    score = jnp.zeros((qb, n_keys), F32)
    for h in range(N_HEADS):
        lg = _dot_nt(iq_ref[0, :, h * HEAD_DIM:(h + 1) * HEAD_DIM].astype(BF16), ik)
        score = score + jnp.maximum(lg, 0.0) * iw[:, h:h + 1]
    key_pos = lax.broadcasted_iota(jnp.int32, (qb, n_keys), 1)
    q_pos = start + lax.broadcasted_iota(jnp.int32, (qb, 1), 0)
    limit = (q_pos // CHUNK + 1) * CHUNK
    adm = key_pos < limit
    score = jnp.where(adm, score, NEG_INF)
    score = jnp.where(score == 0.0, 0.0, score)

    bits = pltpu.bitcast(score, jnp.int32)
    key = bits ^ ((bits >> 31) & jnp.int32(0x7FFFFFFF))
    kf = float(top_k)
    key_ref[:, 0:n_keys] = key
    gr = qb // DSA_ROW_GROUPS

    def theta_body(i, ths):
        bit = jnp.int32(1) << (31 - i)
        out = []
        for g in range(DSA_ROW_GROUPS):
            cand = ths[g] + bit
            cnt = _count(key_ref[g * gr:(g + 1) * gr, 0:n_keys] >= cand)
            out.append(jnp.where(cnt >= kf, cand, ths[g]))
        return tuple(out)

    lowest = jnp.full((gr, 1), -2 ** 31, jnp.int32)
    ths = lax.fori_loop(0, 32, theta_body, (lowest,) * DSA_ROW_GROUPS, unroll=4)
    theta = jnp.concatenate(ths, axis=0)
    gt = key > theta
    eq = key == theta
    n_gt = _count(gt)
    need = kf - n_gt
    n_bits = int(np.ceil(np.log2(n_keys))) + 1

    def tie_search():
        def tie_body(i, jb):
            cand = jb + (jnp.int32(1) << (n_bits - 1 - i))
            ok = (cand <= n_keys) & (_count(eq & (key_pos < cand)) <= need)
            return jnp.where(ok, cand, jb)

        return lax.fori_loop(0, n_bits, tie_body, jnp.zeros((qb, 1), jnp.int32))

    has_tie = jnp.max(n_gt + _count(eq)) > kf
    bound = lax.cond(has_tie, tie_search, lambda: jnp.full((qb, 1), n_keys, jnp.int32))
    bias = jnp.where((gt | (eq & (key_pos < bound))) & adm, 0.0, NEG_INF)

    for h in range(N_HEADS):
        s = _dot_nt(q_ref[0, :, h * HEAD_DIM:(h + 1) * HEAD_DIM].astype(BF16), k) + bias
        p = jnp.exp2(s - jnp.max(s, axis=1, keepdims=True))
        o = _dot(p.astype(BF16), v) / jnp.sum(p, axis=1, keepdims=True)
        o_ref[0, :, h * HEAD_DIM:(h + 1) * HEAD_DIM] = o


def _dsa_attn_kernel(top_k, n_classes, q_ref, iq_ref, vwq_ref, kik_ref, vw_ref, o_ref, key_ref):
    qb = q_ref.shape[1]
    per_class = (kik_ref.shape[1] // qb) // n_classes
    cls = pl.program_id(1) // per_class
    for n in range(n_classes):
        pl.when(cls == n)(functools.partial(
            _dsa_block, top_k, (n + 1) * per_class * qb, q_ref, iq_ref, vwq_ref, kik_ref, vw_ref, o_ref, key_ref))


def _dsa_attention(q, iq, kik, vw, b, s):
    top_k = min(DSA_TOPK_MAX, s // 4)
    qb = min(Q_BLOCK, s)
    n_classes = min(DSA_KEY_CLASSES, s // qb)
    r3 = lambda a: a.reshape(b, s, a.shape[-1])
    blk = lambda w: pl.BlockSpec((1, qb, w), lambda i, j: (i, j, 0))
    seq = lambda w: pl.BlockSpec((1, s, w), lambda i, j: (i, 0, 0))
    return pl.pallas_call(
        functools.partial(_dsa_attn_kernel, top_k, n_classes),
        grid=(b, s // qb),
        in_specs=[blk(MIX_DIM), blk(MIX_DIM), blk(128), seq(128), seq(128)],
        out_specs=blk(MIX_DIM),
        out_shape=jax.ShapeDtypeStruct((b, s, MIX_DIM), F32),
        scratch_shapes=[pltpu.VMEM((qb, s), jnp.int32)],
        compiler_params=_params("parallel", "arbitrary"),
        name="dsa_attention",
    )(r3(q), r3(iq), r3(vw), r3(kik), r3(vw))


def _dsa_weight(w_dsa):
    d = w_dsa.shape[0]
    q, k, v, iq, ik, iw = jnp.split(w_dsa, np.cumsum([512, 64, 64, 512, 64, 8])[:-1].tolist(), axis=1)
    pad = jnp.zeros((d, DSA_COLS_PAD - 1224), w_dsa.dtype)
    return jnp.concatenate([q, iq, k, ik, v, iw, pad], axis=1)


def _merge_kernel(x_ref, g_ref, ya_ref, yb_ref, wa_ref, wb_ref, wo_ref, o_ref):
    d = x_ref.shape[1]
    ga = _sigmoid(g_ref[:, 0:d])
    gb = _sigmoid(g_ref[:, d:2 * d])
    merged = ga * _bdot(ya_ref[...], wa_ref[...]) + gb * _bdot(yb_ref[...], wb_ref[...])
    o_ref[...] = x_ref[...] + _bdot(merged, wo_ref[...])


def _merge(x, gates, ya, yb, wa, wb, wo):
    t, d = x.shape
    tm = min(ROW_TILE, t)
    tok = lambda w: pl.BlockSpec((tm, w), lambda i: (i, 0))
    return pl.pallas_call(
        _merge_kernel,
        grid=(t // tm,),
        in_specs=[tok(d), tok(2 * d), tok(MIX_DIM), tok(MIX_DIM), _full(wa.shape), _full(wb.shape), _full(wo.shape)],
        out_specs=tok(d),
        out_shape=jax.ShapeDtypeStruct((t, d), F32),
        compiler_params=_params("parallel"),
        name="merge_out",
    )(x, gates, ya, yb, wa, wb, wo)


def _mem_kv_kernel(mem_ref, g_ref, wkv_ref, kn_ref, k_ref, v_ref):
    dm = k_ref.shape[2]
    kv = _dot(_rms(mem_ref[0], g_ref[...]).astype(BF16), wkv_ref[...])
    k = kv[:, 0:dm]
    ss = _dot_x2(k * k, _head_ones(dm, MEM_HEAD_DIM)) * (1.0 / MEM_HEAD_DIM)
    k_ref[0] = k * lax.rsqrt(ss + NORM_EPS) * kn_ref[...]
    v_ref[0] = kv[:, dm:2 * dm]


def _mem_kv(mem, g, wkv, k_norm):
    b, m, d = mem.shape
    dm = MEM_HEADS * MEM_HEAD_DIM
    kn = jnp.tile(k_norm, MEM_HEADS).reshape(1, dm)
    spec = pl.BlockSpec((1, m, dm), lambda i: (i, 0, 0))
    return pl.pallas_call(
        _mem_kv_kernel,
        grid=(b,),
        in_specs=[pl.BlockSpec((1, m, d), lambda i: (i, 0, 0)), _full((1, d)), _full(wkv.shape), _full((1, dm))],
        out_specs=(spec, spec),
        out_shape=(jax.ShapeDtypeStruct((b, m, dm), F32),) * 2,
        compiler_params=_params("parallel"),
        name="mem_kv",
    )(mem, g.reshape(1, d), wkv, kn)


def _mem_attn_kernel(x_ref, g_ref, wq_ref, qn_ref, k_ref, v_ref, wo_ref, o_ref):
    x = x_ref[0]
    dm = wq_ref.shape[1]
    q = _dot(_rms(x, g_ref[...]).astype(BF16), wq_ref[...])
    ss = _dot_x2(q * q, _head_ones(dm, MEM_HEAD_DIM)) * (1.0 / MEM_HEAD_DIM)
    q = q * lax.rsqrt(ss + NORM_EPS) * qn_ref[...] * (MEM_HEAD_DIM ** -0.5)
    outs = []
    for h in range(MEM_HEADS):
        sl = slice(h * MEM_HEAD_DIM, (h + 1) * MEM_HEAD_DIM)
        s = _dot_nt(q[:, sl].astype(BF16), k_ref[0, :, sl].astype(BF16))
        p = jnp.exp(s - jnp.max(s, axis=1, keepdims=True))
        p = p / jnp.sum(p, axis=1, keepdims=True)
        outs.append(_dot(p.astype(BF16), v_ref[0, :, sl].astype(BF16)))
    o = jnp.concatenate(outs, axis=1)
    o_ref[0] = x + _bdot(o, wo_ref[...])


def _mem_attention(x, g, wq, q_norm, k, v, wo):
    b, s, d = x.shape
    m, dm = k.shape[1], k.shape[2]
    tm = min(ROW_TILE, s)
    qn = jnp.tile(q_norm, MEM_HEADS).reshape(1, dm)
    tok = pl.BlockSpec((1, tm, d), lambda i, j: (i, j, 0))
    kv = pl.BlockSpec((1, m, dm), lambda i, j: (i, 0, 0))
    return pl.pallas_call(
        _mem_attn_kernel,
        grid=(b, s // tm),
        in_specs=[tok, _full((1, d)), _full(wq.shape), _full((1, dm)), kv, kv, _full(wo.shape)],
        out_specs=tok,
        out_shape=jax.ShapeDtypeStruct((b, s, d), F32),
        compiler_params=_params("parallel", "parallel"),
        name="mem_attention",
    )(x, g.reshape(1, d), wq, qn, k, v, wo)


def _silu(x):
    return x * _sigmoid(x)


def _ffn_kernel(x_ref, g_ref, wg_ref, wu_ref, wd_ref, o_ref, h_ref):
    j = pl.program_id(1)

    @pl.when(j == 0)
    def _():
        x = x_ref[...]
        h_ref[...] = _rms(x, g_ref[...]).astype(BF16)
        o_ref[...] = x

    h = h_ref[...]
    act = _silu(_dot(h, wg_ref[...])) * _dot(h, wu_ref[...])
    o_ref[...] += _bdot(act, wd_ref[...])


def _ffn(x, g, wg, wu, wd, n_f=2):
    t, d = x.shape
    f = wg.shape[1]
    tm = min(ROW_TILE, t)
    tf = f // n_f
    tok = pl.BlockSpec((tm, d), lambda i, j: (i, 0))
    return pl.pallas_call(
        _ffn_kernel,
        grid=(t // tm, n_f),
        in_specs=[tok, _full((1, d)), pl.BlockSpec((d, tf), lambda i, j: (0, j)),
                  pl.BlockSpec((d, tf), lambda i, j: (0, j)), pl.BlockSpec((tf, d), lambda i, j: (j, 0))],
        out_specs=tok,
        out_shape=jax.ShapeDtypeStruct((t, d), F32),
        scratch_shapes=[pltpu.VMEM((tm, d), BF16)],
        compiler_params=_params("parallel", "arbitrary"),
        name="ffn_swiglu",
    )(x, g.reshape(1, d), wg, wu, wd)


def _moe_kernel(x_ref, g_ref, r_ref, b_ref, wg_ref, wu_ref, wd_ref, o_ref, h_ref, gate_ref):
    e = pl.program_id(1)

    @pl.when(e == 0)
    def _():
        x = x_ref[...]
        hn = _rms(x, g_ref[...])
        h_ref[...] = hn.astype(BF16)
        o_ref[...] = x
        logits = _dot(hn, r_ref[...], HIGHEST) + b_ref[...]
        lane = lax.broadcasted_iota(jnp.int32, logits.shape, 1)
        logits = jnp.where(lane < N_EXPERTS, logits, -jnp.inf)
        m1 = jnp.max(logits, axis=1, keepdims=True)
        i1 = jnp.min(jnp.where(logits == m1, lane, 128), axis=1, keepdims=True)
        rest = jnp.where(lane == i1, -jnp.inf, logits)
        m2 = jnp.max(rest, axis=1, keepdims=True)
        i2 = jnp.min(jnp.where(rest == m2, lane, 128), axis=1, keepdims=True)
        e2 = jnp.exp(m2 - m1)
        w1 = 1.0 / (1.0 + e2)
        w2 = e2 / (1.0 + e2)
        gate_ref[...] = jnp.where(lane == i1, w1, 0.0) + jnp.where(lane == i2, w2, 0.0)

    h = h_ref[...]
    lane = lax.broadcasted_iota(jnp.int32, gate_ref.shape, 1)
    gate = jnp.sum(jnp.where(lane == e, gate_ref[...], 0.0), axis=1, keepdims=True)
    act = _silu(_dot(h, wg_ref[0])) * _dot(h, wu_ref[0])
    o_ref[...] += gate * _bdot(act, wd_ref[0])


def _moe(x, g, router, bias, wg, wu, wd):
    t, d = x.shape
    n_e, _, f = wg.shape
    tm = min(ROW_TILE, t)
    r_pad = jnp.zeros((d, 128), F32).at[:, 0:n_e].set(router)
    b_pad = jnp.zeros((1, 128), F32).at[0, 0:n_e].set(bias)
    tok = pl.BlockSpec((tm, d), lambda i, j: (i, 0))
    return pl.pallas_call(
        _moe_kernel,
        grid=(t // tm, n_e),
        in_specs=[tok, _full((1, d)), _full((d, 128)), _full((1, 128)),
                  pl.BlockSpec((1, d, f), lambda i, j: (j, 0, 0)), pl.BlockSpec((1, d, f), lambda i, j: (j, 0, 0)),
                  pl.BlockSpec((1, f, d), lambda i, j: (j, 0, 0))],
        out_specs=tok,
        out_shape=jax.ShapeDtypeStruct((t, d), F32),
        scratch_shapes=[pltpu.VMEM((tm, d), BF16), pltpu.VMEM((tm, 128), F32)],
        compiler_params=_params("parallel", "arbitrary"),
        name="moe_swiglu",
    )(x, g.reshape(1, d), r_pad, b_pad, wg, wu, wd)


def kernel(x, mem, positions, norm_mix, w_in, rwkv_mu, rwkv_w0, rwkv_w2, rwkv_a0, rwkv_a2, rwkv_g2, rwkv_v0, rwkv_v1, rwkv_v2, rwkv_kk, rwkv_ka, rwkv_rk, rwkv_lnx_g, rwkv_lnx_b, dsa_q_norm, dsa_k_norm, idx_k_norm, w_branch_a, w_branch_b, w_out, norm_mem, mem_tok_norm, mem_wq, mem_wkv, mem_q_norm, mem_k_norm, mem_wo, norm_ffn, ffn_wg, ffn_wu, ffn_wd, moe_router, moe_bias, moe_wg, moe_wu, moe_wd):
    w = dict(rwkv_mu=rwkv_mu, rwkv_w0=rwkv_w0, rwkv_w2=rwkv_w2, rwkv_a0=rwkv_a0, rwkv_a2=rwkv_a2,
             rwkv_g2=rwkv_g2, rwkv_v0=rwkv_v0, rwkv_v1=rwkv_v1, rwkv_v2=rwkv_v2, rwkv_kk=rwkv_kk,
             rwkv_ka=rwkv_ka, rwkv_rk=rwkv_rk, rwkv_lnx_g=rwkv_lnx_g, rwkv_lnx_b=rwkv_lnx_b)
    b, s, d = x.shape
    t = b * s
    depth = w_in.shape[0]
    n_a = w_in.shape[2] - 1224 - 2 * d
    bf = lambda a: a.astype(BF16)
    cos, sin = _rope_tables(positions)
    x = x.reshape(t, d)
    v_first = None
    for l in range(depth):
        w_a = bf(w_in[l, :, 0:n_a])
        w_b = bf(_dsa_weight(w_in[l, :, n_a:n_a + 1224]))
        w_g = bf(w_in[l, :, n_a + 1224:])
        cols_a, cols_b, gates = _norm_matmul(x, norm_mix[l], (w_a, w_b, w_g))
        y_a, v_first = _rwkv_branch(cols_a.reshape(b, s, n_a), v_first, _rwkv_params(w, l))
        q, iq, kik, vw = _dsa_prep(cols_b, cos, sin, dsa_q_norm[l], dsa_k_norm[l], idx_k_norm[l])
        y_b = _dsa_attention(q, iq, kik, vw, b, s)
        x = _merge(x, gates, y_a.reshape(t, MIX_DIM), y_b.reshape(t, MIX_DIM),
                   bf(w_branch_a[l]), bf(w_branch_b[l]), bf(w_out[l]))
        mk, mv = _mem_kv(mem, mem_tok_norm[l], bf(mem_wkv[l]), mem_k_norm[l])
        x = _mem_attention(x.reshape(b, s, d), norm_mem[l], bf(mem_wq[l]), mem_q_norm[l], mk, mv,
                           bf(mem_wo[l])).reshape(t, d)
        j = l // 2
        if l % 2 == 0:
            x = _ffn(x, norm_ffn[l], bf(ffn_wg[j]), bf(ffn_wu[j]), bf(ffn_wd[j]))
        else:
            x = _moe(x, norm_ffn[l], moe_router[j], moe_bias[j], bf(moe_wg[j]), bf(moe_wu[j]), bf(moe_wd[j]))
    return x.reshape(b, s, d)
```

```python
import functools

import jax
import jax.numpy as jnp
import numpy as np
from jax import lax
from jax.experimental import pallas as pl
from jax.experimental.pallas import tpu as pltpu

F32 = jnp.float32
BF16 = jnp.bfloat16
HIGHEST = lax.Precision.HIGHEST

NORM_EPS = 1e-6
NEG_INF = -1e30
ROPE_THETA = 10000.0
LOG2_E = float(np.log2(np.e))

CHUNK = 64
Q_BLOCK = 128
DSA_TOPK_MAX = 256
DSA_KEY_CLASSES = 8
DSA_ROW_GROUPS = 4
DSA_RADIX4_MAX_KEYS = 768
HEAD_DIM = 64
N_HEADS = 8
MIX_DIM = N_HEADS * HEAD_DIM
RWKV_LNX_EPS = 1e-5 * HEAD_DIM
MEM_HEADS = 4
MEM_HEAD_DIM = 128
N_EXPERTS = 8

RWKV_CHUNK = 64
RWKV_BLOCK = 256
ROW_TILE = 512
FFN_ROW_TILE = 1024
VMEM_LIMIT = 56 * 1024 * 1024


def _params(*sem):
    return pltpu.CompilerParams(dimension_semantics=sem, vmem_limit_bytes=VMEM_LIMIT)


def _full(shape):
    nd = len(shape)
    return pl.BlockSpec(shape, lambda *_: (0,) * nd)


def _dot(a, b, precision=None):
    return jnp.dot(a, b, preferred_element_type=F32, precision=precision)


def _dot_nt(a, b, precision=None):
    return lax.dot_general(a, b, (((1,), (1,)), ((), ())), preferred_element_type=F32,
                           precision=precision)


def _dot_tn(a, b, precision=None):
    return lax.dot_general(a, b, (((0,), (0,)), ((), ())), preferred_element_type=F32,
                           precision=precision)


def _bdot(a, b):
    return _dot(a.astype(BF16), b.astype(BF16))


def _split(x):
    hi = x.astype(BF16)
    return hi, (x - hi.astype(F32)).astype(BF16)


def _dot_x2(x, m):
    hi, lo = _split(x)
    return _dot(hi, m) + _dot(lo, m)


def _dot_x3(x, w_hi, w_lo):
    hi, lo = _split(x)
    return _dot(hi, w_hi) + (_dot(lo, w_hi) + _dot(hi, w_lo))


def _rms(x, g):
    return x * lax.rsqrt(jnp.mean(x * x, axis=-1, keepdims=True) + NORM_EPS) * g


def _sigmoid(x):
    return 1.0 / (1.0 + jnp.exp(-x))


def _head_ones(width, head):
    r = lax.broadcasted_iota(jnp.int32, (width, width), 0) // head
    c = lax.broadcasted_iota(jnp.int32, (width, width), 1) // head
    return (r == c).astype(BF16)


def _norm_matmul_kernel(x_ref, g_ref, *refs):
    n = len(refs) // 2
    h = _rms(x_ref[...], g_ref[...]).astype(BF16)
    for w_ref, o_ref in zip(refs[:n], refs[n:]):
        o_ref[...] = _dot(h, w_ref[...])


def _norm_matmul(x, g, ws, tm=ROW_TILE // 2):
    t, d = x.shape
    tm = min(tm, t)
    return pl.pallas_call(
        _norm_matmul_kernel,
        grid=(t // tm,),
        in_specs=[pl.BlockSpec((tm, d), lambda i: (i, 0)), _full((1, d))] + [_full(w.shape) for w in ws],
        out_specs=tuple(pl.BlockSpec((tm, w.shape[1]), lambda i: (i, 0)) for w in ws),
        out_shape=tuple(jax.ShapeDtypeStruct((t, w.shape[1]), F32) for w in ws),
        compiler_params=_params("parallel"),
        name="norm_matmul",
    )(x, g.reshape(1, d), *ws)


def _rwkv_kernel(has_vmix, *refs):
    if has_vmix:
        (cols_ref, vf_ref, mu_ref, w0_ref, a0_ref, wah_ref, wal_ref, g2_ref, kk_ref, ka_ref,
         rk_ref, lg_ref, lb_ref, v0_ref, v1_ref, v2_ref,
         y_ref, st_ref, prev_ref, at_ref, rt_ref, bt_ref, kt_ref, bh_ref, kh_ref, vv_ref,
         gc_ref, bonus_ref, gate_ref, yy_ref, y0_ref, p_ref, rb_ref, z_ref, lhs_ref, dl_ref) = refs
        vout_ref = None
    else:
        (cols_ref, mu_ref, w0_ref, a0_ref, wah_ref, wal_ref, g2_ref, kk_ref, ka_ref,
         rk_ref, lg_ref, lb_ref,
         y_ref, vout_ref, st_ref, prev_ref, at_ref, rt_ref, bt_ref, kt_ref, bh_ref, kh_ref, vv_ref,
         gc_ref, bonus_ref, gate_ref, yy_ref, y0_ref, p_ref, rb_ref, z_ref, lhs_ref, dl_ref) = refs

    tb = cols_ref.shape[1]
    c = RWKV_CHUNK
    n_chunks = tb // c
    hd = HEAD_DIM

    @pl.when(pl.program_id(1) == 0)
    def _():
        st_ref[...] = jnp.zeros_like(st_ref)
        prev_ref[...] = jnp.zeros_like(prev_ref)

    cols = cols_ref[0]
    row = lax.broadcasted_iota(jnp.int32, cols.shape, 0)
    shifted = jnp.where(row == 0, prev_ref[...], pltpu.roll(cols, 1, 0))
    prev_ref[...] = cols[tb - 1:tb, :]
    mixed = cols + (shifted - cols) * mu_ref[...]

    r = mixed[:, 0:MIX_DIM]
    k = mixed[:, MIX_DIM:2 * MIX_DIM]
    v = mixed[:, 2 * MIX_DIM:3 * MIX_DIM]
    wa = mixed[:, 3 * MIX_DIM:3 * MIX_DIM + 128]
    gd = mixed[:, 3 * MIX_DIM + 128:3 * MIX_DIM + 256]
    lane = lax.broadcasted_iota(jnp.int32, wa.shape, 1)
    wa = jnp.where(lane < 64, jnp.tanh(wa), wa)
    wa_out = _dot_x3(wa, wah_ref[...], wal_ref[...])
    u_dec = w0_ref[...] + wa_out[:, 0:MIX_DIM]
    a = _sigmoid(a0_ref[...] + wa_out[:, MIX_DIM:2 * MIX_DIM])
    gate_ref[...] = _bdot(_sigmoid(gd), g2_ref[...])
    if has_vmix:
        vm = _bdot(_bdot(v, v1_ref[...]), v2_ref[...])
        v = v + (vf_ref[0] - v) * _sigmoid(v0_ref[...] + vm)
    else:
        vout_ref[0] = v

    hb = _head_ones(MIX_DIM, hd)
    kk = k * kk_ref[...]
    kk = kk / jnp.maximum(jnp.sqrt(_dot_x2(kk * kk, hb)), 1e-12)
    k = k * (1.0 + (a - 1.0) * ka_ref[...])
    bonus_ref[...] = _bdot(r * k * rk_ref[...], hb) * v
    ll = -_sigmoid(u_dec) * float(np.exp(-0.5))

    ri = lax.broadcasted_iota(jnp.int32, (tb, tb), 0)
    ci = lax.broadcasted_iota(jnp.int32, (tb, tb), 1)
    same = (ri // c) == (ci // c)
    strict_f = (same & (ri > ci)).astype(F32)
    incl_f = (same & (ri >= ci)).astype(F32)
    ll_hi, ll_lo = _split(ll)
    tri = incl_f.astype(BF16)
    blk = same.astype(BF16)
    cum = _dot(tri, ll_hi) + _dot(tri, ll_lo)
    tot = _dot(blk, ll_hi) + _dot(blk, ll_lo)
    er = lax.broadcasted_iota(jnp.int32, (tb, n_chunks * 128), 0) // c
    ec = lax.broadcasted_iota(jnp.int32, (tb, n_chunks * 128), 1) // 128
    sel = (er == ec).astype(BF16)
    gc_ref[...] = jnp.exp(_dot_tn(ll_hi, sel) + _dot_tn(ll_lo, sel))
    g_inv = jnp.exp(-cum)
    g_tail = jnp.exp(tot - cum)
    beta = kk * a
    at_ref[...] = (-kk * jnp.exp(cum - ll)).astype(BF16)
    rt_ref[...] = (r * jnp.exp(cum)).astype(BF16)
    bt_ref[...] = (beta * g_inv).astype(BF16)
    kt_ref[...] = (k * g_inv).astype(BF16)
    bh_ref[...] = (beta * g_tail).astype(BF16)
    kh_ref[...] = (k * g_tail).astype(BF16)
    vv_ref[...] = v.astype(BF16)

    n_levels = int(np.log2(c))
    heads = [slice(h * hd, (h + 1) * hd) for h in range(N_HEADS)]

    for h, cs in enumerate(heads):
        a_t = at_ref[:, cs]
        r_t = rt_ref[:, cs]
        b_t = bt_ref[:, cs]
        k_t = kt_ref[:, cs]
        v_c = vv_ref[:, cs]
        p_ref[h] = (_dot_nt(a_t, b_t) * strict_f).astype(BF16)
        m_mat = (_dot_nt(a_t, k_t) * strict_f).astype(BF16)
        rb_ref[h] = (_dot_nt(r_t, b_t) * incl_f).astype(BF16)
        rk = (_dot_nt(r_t, k_t) * incl_f).astype(BF16)
        y0_ref[:, cs] = _dot(rk, v_c)
        z_ref[h] = jnp.concatenate([a_t.astype(F32), _dot(m_mat, v_c)], axis=1)

    for lev in range(n_levels):
        for h in range(N_HEADS):
            p = p_ref[h]
            z = z_ref[h]
            z_ref[h] = z + _dot(p, z.astype(BF16))
            if lev < n_levels - 1:
                p_ref[h] = _dot(p, p).astype(BF16)

    for h, cs in enumerate(heads):
        z_b = z_ref[h].astype(BF16)
        w = _dot(rb_ref[h], z_b)
        r_bar = (rt_ref[:, cs].astype(F32) + w[:, 0:hd]).astype(BF16)
        y0_ref[:, cs] = y0_ref[:, cs] + w[:, hd:2 * hd]
        for ch in range(n_chunks):
            rows = slice(ch * c, (ch + 1) * c)
            pd = _dot_tn(bh_ref[rows, cs], z_b[rows])
            d2 = _dot_tn(kh_ref[rows, cs], vv_ref[rows, cs])
            lhs_ref[h, ch] = jnp.concatenate([r_bar[rows], pd[:, 0:hd].astype(BF16)], axis=0)
            dl_ref[h, ch] = pd[:, hd:2 * hd] + d2

    for ch in range(n_chunks):
        rows = slice(ch * c, (ch + 1) * c)
        for h, cs in enumerate(heads):
            st = st_ref[h]
            res = _dot(lhs_ref[h, ch], st.astype(BF16))
            yy_ref[rows, cs] = res[0:c] + y0_ref[rows, cs]
            g_col = gc_ref[h * hd:(h + 1) * hd, ch * 128:ch * 128 + hd]
            st_ref[h] = st * g_col + res[c:2 * c] + dl_ref[h, ch]

    y = yy_ref[...]
    mean = _bdot(y, hb) * (1.0 / hd)
    yc = y - mean
    var = _bdot(yc * yc, hb) * (1.0 / hd)
    yn = yc * lax.rsqrt(var + RWKV_LNX_EPS) * lg_ref[...] + lb_ref[...]
    y_ref[0] = ((yn + bonus_ref[...]) * gate_ref[...]).astype(y_ref.dtype)


def _rwkv_params(w, l):
    p = {k: w["rwkv_" + k][l] for k in ("mu", "w0", "w2", "a0", "a2", "g2", "kk", "ka", "rk", "lnx_g", "lnx_b")}
    if l > 0:
        p.update({k: w["rwkv_" + k][l - 1] for k in ("v0", "v1", "v2")})
    return p


def _rwkv_branch(cols, v_first, p):
    b, s, nc = cols.shape
    tb = min(RWKV_BLOCK, s)
    has_vmix = v_first is not None
    row = lambda a: a.reshape(1, -1)
    pair = lambda a: list(_split(a))
    tok = lambda w: pl.BlockSpec((1, tb, w), lambda i, j: (i, j, 0))
    wa2 = jnp.zeros((128, 2 * MIX_DIM), F32)
    wa2 = wa2.at[0:64, 0:MIX_DIM].set(p["w2"]).at[64:128, MIX_DIM:].set(p["a2"])
    ins = [cols] + ([v_first] if has_vmix else []) + [
        row(p["mu"]), row(p["w0"]), row(p["a0"])] + pair(wa2) + [p["g2"].astype(BF16)] + [
        row(p["kk"]), row(p["ka"]), row(p["rk"]), row(p["lnx_g"]), row(p["lnx_b"])]
    if has_vmix:
        ins += [row(p["v0"]), p["v1"].astype(BF16), p["v2"].astype(BF16)]
    in_specs = [tok(nc)] + ([tok(MIX_DIM)] if has_vmix else []) + [_full(a.shape) for a in ins[1 + has_vmix:]]
    y_shape = jax.ShapeDtypeStruct((b, s, MIX_DIM), F32)
    out_shape = y_shape if has_vmix else (y_shape, y_shape)
    out_specs = tok(MIX_DIM) if has_vmix else (tok(MIX_DIM), tok(MIX_DIM))
    tok_bf = pltpu.VMEM((tb, MIX_DIM), BF16)
    tok_f32 = pltpu.VMEM((tb, MIX_DIM), F32)
    n_ch = tb // RWKV_CHUNK
    scratch = ([pltpu.VMEM((N_HEADS, HEAD_DIM, HEAD_DIM), F32), pltpu.VMEM((1, nc), F32)] + [tok_bf] * 7
               + [pltpu.VMEM((MIX_DIM, n_ch * 128), F32)] + [tok_f32] * 4
               + [pltpu.VMEM((N_HEADS, tb, tb), BF16)] * 2 + [pltpu.VMEM((N_HEADS, tb, 2 * HEAD_DIM), F32),
                  pltpu.VMEM((N_HEADS, n_ch, 2 * RWKV_CHUNK, HEAD_DIM), BF16),
                  pltpu.VMEM((N_HEADS, n_ch, HEAD_DIM, HEAD_DIM), F32)])
    out = pl.pallas_call(
        functools.partial(_rwkv_kernel, has_vmix),
        grid=(b, s // tb),
        in_specs=in_specs,
        out_specs=out_specs,
        out_shape=out_shape,
        scratch_shapes=scratch,
        compiler_params=_params("parallel", "arbitrary"),
        name="rwkv7_vmix" if has_vmix else "rwkv7_first",
    )(*ins)
    return (out, v_first) if has_vmix else out


DSA_COLS_PAD = 1280
_DSA_Q, _DSA_IQ, _DSA_KIK, _DSA_VW = 0, 512, 1024, 1152


def _rope_table_kernel(pos_ref, freq_ref, cos_ref, sin_ref):
    ang = pos_ref[...] * freq_ref[...]
    lane = lax.broadcasted_iota(jnp.int32, ang.shape, 1)
    cos_ref[...] = jnp.cos(ang)
    sin_ref[...] = jnp.where(lane % HEAD_DIM < HEAD_DIM // 2, -jnp.sin(ang), jnp.sin(ang))


def _rope_tables(positions):
    t = positions.size
    tm = min(ROW_TILE, t)
    inv_freq = 1.0 / (ROPE_THETA ** (jnp.arange(0, HEAD_DIM, 2, dtype=F32) / HEAD_DIM))
    freq = jnp.tile(inv_freq, 4).reshape(1, 128)
    pos = positions.reshape(t, 1).astype(F32)
    spec = pl.BlockSpec((tm, 128), lambda i: (i, 0))
    return pl.pallas_call(
        _rope_table_kernel,
        grid=(t // tm,),
        in_specs=[pl.BlockSpec((tm, 1), lambda i: (i, 0)), _full((1, 128))],
        out_specs=(spec, spec),
        out_shape=(jax.ShapeDtypeStruct((t, 128), F32),) * 2,
        compiler_params=_params("parallel"),
        name="rope_tables",
    )(pos, freq)


def _rope128(x, cos, sin):
    lane = lax.broadcasted_iota(jnp.int32, x.shape, 1)
    half = HEAD_DIM // 2
    rot = jnp.where(lane % HEAD_DIM < half, pltpu.roll(x, 128 - half, 1), pltpu.roll(x, half, 1))
    return x * cos + rot * sin


def _dsa_prep_kernel(cols_ref, cos_ref, sin_ref, qn_ref, kn_ref, q_ref, iq_ref, kik_ref, vw_ref):
    cos, sin = cos_ref[...], sin_ref[...]
    q = cols_ref[:, _DSA_Q:_DSA_Q + MIX_DIM]
    ss = _dot_x2(q * q, _head_ones(MIX_DIM, HEAD_DIM)) * (1.0 / HEAD_DIM)
    q = q * lax.rsqrt(ss + NORM_EPS) * qn_ref[...]
    kik = cols_ref[:, _DSA_KIK:_DSA_KIK + 128]
    ss = _dot_x2(kik * kik, _head_ones(128, HEAD_DIM)) * (1.0 / HEAD_DIM)
    kik_ref[...] = _rope128(kik * lax.rsqrt(ss + NORM_EPS) * kn_ref[...], cos, sin)
    for j in range(MIX_DIM // 128):
        sl = slice(j * 128, (j + 1) * 128)
        q_ref[:, sl] = _rope128(q[:, sl], cos, sin) * (HEAD_DIM ** -0.5 * LOG2_E)
        iq_ref[:, sl] = _rope128(cols_ref[:, _DSA_IQ + j * 128:_DSA_IQ + (j + 1) * 128], cos, sin) * (HEAD_DIM ** -0.5)
    vw_ref[...] = cols_ref[:, _DSA_VW:_DSA_VW + 128]


def _dsa_prep(cols, cos, sin, q_norm, k_norm, idx_k_norm):
    t = cols.shape[0]
    tm = min(ROW_TILE, t)
    qn = jnp.tile(q_norm, N_HEADS).reshape(1, MIX_DIM)
    kn = jnp.concatenate([k_norm, idx_k_norm]).reshape(1, 128)
    tok = lambda w: pl.BlockSpec((tm, w), lambda i: (i, 0))
    sds = lambda w: jax.ShapeDtypeStruct((t, w), F32)
    return pl.pallas_call(
        _dsa_prep_kernel,
        grid=(t // tm,),
        in_specs=[tok(DSA_COLS_PAD), tok(128), tok(128), _full((1, MIX_DIM)), _full((1, 128))],
        out_specs=(tok(MIX_DIM), tok(MIX_DIM), tok(128), tok(128)),
        out_shape=(sds(MIX_DIM), sds(MIX_DIM), sds(128), sds(128)),
        compiler_params=_params("parallel"),
        name="dsa_prep",
    )(cols, cos, sin, qn, kn)


def _count(mask):
    return jnp.sum(jnp.where(mask, 1.0, 0.0), axis=1, keepdims=True)


def _dsa_block(top_k, n_keys, q_ref, iq_ref, vwq_ref, kik_ref, vw_ref, o_ref, key_ref, s_ref):
    qb = q_ref.shape[1]
    start = pl.program_id(1) * qb
    k = kik_ref[0, 0:n_keys, 0:HEAD_DIM].astype(BF16)
    ik = kik_ref[0, 0:n_keys, HEAD_DIM:2 * HEAD_DIM].astype(BF16)
    v = vw_ref[0, 0:n_keys, 0:HEAD_DIM].astype(BF16)
    iw = vwq_ref[0, :, HEAD_DIM:HEAD_DIM + N_HEADS] * (N_HEADS ** -0.5)

    score = jnp.zeros((qb, n_keys), F32)
    for h in range(N_HEADS):
        lg = _dot_nt(iq_ref[0, :, h * HEAD_DIM:(h + 1) * HEAD_DIM].astype(BF16), ik)
        score = score + jnp.maximum(lg, 0.0) * iw[:, h:h + 1]
    key_pos = lax.broadcasted_iota(jnp.int32, (qb, n_keys), 1)
    q_pos = start + lax.broadcasted_iota(jnp.int32, (qb, 1), 0)
    limit = (q_pos // CHUNK + 1) * CHUNK
    adm = key_pos < limit
    score = jnp.where(adm, score, NEG_INF)
    score = jnp.where(score == 0.0, 0.0, score)

    bits = pltpu.bitcast(score, jnp.int32)
    key = bits ^ ((bits >> 31) & jnp.int32(0x7FFFFFFF))
    kf = float(top_k)
    key_ref[:, 0:n_keys] = key
    gr = qb // DSA_ROW_GROUPS
    digit_bits = 2 if n_keys <= DSA_RADIX4_MAX_KEYS else 1
    n_steps = 32 // digit_bits

    def theta_body(i, ths):
        unit = jnp.int32(1) << (32 - digit_bits - digit_bits * i)
        out = []
        for g in range(DSA_ROW_GROUPS):
            keys = key_ref[g * gr:(g + 1) * gr, 0:n_keys]
            passed = jnp.zeros((gr, 1), jnp.int32)
            for d in range(1, 2 ** digit_bits):
                cnt = _count(keys >= ths[g] + unit * d)
                passed = passed + jnp.where(cnt >= kf, 1, 0)
            out.append(ths[g] + unit * passed)
        return tuple(out)

    lowest = jnp.full((gr, 1), -2 ** 31, jnp.int32)
    ths = lax.fori_loop(0, n_steps, theta_body, (lowest,) * DSA_ROW_GROUPS, unroll=4 // digit_bits)
    theta = jnp.concatenate(ths, axis=0)
    gt = key > theta
    eq = key == theta
    n_gt = _count(gt)
    need = kf - n_gt
    n_bits = int(np.ceil(np.log2(n_keys))) + 1

    def tie_search():
        def tie_body(i, jb):
            cand = jb + (jnp.int32(1) << (n_bits - 1 - i))
            ok = (cand <= n_keys) & (_count(eq & (key_pos < cand)) <= need)
            return jnp.where(ok, cand, jb)

        return lax.fori_loop(0, n_bits, tie_body, jnp.zeros((qb, 1), jnp.int32))

    has_tie = jnp.max(n_gt + _count(eq)) > kf
    bound = lax.cond(has_tie, tie_search, lambda: jnp.full((qb, 1), n_keys, jnp.int32))
    bias = jnp.where((gt | (eq & (key_pos < bound))) & adm, 0.0, NEG_INF)

    tops = []
    for h in range(N_HEADS):
        s = _dot_nt(q_ref[0, :, h * HEAD_DIM:(h + 1) * HEAD_DIM].astype(BF16), k) + bias
        s_ref[h, :, 0:n_keys] = s
        tops.append(jnp.max(s, axis=1, keepdims=True))
    for h in range(N_HEADS):
        p = jnp.exp2(s_ref[h, :, 0:n_keys] - tops[h])
        o = _dot(p.astype(BF16), v) / jnp.sum(p, axis=1, keepdims=True)
        o_ref[0, :, h * HEAD_DIM:(h + 1) * HEAD_DIM] = o


def _dsa_attn_kernel(top_k, n_classes, q_ref, iq_ref, vwq_ref, kik_ref, vw_ref, o_ref, key_ref, s_ref):
    qb = q_ref.shape[1]
    per_class = (kik_ref.shape[1] // qb) // n_classes
    cls = pl.program_id(1) // per_class
    for n in range(n_classes):
        pl.when(cls == n)(functools.partial(
            _dsa_block, top_k, (n + 1) * per_class * qb, q_ref, iq_ref, vwq_ref, kik_ref, vw_ref, o_ref, key_ref, s_ref))


def _dsa_attention(q, iq, kik, vw, b, s):
    top_k = min(DSA_TOPK_MAX, s // 4)
    qb = min(Q_BLOCK, s)
    n_classes = min(DSA_KEY_CLASSES, s // qb)
    r3 = lambda a: a.reshape(b, s, a.shape[-1])
    blk = lambda w: pl.BlockSpec((1, qb, w), lambda i, j: (i, j, 0))
    seq = lambda w: pl.BlockSpec((1, s, w), lambda i, j: (i, 0, 0))
    return pl.pallas_call(
        functools.partial(_dsa_attn_kernel, top_k, n_classes),
        grid=(b, s // qb),
        in_specs=[blk(MIX_DIM), blk(MIX_DIM), blk(128), seq(128), seq(128)],
        out_specs=blk(MIX_DIM),
        out_shape=jax.ShapeDtypeStruct((b, s, MIX_DIM), F32),
        scratch_shapes=[pltpu.VMEM((qb, s), jnp.int32), pltpu.VMEM((N_HEADS, qb, s), F32)],
        compiler_params=_params("parallel", "arbitrary"),
        name="dsa_attention",
    )(r3(q), r3(iq), r3(vw), r3(kik), r3(vw))


def _dsa_weight(w_dsa):
    d = w_dsa.shape[0]
    q, k, v, iq, ik, iw = jnp.split(w_dsa, np.cumsum([512, 64, 64, 512, 64, 8])[:-1].tolist(), axis=1)
    pad = jnp.zeros((d, DSA_COLS_PAD - 1224), w_dsa.dtype)
    return jnp.concatenate([q, iq, k, ik, v, iw, pad], axis=1)


def _merge_kernel(x_ref, g_ref, ya_ref, yb_ref, wa_ref, wb_ref, wo_ref, o_ref):
    d = x_ref.shape[1]
    ga = _sigmoid(g_ref[:, 0:d])
    gb = _sigmoid(g_ref[:, d:2 * d])
    merged = ga * _bdot(ya_ref[...], wa_ref[...]) + gb * _bdot(yb_ref[...], wb_ref[...])
    o_ref[...] = x_ref[...] + _bdot(merged, wo_ref[...])


def _merge(x, gates, ya, yb, wa, wb, wo):
    t, d = x.shape
    tm = min(ROW_TILE, t)
    tok = lambda w: pl.BlockSpec((tm, w), lambda i: (i, 0))
    return pl.pallas_call(
        _merge_kernel,
        grid=(t // tm,),
        in_specs=[tok(d), tok(2 * d), tok(MIX_DIM), tok(MIX_DIM), _full(wa.shape), _full(wb.shape), _full(wo.shape)],
        out_specs=tok(d),
        out_shape=jax.ShapeDtypeStruct((t, d), F32),
        compiler_params=_params("parallel"),
        name="merge_out",
    )(x, gates, ya, yb, wa, wb, wo)


def _mem_kv_kernel(mem_ref, g_ref, wkv_ref, kn_ref, k_ref, v_ref):
    dm = k_ref.shape[2]
    kv = _dot(_rms(mem_ref[0], g_ref[...]).astype(BF16), wkv_ref[...])
    k = kv[:, 0:dm]
    ss = _dot_x2(k * k, _head_ones(dm, MEM_HEAD_DIM)) * (1.0 / MEM_HEAD_DIM)
    k_ref[0] = k * lax.rsqrt(ss + NORM_EPS) * kn_ref[...]
    v_ref[0] = kv[:, dm:2 * dm]


def _mem_kv(mem, g, wkv, k_norm):
    b, m, d = mem.shape
    dm = MEM_HEADS * MEM_HEAD_DIM
    kn = jnp.tile(k_norm, MEM_HEADS).reshape(1, dm)
    spec = pl.BlockSpec((1, m, dm), lambda i: (i, 0, 0))
    return pl.pallas_call(
        _mem_kv_kernel,
        grid=(b,),
        in_specs=[pl.BlockSpec((1, m, d), lambda i: (i, 0, 0)), _full((1, d)), _full(wkv.shape), _full((1, dm))],
        out_specs=(spec, spec),
        out_shape=(jax.ShapeDtypeStruct((b, m, dm), F32),) * 2,
        compiler_params=_params("parallel"),
        name="mem_kv",
    )(mem, g.reshape(1, d), wkv, kn)


def _mem_attn_kernel(x_ref, g_ref, wq_ref, qn_ref, k_ref, v_ref, wo_ref, o_ref):
    x = x_ref[0]
    dm = wq_ref.shape[1]
    q = _dot(_rms(x, g_ref[...]).astype(BF16), wq_ref[...])
    ss = _dot_x2(q * q, _head_ones(dm, MEM_HEAD_DIM)) * (1.0 / MEM_HEAD_DIM)
    q = q * lax.rsqrt(ss + NORM_EPS) * qn_ref[...] * (MEM_HEAD_DIM ** -0.5)
    outs = []
    for h in range(MEM_HEADS):
        sl = slice(h * MEM_HEAD_DIM, (h + 1) * MEM_HEAD_DIM)
        s = _dot_nt(q[:, sl].astype(BF16), k_ref[0, :, sl].astype(BF16))
        p = jnp.exp(s - jnp.max(s, axis=1, keepdims=True))
        p = p / jnp.sum(p, axis=1, keepdims=True)
        outs.append(_dot(p.astype(BF16), v_ref[0, :, sl].astype(BF16)))
    o = jnp.concatenate(outs, axis=1)
    o_ref[0] = x + _bdot(o, wo_ref[...])


def _mem_attention(x, g, wq, q_norm, k, v, wo):
    b, s, d = x.shape
    m, dm = k.shape[1], k.shape[2]
    tm = min(ROW_TILE, s)
    qn = jnp.tile(q_norm, MEM_HEADS).reshape(1, dm)
    tok = pl.BlockSpec((1, tm, d), lambda i, j: (i, j, 0))
    kv = pl.BlockSpec((1, m, dm), lambda i, j: (i, 0, 0))
    return pl.pallas_call(
        _mem_attn_kernel,
        grid=(b, s // tm),
        in_specs=[tok, _full((1, d)), _full(wq.shape), _full((1, dm)), kv, kv, _full(wo.shape)],
        out_specs=tok,
        out_shape=jax.ShapeDtypeStruct((b, s, d), F32),
        compiler_params=_params("parallel", "parallel"),
        name="mem_attention",
    )(x, g.reshape(1, d), wq, qn, k, v, wo)


def _silu(x):
    return x * _sigmoid(x)


def _ffn_kernel(x_ref, g_ref, wg_ref, wu_ref, wd_ref, o_ref, h_ref):
    j = pl.program_id(1)

    @pl.when(j == 0)
    def _():
        x = x_ref[...]
        h_ref[...] = _rms(x, g_ref[...]).astype(BF16)
        o_ref[...] = x

    h = h_ref[...]
    act = _silu(_dot(h, wg_ref[...])) * _dot(h, wu_ref[...])
    o_ref[...] += _bdot(act, wd_ref[...])


def _ffn(x, g, wg, wu, wd, n_f=2):
    t, d = x.shape
    f = wg.shape[1]
    tm = min(FFN_ROW_TILE, t)
    tf = f // n_f
    tok = pl.BlockSpec((tm, d), lambda i, j: (i, 0))
    return pl.pallas_call(
        _ffn_kernel,
        grid=(t // tm, n_f),
        in_specs=[tok, _full((1, d)), pl.BlockSpec((d, tf), lambda i, j: (0, j)),
                  pl.BlockSpec((d, tf), lambda i, j: (0, j)), pl.BlockSpec((tf, d), lambda i, j: (j, 0))],
        out_specs=tok,
        out_shape=jax.ShapeDtypeStruct((t, d), F32),
        scratch_shapes=[pltpu.VMEM((tm, d), BF16)],
        compiler_params=_params("parallel", "arbitrary"),
        name="ffn_swiglu",
    )(x, g.reshape(1, d), wg, wu, wd)


def _moe_kernel(x_ref, g_ref, r_ref, b_ref, wg_ref, wu_ref, wd_ref, o_ref, h_ref, gate_ref):
    e = pl.program_id(1)

    @pl.when(e == 0)
    def _():
        x = x_ref[...]
        hn = _rms(x, g_ref[...])
        h_ref[...] = hn.astype(BF16)
        o_ref[...] = x
        logits = _dot(hn, r_ref[...], HIGHEST) + b_ref[...]
        lane = lax.broadcasted_iota(jnp.int32, logits.shape, 1)
        logits = jnp.where(lane < N_EXPERTS, logits, -jnp.inf)
        m1 = jnp.max(logits, axis=1, keepdims=True)
        i1 = jnp.min(jnp.where(logits == m1, lane, 128), axis=1, keepdims=True)
        rest = jnp.where(lane == i1, -jnp.inf, logits)
        m2 = jnp.max(rest, axis=1, keepdims=True)
        i2 = jnp.min(jnp.where(rest == m2, lane, 128), axis=1, keepdims=True)
        e2 = jnp.exp(m2 - m1)
        w1 = 1.0 / (1.0 + e2)
        w2 = e2 / (1.0 + e2)
        gate_ref[...] = jnp.where(lane == i1, w1, 0.0) + jnp.where(lane == i2, w2, 0.0)

    h = h_ref[...]
    lane = lax.broadcasted_iota(jnp.int32, gate_ref.shape, 1)
    gate = jnp.sum(jnp.where(lane == e, gate_ref[...], 0.0), axis=1, keepdims=True)
    act = _silu(_dot(h, wg_ref[0])) * _dot(h, wu_ref[0])
    o_ref[...] += gate * _bdot(act, wd_ref[0])


def _moe(x, g, router, bias, wg, wu, wd):
    t, d = x.shape
    n_e, _, f = wg.shape
    tm = min(FFN_ROW_TILE, t)
    r_pad = jnp.zeros((d, 128), F32).at[:, 0:n_e].set(router)
    b_pad = jnp.zeros((1, 128), F32).at[0, 0:n_e].set(bias)
    tok = pl.BlockSpec((tm, d), lambda i, j: (i, 0))
    return pl.pallas_call(
        _moe_kernel,
        grid=(t // tm, n_e),
        in_specs=[tok, _full((1, d)), _full((d, 128)), _full((1, 128)),
                  pl.BlockSpec((1, d, f), lambda i, j: (j, 0, 0)), pl.BlockSpec((1, d, f), lambda i, j: (j, 0, 0)),
                  pl.BlockSpec((1, f, d), lambda i, j: (j, 0, 0))],
        out_specs=tok,
        out_shape=jax.ShapeDtypeStruct((t, d), F32),
        scratch_shapes=[pltpu.VMEM((tm, d), BF16), pltpu.VMEM((tm, 128), F32)],
        compiler_params=_params("parallel", "arbitrary"),
        name="moe_swiglu",
    )(x, g.reshape(1, d), r_pad, b_pad, wg, wu, wd)


def kernel(x, mem, positions, norm_mix, w_in, rwkv_mu, rwkv_w0, rwkv_w2, rwkv_a0, rwkv_a2, rwkv_g2, rwkv_v0, rwkv_v1, rwkv_v2, rwkv_kk, rwkv_ka, rwkv_rk, rwkv_lnx_g, rwkv_lnx_b, dsa_q_norm, dsa_k_norm, idx_k_norm, w_branch_a, w_branch_b, w_out, norm_mem, mem_tok_norm, mem_wq, mem_wkv, mem_q_norm, mem_k_norm, mem_wo, norm_ffn, ffn_wg, ffn_wu, ffn_wd, moe_router, moe_bias, moe_wg, moe_wu, moe_wd):
    w = dict(rwkv_mu=rwkv_mu, rwkv_w0=rwkv_w0, rwkv_w2=rwkv_w2, rwkv_a0=rwkv_a0, rwkv_a2=rwkv_a2,
             rwkv_g2=rwkv_g2, rwkv_v0=rwkv_v0, rwkv_v1=rwkv_v1, rwkv_v2=rwkv_v2, rwkv_kk=rwkv_kk,
             rwkv_ka=rwkv_ka, rwkv_rk=rwkv_rk, rwkv_lnx_g=rwkv_lnx_g, rwkv_lnx_b=rwkv_lnx_b)
    b, s, d = x.shape
    t = b * s
    depth = w_in.shape[0]
    n_a = w_in.shape[2] - 1224 - 2 * d
    bf = lambda a: a.astype(BF16)
    cos, sin = _rope_tables(positions)
    x = x.reshape(t, d)
    v_first = None
    for l in range(depth):
        w_a = bf(w_in[l, :, 0:n_a])
        w_b = bf(_dsa_weight(w_in[l, :, n_a:n_a + 1224]))
        w_g = bf(w_in[l, :, n_a + 1224:])
        cols_a, cols_b, gates = _norm_matmul(x, norm_mix[l], (w_a, w_b, w_g))
        y_a, v_first = _rwkv_branch(cols_a.reshape(b, s, n_a), v_first, _rwkv_params(w, l))
        q, iq, kik, vw = _dsa_prep(cols_b, cos, sin, dsa_q_norm[l], dsa_k_norm[l], idx_k_norm[l])
        y_b = _dsa_attention(q, iq, kik, vw, b, s)
        x = _merge(x, gates, y_a.reshape(t, MIX_DIM), y_b.reshape(t, MIX_DIM),
                   bf(w_branch_a[l]), bf(w_branch_b[l]), bf(w_out[l]))
        mk, mv = _mem_kv(mem, mem_tok_norm[l], bf(mem_wkv[l]), mem_k_norm[l])
        x = _mem_attention(x.reshape(b, s, d), norm_mem[l], bf(mem_wq[l]), mem_q_norm[l], mk, mv,
                           bf(mem_wo[l])).reshape(t, d)
        j = l // 2
        if l % 2 == 0:
            x = _ffn(x, norm_ffn[l], bf(ffn_wg[j]), bf(ffn_wu[j]), bf(ffn_wd[j]))
        else:
            x = _moe(x, norm_ffn[l], moe_router[j], moe_bias[j], bf(moe_wg[j]), bf(moe_wu[j]), bf(moe_wd[j]))
    return x.reshape(b, s, d)
```

```python
import functools

import jax
import jax.numpy as jnp
import numpy as np
from jax import lax
from jax.experimental import pallas as pl
from jax.experimental.pallas import tpu as pltpu

F32 = jnp.float32
BF16 = jnp.bfloat16
HIGHEST = lax.Precision.HIGHEST

NORM_EPS = 1e-6
NEG_INF = -1e30
ROPE_THETA = 10000.0
LOG2_E = float(np.log2(np.e))

CHUNK = 64
Q_BLOCK = 128
DSA_TOPK_MAX = 256
DSA_KEY_CLASSES = 8
DSA_ROW_GROUPS = 4
DSA_RADIX4_MAX_KEYS = 768
HEAD_DIM = 64
N_HEADS = 8
MIX_DIM = N_HEADS * HEAD_DIM
RWKV_LNX_EPS = 1e-5 * HEAD_DIM
MEM_HEADS = 4
MEM_HEAD_DIM = 128
N_EXPERTS = 8

RWKV_CHUNK = 64
RWKV_BLOCK = 256
ROW_TILE = 512
FFN_ROW_TILE = 1024
MOE_SUB = 288
VMEM_LIMIT = 56 * 1024 * 1024


def _params(*sem):
    return pltpu.CompilerParams(dimension_semantics=sem, vmem_limit_bytes=VMEM_LIMIT)


def _full(shape):
    nd = len(shape)
    return pl.BlockSpec(shape, lambda *_: (0,) * nd)


def _dot(a, b, precision=None):
    return jnp.dot(a, b, preferred_element_type=F32, precision=precision)


def _dot_nt(a, b, precision=None):
    return lax.dot_general(a, b, (((1,), (1,)), ((), ())), preferred_element_type=F32,
                           precision=precision)


def _dot_tn(a, b, precision=None):
    return lax.dot_general(a, b, (((0,), (0,)), ((), ())), preferred_element_type=F32,
                           precision=precision)


def _bdot(a, b):
    return _dot(a.astype(BF16), b.astype(BF16))


def _split(x):
    hi = x.astype(BF16)
    return hi, (x - hi.astype(F32)).astype(BF16)


def _dot_x2(x, m):
    hi, lo = _split(x)
    return _dot(hi, m) + _dot(lo, m)


def _dot_x3(x, w_hi, w_lo):
    hi, lo = _split(x)
    return _dot(hi, w_hi) + (_dot(lo, w_hi) + _dot(hi, w_lo))


def _rms(x, g):
    return x * lax.rsqrt(jnp.mean(x * x, axis=-1, keepdims=True) + NORM_EPS) * g


def _sigmoid(x):
    return 1.0 / (1.0 + jnp.exp(-x))


def _head_ones(width, head):
    r = lax.broadcasted_iota(jnp.int32, (width, width), 0) // head
    c = lax.broadcasted_iota(jnp.int32, (width, width), 1) // head
    return (r == c).astype(BF16)


def _norm_matmul_kernel(x_ref, g_ref, *refs):
    n = len(refs) // 2
    h = _rms(x_ref[...], g_ref[...]).astype(BF16)
    for w_ref, o_ref in zip(refs[:n], refs[n:]):
        o_ref[...] = _dot(h, w_ref[...])


def _norm_matmul(x, g, ws, tm=ROW_TILE // 2):
    t, d = x.shape
    tm = min(tm, t)
    return pl.pallas_call(
        _norm_matmul_kernel,
        grid=(t // tm,),
        in_specs=[pl.BlockSpec((tm, d), lambda i: (i, 0)), _full((1, d))] + [_full(w.shape) for w in ws],
        out_specs=tuple(pl.BlockSpec((tm, w.shape[1]), lambda i: (i, 0)) for w in ws),
        out_shape=tuple(jax.ShapeDtypeStruct((t, w.shape[1]), F32) for w in ws),
        compiler_params=_params("parallel"),
        name="norm_matmul",
    )(x, g.reshape(1, d), *ws)


def _rwkv_kernel(has_vmix, *refs):
    if has_vmix:
        (cols_ref, vf_ref, mu_ref, w0_ref, a0_ref, wah_ref, wal_ref, g2_ref, kk_ref, ka_ref,
         rk_ref, lg_ref, lb_ref, v0_ref, v1_ref, v2_ref,
         y_ref, st_ref, prev_ref, at_ref, rt_ref, bt_ref, kt_ref, bh_ref, kh_ref, vv_ref,
         gc_ref, bonus_ref, gate_ref, yy_ref, y0_ref, p_ref, rb_ref, z_ref, lhs_ref, dl_ref) = refs
        vout_ref = None
    else:
        (cols_ref, mu_ref, w0_ref, a0_ref, wah_ref, wal_ref, g2_ref, kk_ref, ka_ref,
         rk_ref, lg_ref, lb_ref,
         y_ref, vout_ref, st_ref, prev_ref, at_ref, rt_ref, bt_ref, kt_ref, bh_ref, kh_ref, vv_ref,
         gc_ref, bonus_ref, gate_ref, yy_ref, y0_ref, p_ref, rb_ref, z_ref, lhs_ref, dl_ref) = refs

    tb = cols_ref.shape[1]
    c = RWKV_CHUNK
    n_chunks = tb // c
    hd = HEAD_DIM

    @pl.when(pl.program_id(1) == 0)
    def _():
        st_ref[...] = jnp.zeros_like(st_ref)
        prev_ref[...] = jnp.zeros_like(prev_ref)

    cols = cols_ref[0]
    row = lax.broadcasted_iota(jnp.int32, cols.shape, 0)
    shifted = jnp.where(row == 0, prev_ref[...], pltpu.roll(cols, 1, 0))
    prev_ref[...] = cols[tb - 1:tb, :]
    mixed = cols + (shifted - cols) * mu_ref[...]

    r = mixed[:, 0:MIX_DIM]
    k = mixed[:, MIX_DIM:2 * MIX_DIM]
    v = mixed[:, 2 * MIX_DIM:3 * MIX_DIM]
    wa = mixed[:, 3 * MIX_DIM:3 * MIX_DIM + 128]
    gd = mixed[:, 3 * MIX_DIM + 128:3 * MIX_DIM + 256]
    lane = lax.broadcasted_iota(jnp.int32, wa.shape, 1)
    wa = jnp.where(lane < 64, jnp.tanh(wa), wa)
    wa_out = _dot_x3(wa, wah_ref[...], wal_ref[...])
    u_dec = w0_ref[...] + wa_out[:, 0:MIX_DIM]
    a = _sigmoid(a0_ref[...] + wa_out[:, MIX_DIM:2 * MIX_DIM])
    gate_ref[...] = _bdot(_sigmoid(gd), g2_ref[...])
    if has_vmix:
        vm = _bdot(_bdot(v, v1_ref[...]), v2_ref[...])
        v = v + (vf_ref[0] - v) * _sigmoid(v0_ref[...] + vm)
    else:
        vout_ref[0] = v

    hb = _head_ones(MIX_DIM, hd)
    kk = k * kk_ref[...]
    kk = kk / jnp.maximum(jnp.sqrt(_dot_x2(kk * kk, hb)), 1e-12)
    k = k * (1.0 + (a - 1.0) * ka_ref[...])
    bonus_ref[...] = _bdot(r * k * rk_ref[...], hb) * v
    ll = -_sigmoid(u_dec) * float(np.exp(-0.5))

    ri = lax.broadcasted_iota(jnp.int32, (tb, tb), 0)
    ci = lax.broadcasted_iota(jnp.int32, (tb, tb), 1)
    same = (ri // c) == (ci // c)
    strict_f = (same & (ri > ci)).astype(F32)
    incl_f = (same & (ri >= ci)).astype(F32)
    ll_hi, ll_lo = _split(ll)
    tri = incl_f.astype(BF16)
    blk = same.astype(BF16)
    cum = _dot(tri, ll_hi) + _dot(tri, ll_lo)
    tot = _dot(blk, ll_hi) + _dot(blk, ll_lo)
    er = lax.broadcasted_iota(jnp.int32, (tb, n_chunks * 128), 0) // c
    ec = lax.broadcasted_iota(jnp.int32, (tb, n_chunks * 128), 1) // 128
    sel = (er == ec).astype(BF16)
    gc_ref[...] = jnp.exp(_dot_tn(ll_hi, sel) + _dot_tn(ll_lo, sel))
    g_inv = jnp.exp(-cum)
    g_tail = jnp.exp(tot - cum)
    beta = kk * a
    at_ref[...] = (-kk * jnp.exp(cum - ll)).astype(BF16)
    rt_ref[...] = (r * jnp.exp(cum)).astype(BF16)
    bt_ref[...] = (beta * g_inv).astype(BF16)
    kt_ref[...] = (k * g_inv).astype(BF16)
    bh_ref[...] = (beta * g_tail).astype(BF16)
    kh_ref[...] = (k * g_tail).astype(BF16)
    vv_ref[...] = v.astype(BF16)

    n_levels = int(np.log2(c))
    heads = [slice(h * hd, (h + 1) * hd) for h in range(N_HEADS)]

    for h, cs in enumerate(heads):
        a_t = at_ref[:, cs]
        r_t = rt_ref[:, cs]
        b_t = bt_ref[:, cs]
        k_t = kt_ref[:, cs]
        v_c = vv_ref[:, cs]
        p_ref[h] = (_dot_nt(a_t, b_t) * strict_f).astype(BF16)
        m_mat = (_dot_nt(a_t, k_t) * strict_f).astype(BF16)
        rb_ref[h] = (_dot_nt(r_t, b_t) * incl_f).astype(BF16)
        rk = (_dot_nt(r_t, k_t) * incl_f).astype(BF16)
        y0_ref[:, cs] = _dot(rk, v_c)
        z_ref[h] = jnp.concatenate([a_t.astype(F32), _dot(m_mat, v_c)], axis=1)

    for lev in range(n_levels):
        for h in range(N_HEADS):
            p = p_ref[h]
            z = z_ref[h]
            z_ref[h] = z + _dot(p, z.astype(BF16))
            if lev < n_levels - 1:
                p_ref[h] = _dot(p, p).astype(BF16)

    for h, cs in enumerate(heads):
        z_b = z_ref[h].astype(BF16)
        w = _dot(rb_ref[h], z_b)
        r_bar = (rt_ref[:, cs].astype(F32) + w[:, 0:hd]).astype(BF16)
        y0_ref[:, cs] = y0_ref[:, cs] + w[:, hd:2 * hd]
        for ch in range(n_chunks):
            rows = slice(ch * c, (ch + 1) * c)
            pd = _dot_tn(bh_ref[rows, cs], z_b[rows])
            d2 = _dot_tn(kh_ref[rows, cs], vv_ref[rows, cs])
            lhs_ref[h, ch] = jnp.concatenate([r_bar[rows], pd[:, 0:hd].astype(BF16)], axis=0)
            dl_ref[h, ch] = pd[:, hd:2 * hd] + d2

    for ch in range(n_chunks):
        rows = slice(ch * c, (ch + 1) * c)
        for h, cs in enumerate(heads):
            st = st_ref[h]
            res = _dot(lhs_ref[h, ch], st.astype(BF16))
            yy_ref[rows, cs] = res[0:c] + y0_ref[rows, cs]
            g_col = gc_ref[h * hd:(h + 1) * hd, ch * 128:ch * 128 + hd]
            st_ref[h] = st * g_col + res[c:2 * c] + dl_ref[h, ch]

    y = yy_ref[...]
    mean = _bdot(y, hb) * (1.0 / hd)
    yc = y - mean
    var = _bdot(yc * yc, hb) * (1.0 / hd)
    yn = yc * lax.rsqrt(var + RWKV_LNX_EPS) * lg_ref[...] + lb_ref[...]
    y_ref[0] = ((yn + bonus_ref[...]) * gate_ref[...]).astype(y_ref.dtype)


def _rwkv_params(w, l):
    p = {k: w["rwkv_" + k][l] for k in ("mu", "w0", "w2", "a0", "a2", "g2", "kk", "ka", "rk", "lnx_g", "lnx_b")}
    if l > 0:
        p.update({k: w["rwkv_" + k][l - 1] for k in ("v0", "v1", "v2")})
    return p


def _rwkv_branch(cols, v_first, p):
    b, s, nc = cols.shape
    tb = min(RWKV_BLOCK, s)
    has_vmix = v_first is not None
    row = lambda a: a.reshape(1, -1)
    pair = lambda a: list(_split(a))
    tok = lambda w: pl.BlockSpec((1, tb, w), lambda i, j: (i, j, 0))
    wa2 = jnp.zeros((128, 2 * MIX_DIM), F32)
    wa2 = wa2.at[0:64, 0:MIX_DIM].set(p["w2"]).at[64:128, MIX_DIM:].set(p["a2"])
    ins = [cols] + ([v_first] if has_vmix else []) + [
        row(p["mu"]), row(p["w0"]), row(p["a0"])] + pair(wa2) + [p["g2"].astype(BF16)] + [
        row(p["kk"]), row(p["ka"]), row(p["rk"]), row(p["lnx_g"]), row(p["lnx_b"])]
    if has_vmix:
        ins += [row(p["v0"]), p["v1"].astype(BF16), p["v2"].astype(BF16)]
    in_specs = [tok(nc)] + ([tok(MIX_DIM)] if has_vmix else []) + [_full(a.shape) for a in ins[1 + has_vmix:]]
    y_shape = jax.ShapeDtypeStruct((b, s, MIX_DIM), F32)
    out_shape = y_shape if has_vmix else (y_shape, y_shape)
    out_specs = tok(MIX_DIM) if has_vmix else (tok(MIX_DIM), tok(MIX_DIM))
    tok_bf = pltpu.VMEM((tb, MIX_DIM), BF16)
    tok_f32 = pltpu.VMEM((tb, MIX_DIM), F32)
    n_ch = tb // RWKV_CHUNK
    scratch = ([pltpu.VMEM((N_HEADS, HEAD_DIM, HEAD_DIM), F32), pltpu.VMEM((1, nc), F32)] + [tok_bf] * 7
               + [pltpu.VMEM((MIX_DIM, n_ch * 128), F32)] + [tok_f32] * 4
               + [pltpu.VMEM((N_HEADS, tb, tb), BF16)] * 2 + [pltpu.VMEM((N_HEADS, tb, 2 * HEAD_DIM), F32),
                  pltpu.VMEM((N_HEADS, n_ch, 2 * RWKV_CHUNK, HEAD_DIM), BF16),
                  pltpu.VMEM((N_HEADS, n_ch, HEAD_DIM, HEAD_DIM), F32)])
    out = pl.pallas_call(
        functools.partial(_rwkv_kernel, has_vmix),
        grid=(b, s // tb),
        in_specs=in_specs,
        out_specs=out_specs,
        out_shape=out_shape,
        scratch_shapes=scratch,
        compiler_params=_params("parallel", "arbitrary"),
        name="rwkv7_vmix" if has_vmix else "rwkv7_first",
    )(*ins)
    return (out, v_first) if has_vmix else out


DSA_COLS_PAD = 1280
_DSA_Q, _DSA_IQ, _DSA_KIK, _DSA_VW = 0, 512, 1024, 1152


def _rope_table_kernel(pos_ref, freq_ref, cos_ref, sin_ref):
    ang = pos_ref[...] * freq_ref[...]
    lane = lax.broadcasted_iota(jnp.int32, ang.shape, 1)
    cos_ref[...] = jnp.cos(ang)
    sin_ref[...] = jnp.where(lane % HEAD_DIM < HEAD_DIM // 2, -jnp.sin(ang), jnp.sin(ang))


def _rope_tables(positions):
    t = positions.size
    tm = min(ROW_TILE, t)
    inv_freq = 1.0 / (ROPE_THETA ** (jnp.arange(0, HEAD_DIM, 2, dtype=F32) / HEAD_DIM))
    freq = jnp.tile(inv_freq, 4).reshape(1, 128)
    pos = positions.reshape(t, 1).astype(F32)
    spec = pl.BlockSpec((tm, 128), lambda i: (i, 0))
    return pl.pallas_call(
        _rope_table_kernel,
        grid=(t // tm,),
        in_specs=[pl.BlockSpec((tm, 1), lambda i: (i, 0)), _full((1, 128))],
        out_specs=(spec, spec),
        out_shape=(jax.ShapeDtypeStruct((t, 128), F32),) * 2,
        compiler_params=_params("parallel"),
        name="rope_tables",
    )(pos, freq)


def _rope128(x, cos, sin):
    lane = lax.broadcasted_iota(jnp.int32, x.shape, 1)
    half = HEAD_DIM // 2
    rot = jnp.where(lane % HEAD_DIM < half, pltpu.roll(x, 128 - half, 1), pltpu.roll(x, half, 1))
    return x * cos + rot * sin


def _dsa_prep_kernel(cols_ref, cos_ref, sin_ref, qn_ref, kn_ref, q_ref, iq_ref, kik_ref, vw_ref):
    cos, sin = cos_ref[...], sin_ref[...]
    q = cols_ref[:, _DSA_Q:_DSA_Q + MIX_DIM]
    ss = _dot_x2(q * q, _head_ones(MIX_DIM, HEAD_DIM)) * (1.0 / HEAD_DIM)
    q = q * lax.rsqrt(ss + NORM_EPS) * qn_ref[...]
    kik = cols_ref[:, _DSA_KIK:_DSA_KIK + 128]
    ss = _dot_x2(kik * kik, _head_ones(128, HEAD_DIM)) * (1.0 / HEAD_DIM)
    kik_ref[...] = _rope128(kik * lax.rsqrt(ss + NORM_EPS) * kn_ref[...], cos, sin)
    for j in range(MIX_DIM // 128):
        sl = slice(j * 128, (j + 1) * 128)
        q_ref[:, sl] = _rope128(q[:, sl], cos, sin) * (HEAD_DIM ** -0.5 * LOG2_E)
        iq_ref[:, sl] = _rope128(cols_ref[:, _DSA_IQ + j * 128:_DSA_IQ + (j + 1) * 128], cos, sin) * (HEAD_DIM ** -0.5)
    vw_ref[...] = cols_ref[:, _DSA_VW:_DSA_VW + 128]


def _dsa_prep(cols, cos, sin, q_norm, k_norm, idx_k_norm):
    t = cols.shape[0]
    tm = min(ROW_TILE, t)
    qn = jnp.tile(q_norm, N_HEADS).reshape(1, MIX_DIM)
    kn = jnp.concatenate([k_norm, idx_k_norm]).reshape(1, 128)
    tok = lambda w: pl.BlockSpec((tm, w), lambda i: (i, 0))
    sds = lambda w: jax.ShapeDtypeStruct((t, w), F32)
    return pl.pallas_call(
        _dsa_prep_kernel,
        grid=(t // tm,),
        in_specs=[tok(DSA_COLS_PAD), tok(128), tok(128), _full((1, MIX_DIM)), _full((1, 128))],
        out_specs=(tok(MIX_DIM), tok(MIX_DIM), tok(128), tok(128)),
        out_shape=(sds(MIX_DIM), sds(MIX_DIM), sds(128), sds(128)),
        compiler_params=_params("parallel"),
        name="dsa_prep",
    )(cols, cos, sin, qn, kn)


def _count(mask):
    return jnp.sum(jnp.where(mask, 1.0, 0.0), axis=1, keepdims=True)


def _dsa_block(top_k, n_keys, q_ref, iq_ref, vwq_ref, kik_ref, vw_ref, o_ref, key_ref, s_ref):
    qb = q_ref.shape[1]
    start = pl.program_id(1) * qb
    k = kik_ref[0, 0:n_keys, 0:HEAD_DIM].astype(BF16)
    ik = kik_ref[0, 0:n_keys, HEAD_DIM:2 * HEAD_DIM].astype(BF16)
    v = vw_ref[0, 0:n_keys, 0:HEAD_DIM].astype(BF16)
    iw = vwq_ref[0, :, HEAD_DIM:HEAD_DIM + N_HEADS] * (N_HEADS ** -0.5)

    score = jnp.zeros((qb, n_keys), F32)
    for h in range(N_HEADS):
        lg = _dot_nt(iq_ref[0, :, h * HEAD_DIM:(h + 1) * HEAD_DIM].astype(BF16), ik)
        score = score + jnp.maximum(lg, 0.0) * iw[:, h:h + 1]
    key_pos = lax.broadcasted_iota(jnp.int32, (qb, n_keys), 1)
    q_pos = start + lax.broadcasted_iota(jnp.int32, (qb, 1), 0)
    limit = (q_pos // CHUNK + 1) * CHUNK
    adm = key_pos < limit
    score = jnp.where(adm, score, NEG_INF)
    score = jnp.where(score == 0.0, 0.0, score)

    bits = pltpu.bitcast(score, jnp.int32)
    key = bits ^ ((bits >> 31) & jnp.int32(0x7FFFFFFF))
    kf = float(top_k)
    key_ref[:, 0:n_keys] = key
    gr = qb // DSA_ROW_GROUPS
    digit_bits = 2 if n_keys <= DSA_RADIX4_MAX_KEYS else 1
    n_steps = 32 // digit_bits

    def theta_body(i, ths):
        unit = jnp.int32(1) << (32 - digit_bits - digit_bits * i)
        out = []
        for g in range(DSA_ROW_GROUPS):
            keys = key_ref[g * gr:(g + 1) * gr, 0:n_keys]
            passed = jnp.zeros((gr, 1), jnp.int32)
            for d in range(1, 2 ** digit_bits):
                cnt = _count(keys >= ths[g] + unit * d)
                passed = passed + jnp.where(cnt >= kf, 1, 0)
            out.append(ths[g] + unit * passed)
        return tuple(out)

    lowest = jnp.full((gr, 1), -2 ** 31, jnp.int32)
    ths = lax.fori_loop(0, n_steps, theta_body, (lowest,) * DSA_ROW_GROUPS, unroll=4 // digit_bits)
    theta = jnp.concatenate(ths, axis=0)
    gt = key > theta
    eq = key == theta
    n_gt = _count(gt)
    need = kf - n_gt
    n_bits = int(np.ceil(np.log2(n_keys))) + 1

    def tie_search():
        def tie_body(i, jb):
            cand = jb + (jnp.int32(1) << (n_bits - 1 - i))
            ok = (cand <= n_keys) & (_count(eq & (key_pos < cand)) <= need)
            return jnp.where(ok, cand, jb)

        return lax.fori_loop(0, n_bits, tie_body, jnp.zeros((qb, 1), jnp.int32))

    has_tie = jnp.max(n_gt + _count(eq)) > kf
    bound = lax.cond(has_tie, tie_search, lambda: jnp.full((qb, 1), n_keys, jnp.int32))
    bias = jnp.where((gt | (eq & (key_pos < bound))) & adm, 0.0, NEG_INF)

    tops = []
    for h in range(N_HEADS):
        s = _dot_nt(q_ref[0, :, h * HEAD_DIM:(h + 1) * HEAD_DIM].astype(BF16), k) + bias
        s_ref[h, :, 0:n_keys] = s
        tops.append(jnp.max(s, axis=1, keepdims=True))
    for h in range(N_HEADS):
        p = jnp.exp2(s_ref[h, :, 0:n_keys] - tops[h])
        o = _dot(p.astype(BF16), v) / jnp.sum(p, axis=1, keepdims=True)
        o_ref[0, :, h * HEAD_DIM:(h + 1) * HEAD_DIM] = o


def _dsa_attn_kernel(top_k, n_classes, q_ref, iq_ref, vwq_ref, kik_ref, vw_ref, o_ref, key_ref, s_ref):
    qb = q_ref.shape[1]
    per_class = (kik_ref.shape[1] // qb) // n_classes
    cls = pl.program_id(1) // per_class
    for n in range(n_classes):
        pl.when(cls == n)(functools.partial(
            _dsa_block, top_k, (n + 1) * per_class * qb, q_ref, iq_ref, vwq_ref, kik_ref, vw_ref, o_ref, key_ref, s_ref))


def _dsa_attention(q, iq, kik, vw, b, s):
    top_k = min(DSA_TOPK_MAX, s // 4)
    qb = min(Q_BLOCK, s)
    n_classes = min(DSA_KEY_CLASSES, s // qb)
    r3 = lambda a: a.reshape(b, s, a.shape[-1])
    blk = lambda w: pl.BlockSpec((1, qb, w), lambda i, j: (i, j, 0))
    seq = lambda w: pl.BlockSpec((1, s, w), lambda i, j: (i, 0, 0))
    return pl.pallas_call(
        functools.partial(_dsa_attn_kernel, top_k, n_classes),
        grid=(b, s // qb),
        in_specs=[blk(MIX_DIM), blk(MIX_DIM), blk(128), seq(128), seq(128)],
        out_specs=blk(MIX_DIM),
        out_shape=jax.ShapeDtypeStruct((b, s, MIX_DIM), F32),
        scratch_shapes=[pltpu.VMEM((qb, s), jnp.int32), pltpu.VMEM((N_HEADS, qb, s), F32)],
        compiler_params=_params("parallel", "arbitrary"),
        name="dsa_attention",
    )(r3(q), r3(iq), r3(vw), r3(kik), r3(vw))


def _dsa_weight(w_dsa):
    d = w_dsa.shape[0]
    q, k, v, iq, ik, iw = jnp.split(w_dsa, np.cumsum([512, 64, 64, 512, 64, 8])[:-1].tolist(), axis=1)
    pad = jnp.zeros((d, DSA_COLS_PAD - 1224), w_dsa.dtype)
    return jnp.concatenate([q, iq, k, ik, v, iw, pad], axis=1)


def _merge_kernel(x_ref, g_ref, ya_ref, yb_ref, wa_ref, wb_ref, wo_ref, o_ref):
    d = x_ref.shape[1]
    ga = _sigmoid(g_ref[:, 0:d])
    gb = _sigmoid(g_ref[:, d:2 * d])
    merged = ga * _bdot(ya_ref[...], wa_ref[...]) + gb * _bdot(yb_ref[...], wb_ref[...])
    o_ref[...] = x_ref[...] + _bdot(merged, wo_ref[...])


def _merge(x, gates, ya, yb, wa, wb, wo):
    t, d = x.shape
    tm = min(ROW_TILE, t)
    tok = lambda w: pl.BlockSpec((tm, w), lambda i: (i, 0))
    return pl.pallas_call(
        _merge_kernel,
        grid=(t // tm,),
        in_specs=[tok(d), tok(2 * d), tok(MIX_DIM), tok(MIX_DIM), _full(wa.shape), _full(wb.shape), _full(wo.shape)],
        out_specs=tok(d),
        out_shape=jax.ShapeDtypeStruct((t, d), F32),
        compiler_params=_params("parallel"),
        name="merge_out",
    )(x, gates, ya, yb, wa, wb, wo)


def _mem_kv_kernel(mem_ref, g_ref, wkv_ref, kn_ref, k_ref, v_ref):
    dm = k_ref.shape[2]
    kv = _dot(_rms(mem_ref[0], g_ref[...]).astype(BF16), wkv_ref[...])
    k = kv[:, 0:dm]
    ss = _dot_x2(k * k, _head_ones(dm, MEM_HEAD_DIM)) * (1.0 / MEM_HEAD_DIM)
    k_ref[0] = k * lax.rsqrt(ss + NORM_EPS) * kn_ref[...]
    v_ref[0] = kv[:, dm:2 * dm]


def _mem_kv(mem, g, wkv, k_norm):
    b, m, d = mem.shape
    dm = MEM_HEADS * MEM_HEAD_DIM
    kn = jnp.tile(k_norm, MEM_HEADS).reshape(1, dm)
    spec = pl.BlockSpec((1, m, dm), lambda i: (i, 0, 0))
    return pl.pallas_call(
        _mem_kv_kernel,
        grid=(b,),
        in_specs=[pl.BlockSpec((1, m, d), lambda i: (i, 0, 0)), _full((1, d)), _full(wkv.shape), _full((1, dm))],
        out_specs=(spec, spec),
        out_shape=(jax.ShapeDtypeStruct((b, m, dm), F32),) * 2,
        compiler_params=_params("parallel"),
        name="mem_kv",
    )(mem, g.reshape(1, d), wkv, kn)


def _mem_attn_kernel(x_ref, g_ref, wq_ref, qn_ref, k_ref, v_ref, wo_ref, o_ref):
    x = x_ref[0]
    dm = wq_ref.shape[1]
    q = _dot(_rms(x, g_ref[...]).astype(BF16), wq_ref[...])
    ss = _dot_x2(q * q, _head_ones(dm, MEM_HEAD_DIM)) * (1.0 / MEM_HEAD_DIM)
    q = q * lax.rsqrt(ss + NORM_EPS) * qn_ref[...] * (MEM_HEAD_DIM ** -0.5)
    outs = []
    for h in range(MEM_HEADS):
        sl = slice(h * MEM_HEAD_DIM, (h + 1) * MEM_HEAD_DIM)
        s = _dot_nt(q[:, sl].astype(BF16), k_ref[0, :, sl].astype(BF16))
        p = jnp.exp(s - jnp.max(s, axis=1, keepdims=True))
        p = p / jnp.sum(p, axis=1, keepdims=True)
        outs.append(_dot(p.astype(BF16), v_ref[0, :, sl].astype(BF16)))
    o = jnp.concatenate(outs, axis=1)
    o_ref[0] = x + _bdot(o, wo_ref[...])


def _mem_attention(x, g, wq, q_norm, k, v, wo):
    b, s, d = x.shape
    m, dm = k.shape[1], k.shape[2]
    tm = min(ROW_TILE, s)
    qn = jnp.tile(q_norm, MEM_HEADS).reshape(1, dm)
    tok = pl.BlockSpec((1, tm, d), lambda i, j: (i, j, 0))
    kv = pl.BlockSpec((1, m, dm), lambda i, j: (i, 0, 0))
    return pl.pallas_call(
        _mem_attn_kernel,
        grid=(b, s // tm),
        in_specs=[tok, _full((1, d)), _full(wq.shape), _full((1, dm)), kv, kv, _full(wo.shape)],
        out_specs=tok,
        out_shape=jax.ShapeDtypeStruct((b, s, d), F32),
        compiler_params=_params("parallel", "parallel"),
        name="mem_attention",
    )(x, g.reshape(1, d), wq, qn, k, v, wo)


def _silu(x):
    return x * _sigmoid(x)


def _ffn_kernel(x_ref, g_ref, wg_ref, wu_ref, wd_ref, o_ref, h_ref):
    j = pl.program_id(1)

    @pl.when(j == 0)
    def _():
        x = x_ref[...]
        h_ref[...] = _rms(x, g_ref[...]).astype(BF16)
        o_ref[...] = x

    h = h_ref[...]
    act = _silu(_dot(h, wg_ref[...])) * _dot(h, wu_ref[...])
    o_ref[...] += _bdot(act, wd_ref[...])


def _ffn(x, g, wg, wu, wd, n_f=2):
    t, d = x.shape
    f = wg.shape[1]
    tm = min(FFN_ROW_TILE, t)
    tf = f // n_f
    tok = pl.BlockSpec((tm, d), lambda i, j: (i, 0))
    return pl.pallas_call(
        _ffn_kernel,
        grid=(t // tm, n_f),
        in_specs=[tok, _full((1, d)), pl.BlockSpec((d, tf), lambda i, j: (0, j)),
                  pl.BlockSpec((d, tf), lambda i, j: (0, j)), pl.BlockSpec((tf, d), lambda i, j: (j, 0))],
        out_specs=tok,
        out_shape=jax.ShapeDtypeStruct((t, d), F32),
        scratch_shapes=[pltpu.VMEM((tm, d), BF16)],
        compiler_params=_params("parallel", "arbitrary"),
        name="ffn_swiglu",
    )(x, g.reshape(1, d), wg, wu, wd)


def _moe_kernel(x_ref, g_ref, rh_ref, rl_ref, b_ref, wg_ref, wu_ref, wd_ref, o_ref, h_ref, gate_ref, rank_ref,
                gate_t_ref, rank_t_ref):
    e = pl.program_id(1)
    tm = x_ref.shape[0]

    @pl.when(e == 0)
    def _():
        x = x_ref[...]
        hn = _rms(x, g_ref[...])
        h_ref[...] = hn.astype(BF16)
        o_ref[...] = x
        logits = _dot_x3(hn, rh_ref[...], rl_ref[...]) + b_ref[...]
        lane = lax.broadcasted_iota(jnp.int32, logits.shape, 1)
        logits = jnp.where(lane < N_EXPERTS, logits, -jnp.inf)
        m1 = jnp.max(logits, axis=1, keepdims=True)
        i1 = jnp.min(jnp.where(logits == m1, lane, 128), axis=1, keepdims=True)
        rest = jnp.where(lane == i1, -jnp.inf, logits)
        m2 = jnp.max(rest, axis=1, keepdims=True)
        i2 = jnp.min(jnp.where(rest == m2, lane, 128), axis=1, keepdims=True)
        e2 = jnp.exp(m2 - m1)
        w1 = 1.0 / (1.0 + e2)
        w2 = e2 / (1.0 + e2)
        gates = jnp.where(lane == i1, w1, 0.0) + jnp.where(lane == i2, w2, 0.0)
        gate_ref[...] = gates
        member = jnp.where(gates != 0.0, 1.0, 0.0).astype(BF16)
        earlier = jnp.where(lax.broadcasted_iota(jnp.int32, (tm, tm), 1)
                            < lax.broadcasted_iota(jnp.int32, (tm, tm), 0), 1.0, 0.0).astype(BF16)
        rank = _dot(earlier, member)
        rank_ref[...] = rank
        gate_t_ref[...] = gates.T
        rank_t_ref[...] = rank.T

    lane = lax.broadcasted_iota(jnp.int32, gate_ref.shape, 1)
    gate_col = jnp.sum(jnp.where(lane == e, gate_ref[...], 0.0), axis=1, keepdims=True)
    rank_col = jnp.sum(jnp.where(lane == e, rank_ref[...], 0.0), axis=1, keepdims=True)
    gate_row = gate_t_ref[pl.ds(e, 1), :]
    rank_row = rank_t_ref[pl.ds(e, 1), :]
    n_tok = jnp.sum(jnp.where(gate_col != 0.0, 1.0, 0.0)).astype(jnp.int32)
    sub = min(MOE_SUB, tm)

    def sub_tile(s_idx, carry):
        base = (s_idx * sub).astype(F32)
        slot_col = base + lax.broadcasted_iota(jnp.int32, (sub, 1), 0).astype(F32)
        slot_row = base + lax.broadcasted_iota(jnp.int32, (1, sub), 1).astype(F32)
        pick = (gate_row != 0.0) & (rank_row == slot_col)
        place = (gate_col != 0.0) & (rank_col == slot_row)
        hs = _dot(jnp.where(pick, 1.0, 0.0).astype(BF16), h_ref[...]).astype(BF16)
        act = _silu(_dot(hs, wg_ref[0])) * _dot(hs, wu_ref[0])
        out = _bdot(act, wd_ref[0]) * jnp.sum(jnp.where(pick, gate_row, 0.0), axis=1, keepdims=True)
        o_ref[...] += _dot(jnp.where(place, 1.0, 0.0).astype(BF16), out.astype(BF16))
        return carry

    lax.fori_loop(0, (n_tok + sub - 1) // sub, sub_tile, 0)


def _moe(x, g, router, bias, wg, wu, wd):
    t, d = x.shape
    n_e, _, f = wg.shape
    tm = min(FFN_ROW_TILE, t)
    r_pad = jnp.zeros((d, 128), F32).at[:, 0:n_e].set(router)
    b_pad = jnp.zeros((1, 128), F32).at[0, 0:n_e].set(bias)
    tok = pl.BlockSpec((tm, d), lambda i, j: (i, 0))
    return pl.pallas_call(
        _moe_kernel,
        grid=(t // tm, n_e),
        in_specs=[tok, _full((1, d)), _full((d, 128)), _full((d, 128)), _full((1, 128)),
                  pl.BlockSpec((1, d, f), lambda i, j: (j, 0, 0)), pl.BlockSpec((1, d, f), lambda i, j: (j, 0, 0)),
                  pl.BlockSpec((1, f, d), lambda i, j: (j, 0, 0))],
        out_specs=tok,
        out_shape=jax.ShapeDtypeStruct((t, d), F32),
        scratch_shapes=[pltpu.VMEM((tm, d), BF16), pltpu.VMEM((tm, 128), F32), pltpu.VMEM((tm, 128), F32),
                        pltpu.VMEM((128, tm), F32), pltpu.VMEM((128, tm), F32)],
        compiler_params=_params("parallel", "arbitrary"),
        name="moe_swiglu",
    )(x, g.reshape(1, d), *_split(r_pad), b_pad, wg, wu, wd)


def kernel(x, mem, positions, norm_mix, w_in, rwkv_mu, rwkv_w0, rwkv_w2, rwkv_a0, rwkv_a2, rwkv_g2, rwkv_v0, rwkv_v1, rwkv_v2, rwkv_kk, rwkv_ka, rwkv_rk, rwkv_lnx_g, rwkv_lnx_b, dsa_q_norm, dsa_k_norm, idx_k_norm, w_branch_a, w_branch_b, w_out, norm_mem, mem_tok_norm, mem_wq, mem_wkv, mem_q_norm, mem_k_norm, mem_wo, norm_ffn, ffn_wg, ffn_wu, ffn_wd, moe_router, moe_bias, moe_wg, moe_wu, moe_wd):
    w = dict(rwkv_mu=rwkv_mu, rwkv_w0=rwkv_w0, rwkv_w2=rwkv_w2, rwkv_a0=rwkv_a0, rwkv_a2=rwkv_a2,
             rwkv_g2=rwkv_g2, rwkv_v0=rwkv_v0, rwkv_v1=rwkv_v1, rwkv_v2=rwkv_v2, rwkv_kk=rwkv_kk,
             rwkv_ka=rwkv_ka, rwkv_rk=rwkv_rk, rwkv_lnx_g=rwkv_lnx_g, rwkv_lnx_b=rwkv_lnx_b)
    b, s, d = x.shape
    t = b * s
    depth = w_in.shape[0]
    n_a = w_in.shape[2] - 1224 - 2 * d
    bf = lambda a: a.astype(BF16)
    cos, sin = _rope_tables(positions)
    x = x.reshape(t, d)
    v_first = None
    for l in range(depth):
        w_a = bf(w_in[l, :, 0:n_a])
        w_b = bf(_dsa_weight(w_in[l, :, n_a:n_a + 1224]))
        w_g = bf(w_in[l, :, n_a + 1224:])
        cols_a, cols_b, gates = _norm_matmul(x, norm_mix[l], (w_a, w_b, w_g))
        y_a, v_first = _rwkv_branch(cols_a.reshape(b, s, n_a), v_first, _rwkv_params(w, l))
        q, iq, kik, vw = _dsa_prep(cols_b, cos, sin, dsa_q_norm[l], dsa_k_norm[l], idx_k_norm[l])
        y_b = _dsa_attention(q, iq, kik, vw, b, s)
        x = _merge(x, gates, y_a.reshape(t, MIX_DIM), y_b.reshape(t, MIX_DIM),
                   bf(w_branch_a[l]), bf(w_branch_b[l]), bf(w_out[l]))
        mk, mv = _mem_kv(mem, mem_tok_norm[l], bf(mem_wkv[l]), mem_k_norm[l])
        x = _mem_attention(x.reshape(b, s, d), norm_mem[l], bf(mem_wq[l]), mem_q_norm[l], mk, mv,
                           bf(mem_wo[l])).reshape(t, d)
        j = l // 2
        if l % 2 == 0:
            x = _ffn(x, norm_ffn[l], bf(ffn_wg[j]), bf(ffn_wu[j]), bf(ffn_wd[j]))
        else:
            x = _moe(x, norm_ffn[l], moe_router[j], moe_bias[j], bf(moe_wg[j]), bf(moe_wu[j]), bf(moe_wd[j]))
    return x.reshape(b, s, d)
```

```python
import functools

import jax
import jax.numpy as jnp
import numpy as np
from jax import lax
from jax.experimental import pallas as pl
from jax.experimental.pallas import tpu as pltpu

F32 = jnp.float32
BF16 = jnp.bfloat16
HIGHEST = lax.Precision.HIGHEST

NORM_EPS = 1e-6
NEG_INF = -1e30
ROPE_THETA = 10000.0
LOG2_E = float(np.log2(np.e))

CHUNK = 64
Q_BLOCK = 128
DSA_TOPK_MAX = 256
DSA_KEY_CLASSES = 8
DSA_ROW_GROUPS = 4
DSA_RADIX4_MAX_KEYS = 768
HEAD_DIM = 64
N_HEADS = 8
MIX_DIM = N_HEADS * HEAD_DIM
RWKV_LNX_EPS = 1e-5 * HEAD_DIM
MEM_HEADS = 4
MEM_HEAD_DIM = 128
N_EXPERTS = 8

RWKV_CHUNK = 64
RWKV_BLOCK = 256
ROW_TILE = 512
FFN_ROW_TILE = 1024
MOE_SUB = 288
VMEM_LIMIT = 56 * 1024 * 1024


def _params(*sem):
    return pltpu.CompilerParams(dimension_semantics=sem, vmem_limit_bytes=VMEM_LIMIT)


def _full(shape):
    nd = len(shape)
    return pl.BlockSpec(shape, lambda *_: (0,) * nd)


def _dot(a, b, precision=None):
    return jnp.dot(a, b, preferred_element_type=F32, precision=precision)


def _dot_nt(a, b, precision=None):
    return lax.dot_general(a, b, (((1,), (1,)), ((), ())), preferred_element_type=F32,
                           precision=precision)


def _dot_tn(a, b, precision=None):
    return lax.dot_general(a, b, (((0,), (0,)), ((), ())), preferred_element_type=F32,
                           precision=precision)


def _bdot(a, b):
    return _dot(a.astype(BF16), b.astype(BF16))


def _split(x):
    hi = x.astype(BF16)
    return hi, (x - hi.astype(F32)).astype(BF16)


def _dot_x2(x, m):
    hi, lo = _split(x)
    return _dot(hi, m) + _dot(lo, m)


def _dot_x3(x, w_hi, w_lo):
    hi, lo = _split(x)
    return _dot(hi, w_hi) + (_dot(lo, w_hi) + _dot(hi, w_lo))


def _rms(x, g):
    return x * lax.rsqrt(jnp.mean(x * x, axis=-1, keepdims=True) + NORM_EPS) * g


def _sigmoid(x):
    return 1.0 / (1.0 + jnp.exp(-x))


def _head_ones(width, head):
    r = lax.broadcasted_iota(jnp.int32, (width, width), 0) // head
    c = lax.broadcasted_iota(jnp.int32, (width, width), 1) // head
    return (r == c).astype(BF16)


def _norm_matmul_kernel(x_ref, g_ref, *refs):
    n = len(refs) // 2
    h = _rms(x_ref[...], g_ref[...]).astype(BF16)
    for w_ref, o_ref in zip(refs[:n], refs[n:]):
        o_ref[...] = _dot(h, w_ref[...])


def _norm_matmul(x, g, ws, tm=ROW_TILE // 2):
    t, d = x.shape
    tm = min(tm, t)
    return pl.pallas_call(
        _norm_matmul_kernel,
        grid=(t // tm,),
        in_specs=[pl.BlockSpec((tm, d), lambda i: (i, 0)), _full((1, d))] + [_full(w.shape) for w in ws],
        out_specs=tuple(pl.BlockSpec((tm, w.shape[1]), lambda i: (i, 0)) for w in ws),
        out_shape=tuple(jax.ShapeDtypeStruct((t, w.shape[1]), F32) for w in ws),
        compiler_params=_params("parallel"),
        name="norm_matmul",
    )(x, g.reshape(1, d), *ws)


def _rwkv_kernel(has_vmix, *refs):
    if has_vmix:
        (cols_ref, vf_ref, mu_ref, w0_ref, a0_ref, wah_ref, wal_ref, g2_ref, kk_ref, ka_ref,
         rk_ref, lg_ref, lb_ref, v0_ref, v1_ref, v2_ref,
         y_ref, st_ref, prev_ref, at_ref, rt_ref, bt_ref, kt_ref, bh_ref, kh_ref, vv_ref,
         gdec_ref, bonus_ref, gate_ref, yy_ref, y0_ref, p_ref, rb_ref, z_ref, rhs_ref, dl_ref) = refs
        vout_ref = None
    else:
        (cols_ref, mu_ref, w0_ref, a0_ref, wah_ref, wal_ref, g2_ref, kk_ref, ka_ref,
         rk_ref, lg_ref, lb_ref,
         y_ref, vout_ref, st_ref, prev_ref, at_ref, rt_ref, bt_ref, kt_ref, bh_ref, kh_ref, vv_ref,
         gdec_ref, bonus_ref, gate_ref, yy_ref, y0_ref, p_ref, rb_ref, z_ref, rhs_ref, dl_ref) = refs

    tb = cols_ref.shape[1]
    c = RWKV_CHUNK
    n_chunks = tb // c
    hd = HEAD_DIM

    @pl.when(pl.program_id(1) == 0)
    def _():
        st_ref[...] = jnp.zeros_like(st_ref)
        prev_ref[...] = jnp.zeros_like(prev_ref)

    cols = cols_ref[0]
    row = lax.broadcasted_iota(jnp.int32, cols.shape, 0)
    shifted = jnp.where(row == 0, prev_ref[...], pltpu.roll(cols, 1, 0))
    prev_ref[...] = cols[tb - 1:tb, :]
    mixed = cols + (shifted - cols) * mu_ref[...]

    r = mixed[:, 0:MIX_DIM]
    k = mixed[:, MIX_DIM:2 * MIX_DIM]
    v = mixed[:, 2 * MIX_DIM:3 * MIX_DIM]
    wa = mixed[:, 3 * MIX_DIM:3 * MIX_DIM + 128]
    gd = mixed[:, 3 * MIX_DIM + 128:3 * MIX_DIM + 256]
    lane = lax.broadcasted_iota(jnp.int32, wa.shape, 1)
    wa = jnp.where(lane < 64, jnp.tanh(wa), wa)
    wa_out = _dot_x3(wa, wah_ref[...], wal_ref[...])
    u_dec = w0_ref[...] + wa_out[:, 0:MIX_DIM]
    a = _sigmoid(a0_ref[...] + wa_out[:, MIX_DIM:2 * MIX_DIM])
    gate_ref[...] = _bdot(_sigmoid(gd), g2_ref[...])
    if has_vmix:
        vm = _bdot(_bdot(v, v1_ref[...]), v2_ref[...])
        v = v + (vf_ref[0] - v) * _sigmoid(v0_ref[...] + vm)
    else:
        vout_ref[0] = v

    hb = _head_ones(MIX_DIM, hd)
    kk = k * kk_ref[...]
    kk = kk / jnp.maximum(jnp.sqrt(_dot_x2(kk * kk, hb)), 1e-12)
    k = k * (1.0 + (a - 1.0) * ka_ref[...])
    bonus_ref[...] = _bdot(r * k * rk_ref[...], hb) * v
    ll = -_sigmoid(u_dec) * float(np.exp(-0.5))

    ri = lax.broadcasted_iota(jnp.int32, (tb, tb), 0)
    ci = lax.broadcasted_iota(jnp.int32, (tb, tb), 1)
    same = (ri // c) == (ci // c)
    strict_u = (same & (ri < ci)).astype(F32)
    incl_u = (same & (ri <= ci)).astype(F32)
    ll_hi, ll_lo = _split(ll)
    tri = (same & (ri >= ci)).astype(BF16)
    cum = _dot(tri, ll_hi) + _dot(tri, ll_lo)
    tot_rows = [cum[(ch + 1) * c - 1:(ch + 1) * c, :] for ch in range(n_chunks)]
    tot = jnp.concatenate([jnp.broadcast_to(t_row, (c, MIX_DIM)) for t_row in tot_rows], axis=0)
    for ch in range(n_chunks):
        gdec_ref[ch:ch + 1, :] = jnp.exp(tot_rows[ch])
    g_inv = jnp.exp(-cum)
    g_tail = jnp.exp(tot - cum)
    beta = kk * a
    at_ref[...] = (-kk * jnp.exp(cum - ll)).T.astype(BF16)
    rt_ref[...] = (r * jnp.exp(cum)).T.astype(BF16)
    vv_ref[...] = v.T.astype(BF16)
    bt_ref[...] = (beta * g_inv).astype(BF16)
    kt_ref[...] = (k * g_inv).astype(BF16)
    bh_ref[...] = (beta * g_tail).astype(BF16)
    kh_ref[...] = (k * g_tail).astype(BF16)

    n_levels = int(np.log2(c))
    heads = [slice(h * hd, (h + 1) * hd) for h in range(N_HEADS)]

    for h, cs in enumerate(heads):
        b_t = bt_ref[:, cs]
        k_t = kt_ref[:, cs]
        a_tr = at_ref[cs, :]
        r_tr = rt_ref[cs, :]
        v_tr = vv_ref[cs, :]
        p_ref[h] = (_dot(b_t, a_tr) * strict_u).astype(BF16)
        m_tr = (_dot(k_t, a_tr) * strict_u).astype(BF16)
        rb_ref[h] = (_dot(b_t, r_tr) * incl_u).astype(BF16)
        rk_tr = (_dot(k_t, r_tr) * incl_u).astype(BF16)
        y0_ref[cs, :] = _dot(v_tr, rk_tr)
        z_ref[h] = jnp.concatenate([a_tr.astype(F32), _dot(v_tr, m_tr)], axis=0)

    for lev in range(n_levels):
        for h in range(N_HEADS):
            p = p_ref[h]
            z = z_ref[h]
            z_ref[h] = z + _dot(z.astype(BF16), p)
            if lev < n_levels - 1:
                p_ref[h] = _dot(p, p).astype(BF16)

    for h, cs in enumerate(heads):
        z_b = z_ref[h].astype(BF16)
        w = _dot(z_b, rb_ref[h])
        r_bar = (rt_ref[cs, :].astype(F32) + w[0:hd]).astype(BF16)
        y0_ref[cs, :] = y0_ref[cs, :] + w[hd:2 * hd]
        for ch in range(n_chunks):
            cols = slice(ch * c, (ch + 1) * c)
            pd = _dot(z_b[:, cols], bh_ref[cols, cs])
            d2 = _dot(vv_ref[cs, cols], kh_ref[cols, cs])
            rhs_ref[h, ch] = jnp.concatenate([pd[0:hd].astype(BF16), r_bar[:, cols]], axis=1)
            dl_ref[h, ch] = pd[hd:2 * hd] + d2

    for ch in range(n_chunks):
        cols = slice(ch * c, (ch + 1) * c)
        for h, cs in enumerate(heads):
            st = st_ref[h]
            res = _dot(st.astype(BF16), rhs_ref[h, ch])
            yy_ref[cs, cols] = res[:, hd:hd + c] + y0_ref[cs, cols]
            st_ref[h] = st * gdec_ref[ch:ch + 1, cs] + res[:, 0:hd] + dl_ref[h, ch]

    y = yy_ref[...].T
    mean = _bdot(y, hb) * (1.0 / hd)
    yc = y - mean
    var = _bdot(yc * yc, hb) * (1.0 / hd)
    yn = yc * lax.rsqrt(var + RWKV_LNX_EPS) * lg_ref[...] + lb_ref[...]
    y_ref[0] = ((yn + bonus_ref[...]) * gate_ref[...]).astype(y_ref.dtype)


def _rwkv_params(w, l):
    p = {k: w["rwkv_" + k][l] for k in ("mu", "w0", "w2", "a0", "a2", "g2", "kk", "ka", "rk", "lnx_g", "lnx_b")}
    if l > 0:
        p.update({k: w["rwkv_" + k][l - 1] for k in ("v0", "v1", "v2")})
    return p


def _rwkv_branch(cols, v_first, p):
    b, s, nc = cols.shape
    tb = min(RWKV_BLOCK, s)
    has_vmix = v_first is not None
    row = lambda a: a.reshape(1, -1)
    pair = lambda a: list(_split(a))
    tok = lambda w: pl.BlockSpec((1, tb, w), lambda i, j: (i, j, 0))
    wa2 = jnp.zeros((128, 2 * MIX_DIM), F32)
    wa2 = wa2.at[0:64, 0:MIX_DIM].set(p["w2"]).at[64:128, MIX_DIM:].set(p["a2"])
    ins = [cols] + ([v_first] if has_vmix else []) + [
        row(p["mu"]), row(p["w0"]), row(p["a0"])] + pair(wa2) + [p["g2"].astype(BF16)] + [
        row(p["kk"]), row(p["ka"]), row(p["rk"]), row(p["lnx_g"]), row(p["lnx_b"])]
    if has_vmix:
        ins += [row(p["v0"]), p["v1"].astype(BF16), p["v2"].astype(BF16)]
    in_specs = [tok(nc)] + ([tok(MIX_DIM)] if has_vmix else []) + [_full(a.shape) for a in ins[1 + has_vmix:]]
    y_shape = jax.ShapeDtypeStruct((b, s, MIX_DIM), F32)
    out_shape = y_shape if has_vmix else (y_shape, y_shape)
    out_specs = tok(MIX_DIM) if has_vmix else (tok(MIX_DIM), tok(MIX_DIM))
    n_ch = tb // RWKV_CHUNK
    tok_bf = pltpu.VMEM((tb, MIX_DIM), BF16)
    chan_bf = pltpu.VMEM((MIX_DIM, tb), BF16)
    tok_f32 = pltpu.VMEM((tb, MIX_DIM), F32)
    chan_f32 = pltpu.VMEM((MIX_DIM, tb), F32)
    scratch = [pltpu.VMEM((N_HEADS, HEAD_DIM, HEAD_DIM), F32), pltpu.VMEM((1, nc), F32),
               chan_bf, chan_bf, tok_bf, tok_bf, tok_bf, tok_bf, chan_bf,
               pltpu.VMEM((8, MIX_DIM), F32), tok_f32, tok_f32, chan_f32, chan_f32,
               pltpu.VMEM((N_HEADS, tb, tb), BF16), pltpu.VMEM((N_HEADS, tb, tb), BF16),
               pltpu.VMEM((N_HEADS, 2 * HEAD_DIM, tb), F32),
               pltpu.VMEM((N_HEADS, n_ch, HEAD_DIM, 2 * RWKV_CHUNK), BF16),
               pltpu.VMEM((N_HEADS, n_ch, HEAD_DIM, HEAD_DIM), F32)]
    out = pl.pallas_call(
        functools.partial(_rwkv_kernel, has_vmix),
        grid=(b, s // tb),
        in_specs=in_specs,
        out_specs=out_specs,
        out_shape=out_shape,
        scratch_shapes=scratch,
        compiler_params=_params("parallel", "arbitrary"),
        name="rwkv7_vmix" if has_vmix else "rwkv7_first",
    )(*ins)
    return (out, v_first) if has_vmix else out


DSA_COLS_PAD = 1280
_DSA_Q, _DSA_IQ, _DSA_KIK, _DSA_VW = 0, 512, 1024, 1152


def _rope_table_kernel(pos_ref, freq_ref, cos_ref, sin_ref):
    ang = pos_ref[...] * freq_ref[...]
    lane = lax.broadcasted_iota(jnp.int32, ang.shape, 1)
    cos_ref[...] = jnp.cos(ang)
    sin_ref[...] = jnp.where(lane % HEAD_DIM < HEAD_DIM // 2, -jnp.sin(ang), jnp.sin(ang))


def _rope_tables(positions):
    t = positions.size
    tm = min(ROW_TILE, t)
    inv_freq = 1.0 / (ROPE_THETA ** (jnp.arange(0, HEAD_DIM, 2, dtype=F32) / HEAD_DIM))
    freq = jnp.tile(inv_freq, 4).reshape(1, 128)
    pos = positions.reshape(t, 1).astype(F32)
    spec = pl.BlockSpec((tm, 128), lambda i: (i, 0))
    return pl.pallas_call(
        _rope_table_kernel,
        grid=(t // tm,),
        in_specs=[pl.BlockSpec((tm, 1), lambda i: (i, 0)), _full((1, 128))],
        out_specs=(spec, spec),
        out_shape=(jax.ShapeDtypeStruct((t, 128), F32),) * 2,
        compiler_params=_params("parallel"),
        name="rope_tables",
    )(pos, freq)


def _rope128(x, cos, sin):
    lane = lax.broadcasted_iota(jnp.int32, x.shape, 1)
    half = HEAD_DIM // 2
    rot = jnp.where(lane % HEAD_DIM < half, pltpu.roll(x, 128 - half, 1), pltpu.roll(x, half, 1))
    return x * cos + rot * sin


def _dsa_prep_kernel(cols_ref, cos_ref, sin_ref, qn_ref, kn_ref, q_ref, iq_ref, kik_ref, vw_ref):
    cos, sin = cos_ref[...], sin_ref[...]
    q = cols_ref[:, _DSA_Q:_DSA_Q + MIX_DIM]
    ss = _dot_x2(q * q, _head_ones(MIX_DIM, HEAD_DIM)) * (1.0 / HEAD_DIM)
    q = q * lax.rsqrt(ss + NORM_EPS) * qn_ref[...]
    kik = cols_ref[:, _DSA_KIK:_DSA_KIK + 128]
    ss = _dot_x2(kik * kik, _head_ones(128, HEAD_DIM)) * (1.0 / HEAD_DIM)
    kik_ref[...] = _rope128(kik * lax.rsqrt(ss + NORM_EPS) * kn_ref[...], cos, sin)
    for j in range(MIX_DIM // 128):
        sl = slice(j * 128, (j + 1) * 128)
        q2 = _rope128(q[:, sl], cos, sin) * (HEAD_DIM ** -0.5 * LOG2_E)
        q_ref[2 * j] = q2[:, 0:HEAD_DIM]
        q_ref[2 * j + 1] = q2[:, HEAD_DIM:2 * HEAD_DIM]
        iq_ref[:, sl] = _rope128(cols_ref[:, _DSA_IQ + j * 128:_DSA_IQ + (j + 1) * 128], cos, sin) * (HEAD_DIM ** -0.5)
    vw = cols_ref[:, _DSA_VW:_DSA_VW + 128]
    vw_ref[...] = jnp.where(lax.broadcasted_iota(jnp.int32, vw.shape, 1) == 127, 1.0, vw)


def _dsa_prep(cols, cos, sin, q_norm, k_norm, idx_k_norm):
    t = cols.shape[0]
    tm = min(ROW_TILE, t)
    qn = jnp.tile(q_norm, N_HEADS).reshape(1, MIX_DIM)
    kn = jnp.concatenate([k_norm, idx_k_norm]).reshape(1, 128)
    tok = lambda w: pl.BlockSpec((tm, w), lambda i: (i, 0))
    sds = lambda w: jax.ShapeDtypeStruct((t, w), F32)
    return pl.pallas_call(
        _dsa_prep_kernel,
        grid=(t // tm,),
        in_specs=[tok(DSA_COLS_PAD), tok(128), tok(128), _full((1, MIX_DIM)), _full((1, 128))],
        out_specs=(pl.BlockSpec((N_HEADS, tm, HEAD_DIM), lambda i: (0, i, 0)), tok(MIX_DIM), tok(128), tok(128)),
        out_shape=(jax.ShapeDtypeStruct((N_HEADS, t, HEAD_DIM), F32), sds(MIX_DIM), sds(128), sds(128)),
        compiler_params=_params("parallel"),
        name="dsa_prep",
    )(cols, cos, sin, qn, kn)


def _count(mask):
    return jnp.sum(jnp.where(mask, 1.0, 0.0), axis=1, keepdims=True)


def _dsa_block(top_k, n_keys, q_ref, iq_ref, vwq_ref, kik_ref, vw_ref, o_ref, key_ref, s_ref):
    qb = iq_ref.shape[1]
    start = pl.program_id(1) * qb
    k = kik_ref[0, 0:n_keys, 0:HEAD_DIM].astype(BF16)
    ik = kik_ref[0, 0:n_keys, HEAD_DIM:2 * HEAD_DIM].astype(BF16)
    v_ones = vw_ref[0, 0:n_keys, :].astype(BF16)
    iw = vwq_ref[0, :, HEAD_DIM:HEAD_DIM + N_HEADS] * (N_HEADS ** -0.5)

    score = jnp.zeros((qb, n_keys), F32)
    for h in range(N_HEADS):
        lg = _dot_nt(iq_ref[0, :, h * HEAD_DIM:(h + 1) * HEAD_DIM].astype(BF16), ik)
        score = score + jnp.maximum(lg, 0.0) * iw[:, h:h + 1]
    key_pos = lax.broadcasted_iota(jnp.int32, (qb, n_keys), 1)
    q_pos = start + lax.broadcasted_iota(jnp.int32, (qb, 1), 0)
    limit = (q_pos // CHUNK + 1) * CHUNK
    adm = key_pos < limit
    score = jnp.where(adm, score, NEG_INF)
    score = jnp.where(score == 0.0, 0.0, score)

    bits = pltpu.bitcast(score, jnp.int32)
    key = bits ^ ((bits >> 31) & jnp.int32(0x7FFFFFFF))
    kf = float(top_k)
    key_ref[:, 0:n_keys] = key
    gr = qb // DSA_ROW_GROUPS
    digit_bits = 2 if n_keys <= DSA_RADIX4_MAX_KEYS else 1
    n_steps = 32 // digit_bits

    def theta_body(i, ths):
        unit = jnp.int32(1) << (32 - digit_bits - digit_bits * i)
        out = []
        for g in range(DSA_ROW_GROUPS):
            keys = key_ref[g * gr:(g + 1) * gr, 0:n_keys]
            passed = jnp.zeros((gr, 1), jnp.int32)
            for d in range(1, 2 ** digit_bits):
                cnt = _count(keys >= ths[g] + unit * d)
                passed = passed + jnp.where(cnt >= kf, 1, 0)
            out.append(ths[g] + unit * passed)
        return tuple(out)

    per_head = n_steps // N_HEADS

    def head_step(hh, ths):
        s_ref[hh, :, 0:n_keys] = _dot_nt(q_ref[hh].astype(BF16), k)
        for j in range(per_head):
            ths = theta_body(hh * per_head + j, ths)
        return ths

    lowest = jnp.full((gr, 1), -2 ** 31, jnp.int32)
    ths = lax.fori_loop(0, N_HEADS, head_step, (lowest,) * DSA_ROW_GROUPS)
    theta = jnp.concatenate(ths, axis=0)
    gt = key > theta
    eq = key == theta
    n_gt = _count(gt)
    need = kf - n_gt
    n_bits = int(np.ceil(np.log2(n_keys))) + 1

    def tie_search():
        def tie_body(i, jb):
            cand = jb + (jnp.int32(1) << (n_bits - 1 - i))
            ok = (cand <= n_keys) & (_count(eq & (key_pos < cand)) <= need)
            return jnp.where(ok, cand, jb)

        return lax.fori_loop(0, n_bits, tie_body, jnp.zeros((qb, 1), jnp.int32))

    has_tie = jnp.max(n_gt + _count(eq)) > kf
    bound = lax.cond(has_tie, tie_search, lambda: jnp.full((qb, 1), n_keys, jnp.int32))
    bias = jnp.where((gt | (eq & (key_pos < bound))) & adm, 0.0, NEG_INF)

    tops = []
    for h in range(N_HEADS):
        s = s_ref[h, :, 0:n_keys] + bias
        s_ref[h, :, 0:n_keys] = s
        tops.append(jnp.max(s, axis=1, keepdims=True))
    for h in range(N_HEADS):
        p = jnp.exp2(s_ref[h, :, 0:n_keys] - tops[h])
        o = _dot(p.astype(BF16), v_ones)
        o_ref[0, :, h * HEAD_DIM:(h + 1) * HEAD_DIM] = o[:, 0:HEAD_DIM] / o[:, 127:128]


def _dsa_attn_kernel(top_k, n_classes, q_ref, iq_ref, vwq_ref, kik_ref, vw_ref, o_ref, key_ref, s_ref):
    qb = iq_ref.shape[1]
    per_class = (kik_ref.shape[1] // qb) // n_classes
    cls = pl.program_id(1) // per_class
    for n in range(n_classes):
        pl.when(cls == n)(functools.partial(
            _dsa_block, top_k, (n + 1) * per_class * qb, q_ref, iq_ref, vwq_ref, kik_ref, vw_ref, o_ref, key_ref, s_ref))


def _dsa_attention(q, iq, kik, vw, b, s):
    top_k = min(DSA_TOPK_MAX, s // 4)
    qb = min(Q_BLOCK, s)
    n_classes = min(DSA_KEY_CLASSES, s // qb)
    r3 = lambda a: a.reshape(b, s, a.shape[-1])
    blk = lambda w: pl.BlockSpec((1, qb, w), lambda i, j: (i, j, 0))
    seq = lambda w: pl.BlockSpec((1, s, w), lambda i, j: (i, 0, 0))
    q_heads = pl.BlockSpec((N_HEADS, qb, HEAD_DIM), lambda i, j: (0, i * (s // qb) + j, 0))
    return pl.pallas_call(
        functools.partial(_dsa_attn_kernel, top_k, n_classes),
        grid=(b, s // qb),
        in_specs=[q_heads, blk(MIX_DIM), blk(128), seq(128), seq(128)],
        out_specs=blk(MIX_DIM),
        out_shape=jax.ShapeDtypeStruct((b, s, MIX_DIM), F32),
        scratch_shapes=[pltpu.VMEM((qb, s), jnp.int32), pltpu.VMEM((N_HEADS, qb, s), F32)],
        compiler_params=_params("parallel", "arbitrary"),
        name="dsa_attention",
    )(q, r3(iq), r3(vw), r3(kik), r3(vw))


def _dsa_weight(w_dsa):
    d = w_dsa.shape[0]
    q, k, v, iq, ik, iw = jnp.split(w_dsa, np.cumsum([512, 64, 64, 512, 64, 8])[:-1].tolist(), axis=1)
    pad = jnp.zeros((d, DSA_COLS_PAD - 1224), w_dsa.dtype)
    return jnp.concatenate([q, iq, k, ik, v, iw, pad], axis=1)


def _merge_kernel(x_ref, g_ref, ya_ref, yb_ref, wa_ref, wb_ref, wo_ref, o_ref):
    d = x_ref.shape[1]
    ga = _sigmoid(g_ref[:, 0:d])
    gb = _sigmoid(g_ref[:, d:2 * d])
    merged = ga * _bdot(ya_ref[...], wa_ref[...]) + gb * _bdot(yb_ref[...], wb_ref[...])
    o_ref[...] = x_ref[...] + _bdot(merged, wo_ref[...])


def _merge(x, gates, ya, yb, wa, wb, wo):
    t, d = x.shape
    tm = min(ROW_TILE, t)
    tok = lambda w: pl.BlockSpec((tm, w), lambda i: (i, 0))
    return pl.pallas_call(
        _merge_kernel,
        grid=(t // tm,),
        in_specs=[tok(d), tok(2 * d), tok(MIX_DIM), tok(MIX_DIM), _full(wa.shape), _full(wb.shape), _full(wo.shape)],
        out_specs=tok(d),
        out_shape=jax.ShapeDtypeStruct((t, d), F32),
        compiler_params=_params("parallel"),
        name="merge_out",
    )(x, gates, ya, yb, wa, wb, wo)


def _mem_kv_kernel(mem_ref, g_ref, wkv_ref, kn_ref, k_ref, v_ref):
    dm = k_ref.shape[2]
    kv = _dot(_rms(mem_ref[0], g_ref[...]).astype(BF16), wkv_ref[...])
    k = kv[:, 0:dm]
    ss = _dot_x2(k * k, _head_ones(dm, MEM_HEAD_DIM)) * (1.0 / MEM_HEAD_DIM)
    k_ref[0] = k * lax.rsqrt(ss + NORM_EPS) * kn_ref[...]
    v_ref[0] = kv[:, dm:2 * dm]


def _mem_kv(mem, g, wkv, k_norm):
    b, m, d = mem.shape
    dm = MEM_HEADS * MEM_HEAD_DIM
    kn = jnp.tile(k_norm, MEM_HEADS).reshape(1, dm)
    spec = pl.BlockSpec((1, m, dm), lambda i: (i, 0, 0))
    return pl.pallas_call(
        _mem_kv_kernel,
        grid=(b,),
        in_specs=[pl.BlockSpec((1, m, d), lambda i: (i, 0, 0)), _full((1, d)), _full(wkv.shape), _full((1, dm))],
        out_specs=(spec, spec),
        out_shape=(jax.ShapeDtypeStruct((b, m, dm), F32),) * 2,
        compiler_params=_params("parallel"),
        name="mem_kv",
    )(mem, g.reshape(1, d), wkv, kn)


def _mem_attn_kernel(x_ref, g_ref, wq_ref, qn_ref, k_ref, v_ref, wo_ref, o_ref):
    x = x_ref[0]
    dm = wq_ref.shape[1]
    q = _dot(_rms(x, g_ref[...]).astype(BF16), wq_ref[...])
    ss = _dot_x2(q * q, _head_ones(dm, MEM_HEAD_DIM)) * (1.0 / MEM_HEAD_DIM)
    q = q * lax.rsqrt(ss + NORM_EPS) * qn_ref[...] * (MEM_HEAD_DIM ** -0.5)
    outs = []
    for h in range(MEM_HEADS):
        sl = slice(h * MEM_HEAD_DIM, (h + 1) * MEM_HEAD_DIM)
        s = _dot_nt(q[:, sl].astype(BF16), k_ref[0, :, sl].astype(BF16))
        p = jnp.exp(s - jnp.max(s, axis=1, keepdims=True))
        p = p / jnp.sum(p, axis=1, keepdims=True)
        outs.append(_dot(p.astype(BF16), v_ref[0, :, sl].astype(BF16)))
    o = jnp.concatenate(outs, axis=1)
    o_ref[0] = x + _bdot(o, wo_ref[...])


def _mem_attention(x, g, wq, q_norm, k, v, wo):
    b, s, d = x.shape
    m, dm = k.shape[1], k.shape[2]
    tm = min(ROW_TILE, s)
    qn = jnp.tile(q_norm, MEM_HEADS).reshape(1, dm)
    tok = pl.BlockSpec((1, tm, d), lambda i, j: (i, j, 0))
    kv = pl.BlockSpec((1, m, dm), lambda i, j: (i, 0, 0))
    return pl.pallas_call(
        _mem_attn_kernel,
        grid=(b, s // tm),
        in_specs=[tok, _full((1, d)), _full(wq.shape), _full((1, dm)), kv, kv, _full(wo.shape)],
        out_specs=tok,
        out_shape=jax.ShapeDtypeStruct((b, s, d), F32),
        compiler_params=_params("parallel", "parallel"),
        name="mem_attention",
    )(x, g.reshape(1, d), wq, qn, k, v, wo)


def _silu(x):
    return x * _sigmoid(x)


def _ffn_kernel(x_ref, g_ref, wg_ref, wu_ref, wd_ref, o_ref, h_ref):
    j = pl.program_id(1)

    @pl.when(j == 0)
    def _():
        x = x_ref[...]
        h_ref[...] = _rms(x, g_ref[...]).astype(BF16)
        o_ref[...] = x

    h = h_ref[...]
    act = _silu(_dot(h, wg_ref[...])) * _dot(h, wu_ref[...])
    o_ref[...] += _bdot(act, wd_ref[...])


def _ffn(x, g, wg, wu, wd, n_f=2):
    t, d = x.shape
    f = wg.shape[1]
    tm = min(FFN_ROW_TILE, t)
    tf = f // n_f
    tok = pl.BlockSpec((tm, d), lambda i, j: (i, 0))
    return pl.pallas_call(
        _ffn_kernel,
        grid=(t // tm, n_f),
        in_specs=[tok, _full((1, d)), pl.BlockSpec((d, tf), lambda i, j: (0, j)),
                  pl.BlockSpec((d, tf), lambda i, j: (0, j)), pl.BlockSpec((tf, d), lambda i, j: (j, 0))],
        out_specs=tok,
        out_shape=jax.ShapeDtypeStruct((t, d), F32),
        scratch_shapes=[pltpu.VMEM((tm, d), BF16)],
        compiler_params=_params("parallel", "arbitrary"),
        name="ffn_swiglu",
    )(x, g.reshape(1, d), wg, wu, wd)


def _moe_kernel(x_ref, g_ref, rh_ref, rl_ref, b_ref, wg_ref, wu_ref, wd_ref, o_ref, h_ref, gate_ref, rank_ref,
                gate_t_ref, rank_t_ref):
    e = pl.program_id(1)
    tm = x_ref.shape[0]

    @pl.when(e == 0)
    def _():
        x = x_ref[...]
        hn = _rms(x, g_ref[...])
        h_ref[...] = hn.astype(BF16)
        o_ref[...] = x
        logits = _dot_x3(hn, rh_ref[...], rl_ref[...]) + b_ref[...]
        lane = lax.broadcasted_iota(jnp.int32, logits.shape, 1)
        logits = jnp.where(lane < N_EXPERTS, logits, -jnp.inf)
        m1 = jnp.max(logits, axis=1, keepdims=True)
        i1 = jnp.min(jnp.where(logits == m1, lane, 128), axis=1, keepdims=True)
        rest = jnp.where(lane == i1, -jnp.inf, logits)
        m2 = jnp.max(rest, axis=1, keepdims=True)
        i2 = jnp.min(jnp.where(rest == m2, lane, 128), axis=1, keepdims=True)
        e2 = jnp.exp(m2 - m1)
        w1 = 1.0 / (1.0 + e2)
        w2 = e2 / (1.0 + e2)
        gates = jnp.where(lane == i1, w1, 0.0) + jnp.where(lane == i2, w2, 0.0)
        gate_ref[...] = gates
        member = jnp.where(gates != 0.0, 1.0, 0.0).astype(BF16)
        earlier = jnp.where(lax.broadcasted_iota(jnp.int32, (tm, tm), 1)
                            < lax.broadcasted_iota(jnp.int32, (tm, tm), 0), 1.0, 0.0).astype(BF16)
        rank = _dot(earlier, member)
        rank_ref[...] = rank
        gate_t_ref[...] = gates.T
        rank_t_ref[...] = rank.T

    lane = lax.broadcasted_iota(jnp.int32, gate_ref.shape, 1)
    gate_col = jnp.sum(jnp.where(lane == e, gate_ref[...], 0.0), axis=1, keepdims=True)
    rank_col = jnp.sum(jnp.where(lane == e, rank_ref[...], 0.0), axis=1, keepdims=True)
    gate_row = gate_t_ref[pl.ds(e, 1), :]
    rank_row = rank_t_ref[pl.ds(e, 1), :]
    n_tok = jnp.sum(jnp.where(gate_col != 0.0, 1.0, 0.0)).astype(jnp.int32)
    sub = min(MOE_SUB, tm)

    def sub_tile(s_idx, carry):
        base = (s_idx * sub).astype(F32)
        slot_col = base + lax.broadcasted_iota(jnp.int32, (sub, 1), 0).astype(F32)
        slot_row = base + lax.broadcasted_iota(jnp.int32, (1, sub), 1).astype(F32)
        pick = (gate_row != 0.0) & (rank_row == slot_col)
        place = (gate_col != 0.0) & (rank_col == slot_row)
        hs = _dot(jnp.where(pick, 1.0, 0.0).astype(BF16), h_ref[...]).astype(BF16)
        act = _silu(_dot(hs, wg_ref[0])) * _dot(hs, wu_ref[0])
        out = _bdot(act, wd_ref[0]) * jnp.sum(jnp.where(pick, gate_row, 0.0), axis=1, keepdims=True)
        o_ref[...] += _dot(jnp.where(place, 1.0, 0.0).astype(BF16), out.astype(BF16))
        return carry

    lax.fori_loop(0, (n_tok + sub - 1) // sub, sub_tile, 0)


def _moe(x, g, router, bias, wg, wu, wd):
    t, d = x.shape
    n_e, _, f = wg.shape
    tm = min(FFN_ROW_TILE, t)
    r_pad = jnp.zeros((d, 128), F32).at[:, 0:n_e].set(router)
    b_pad = jnp.zeros((1, 128), F32).at[0, 0:n_e].set(bias)
    tok = pl.BlockSpec((tm, d), lambda i, j: (i, 0))
    return pl.pallas_call(
        _moe_kernel,
        grid=(t // tm, n_e),
        in_specs=[tok, _full((1, d)), _full((d, 128)), _full((d, 128)), _full((1, 128)),
                  pl.BlockSpec((1, d, f), lambda i, j: (j, 0, 0)), pl.BlockSpec((1, d, f), lambda i, j: (j, 0, 0)),
                  pl.BlockSpec((1, f, d), lambda i, j: (j, 0, 0))],
        out_specs=tok,
        out_shape=jax.ShapeDtypeStruct((t, d), F32),
        scratch_shapes=[pltpu.VMEM((tm, d), BF16), pltpu.VMEM((tm, 128), F32), pltpu.VMEM((tm, 128), F32),
                        pltpu.VMEM((128, tm), F32), pltpu.VMEM((128, tm), F32)],
        compiler_params=_params("parallel", "arbitrary"),
        name="moe_swiglu",
    )(x, g.reshape(1, d), *_split(r_pad), b_pad, wg, wu, wd)


def kernel(x, mem, positions, norm_mix, w_in, rwkv_mu, rwkv_w0, rwkv_w2, rwkv_a0, rwkv_a2, rwkv_g2, rwkv_v0, rwkv_v1, rwkv_v2, rwkv_kk, rwkv_ka, rwkv_rk, rwkv_lnx_g, rwkv_lnx_b, dsa_q_norm, dsa_k_norm, idx_k_norm, w_branch_a, w_branch_b, w_out, norm_mem, mem_tok_norm, mem_wq, mem_wkv, mem_q_norm, mem_k_norm, mem_wo, norm_ffn, ffn_wg, ffn_wu, ffn_wd, moe_router, moe_bias, moe_wg, moe_wu, moe_wd):
    w = dict(rwkv_mu=rwkv_mu, rwkv_w0=rwkv_w0, rwkv_w2=rwkv_w2, rwkv_a0=rwkv_a0, rwkv_a2=rwkv_a2,
             rwkv_g2=rwkv_g2, rwkv_v0=rwkv_v0, rwkv_v1=rwkv_v1, rwkv_v2=rwkv_v2, rwkv_kk=rwkv_kk,
             rwkv_ka=rwkv_ka, rwkv_rk=rwkv_rk, rwkv_lnx_g=rwkv_lnx_g, rwkv_lnx_b=rwkv_lnx_b)
    b, s, d = x.shape
    t = b * s
    depth = w_in.shape[0]
    n_a = w_in.shape[2] - 1224 - 2 * d
    bf = lambda a: a.astype(BF16)
    cos, sin = _rope_tables(positions)
    x = x.reshape(t, d)
    v_first = None
    for l in range(depth):
        w_a = bf(w_in[l, :, 0:n_a])
        w_b = bf(_dsa_weight(w_in[l, :, n_a:n_a + 1224]))
        w_g = bf(w_in[l, :, n_a + 1224:])
        cols_a, cols_b, gates = _norm_matmul(x, norm_mix[l], (w_a, w_b, w_g))
        y_a, v_first = _rwkv_branch(cols_a.reshape(b, s, n_a), v_first, _rwkv_params(w, l))
        q, iq, kik, vw = _dsa_prep(cols_b, cos, sin, dsa_q_norm[l], dsa_k_norm[l], idx_k_norm[l])
        y_b = _dsa_attention(q, iq, kik, vw, b, s)
        x = _merge(x, gates, y_a.reshape(t, MIX_DIM), y_b.reshape(t, MIX_DIM),
                   bf(w_branch_a[l]), bf(w_branch_b[l]), bf(w_out[l]))
        mk, mv = _mem_kv(mem, mem_tok_norm[l], bf(mem_wkv[l]), mem_k_norm[l])
        x = _mem_attention(x.reshape(b, s, d), norm_mem[l], bf(mem_wq[l]), mem_q_norm[l], mk, mv,
                           bf(mem_wo[l])).reshape(t, d)
        j = l // 2
        if l % 2 == 0:
            x = _ffn(x, norm_ffn[l], bf(ffn_wg[j]), bf(ffn_wu[j]), bf(ffn_wd[j]))
        else:
            x = _moe(x, norm_ffn[l], moe_router[j], moe_bias[j], bf(moe_wg[j]), bf(moe_wu[j]), bf(moe_wd[j]))
    return x.reshape(b, s, d)
```

```python
import functools

import jax
import jax.numpy as jnp
import numpy as np
from jax import lax
from jax.experimental import pallas as pl
from jax.experimental.pallas import tpu as pltpu

F32 = jnp.float32
BF16 = jnp.bfloat16
HIGHEST = lax.Precision.HIGHEST

NORM_EPS = 1e-6
NEG_INF = -1e30
ROPE_THETA = 10000.0
LOG2_E = float(np.log2(np.e))

CHUNK = 64
Q_BLOCK = 128
DSA_TOPK_MAX = 256
DSA_KEY_CLASSES = 8
DSA_ROW_GROUPS = 4
DSA_RADIX4_MAX_KEYS = 768
HEAD_DIM = 64
N_HEADS = 8
MIX_DIM = N_HEADS * HEAD_DIM
RWKV_LNX_EPS = 1e-5 * HEAD_DIM
MEM_HEADS = 4
MEM_HEAD_DIM = 128
N_EXPERTS = 8

RWKV_CHUNK = 64
RWKV_BLOCK = 256
ROW_TILE = 512
FFN_ROW_TILE = 1024
MOE_SUB = 288
VMEM_LIMIT = 56 * 1024 * 1024


def _params(*sem):
    return pltpu.CompilerParams(dimension_semantics=sem, vmem_limit_bytes=VMEM_LIMIT)


def _full(shape):
    nd = len(shape)
    return pl.BlockSpec(shape, lambda *_: (0,) * nd)


def _dot(a, b, precision=None):
    return jnp.dot(a, b, preferred_element_type=F32, precision=precision)


def _dot_nt(a, b, precision=None):
    return lax.dot_general(a, b, (((1,), (1,)), ((), ())), preferred_element_type=F32,
                           precision=precision)


def _dot_tn(a, b, precision=None):
    return lax.dot_general(a, b, (((0,), (0,)), ((), ())), preferred_element_type=F32,
                           precision=precision)


def _bdot(a, b):
    return _dot(a.astype(BF16), b.astype(BF16))


def _split(x):
    hi = x.astype(BF16)
    return hi, (x - hi.astype(F32)).astype(BF16)


def _dot_x2(x, m):
    hi, lo = _split(x)
    return _dot(hi, m) + _dot(lo, m)


def _dot_x3(x, w_hi, w_lo):
    hi, lo = _split(x)
    return _dot(hi, w_hi) + (_dot(lo, w_hi) + _dot(hi, w_lo))


def _rms(x, g):
    return x * lax.rsqrt(jnp.mean(x * x, axis=-1, keepdims=True) + NORM_EPS) * g


def _sigmoid(x):
    return 1.0 / (1.0 + jnp.exp(-x))


def _head_ones(width, head):
    r = lax.broadcasted_iota(jnp.int32, (width, width), 0) // head
    c = lax.broadcasted_iota(jnp.int32, (width, width), 1) // head
    return (r == c).astype(BF16)


def _norm_matmul_kernel(x_ref, g_ref, *refs):
    n = len(refs) // 2
    h = _rms(x_ref[...], g_ref[...]).astype(BF16)
    for w_ref, o_ref in zip(refs[:n], refs[n:]):
        o_ref[...] = _dot(h, w_ref[...])


def _norm_matmul(x, g, ws, tm=ROW_TILE // 2):
    t, d = x.shape
    tm = min(tm, t)
    return pl.pallas_call(
        _norm_matmul_kernel,
        grid=(t // tm,),
        in_specs=[pl.BlockSpec((tm, d), lambda i: (i, 0)), _full((1, d))] + [_full(w.shape) for w in ws],
        out_specs=tuple(pl.BlockSpec((tm, w.shape[1]), lambda i: (i, 0)) for w in ws),
        out_shape=tuple(jax.ShapeDtypeStruct((t, w.shape[1]), F32) for w in ws),
        compiler_params=_params("parallel"),
        name="norm_matmul",
    )(x, g.reshape(1, d), *ws)


def _rwkv_kernel(has_vmix, *refs):
    if has_vmix:
        (cols_ref, vf_ref, mu_ref, w0_ref, a0_ref, wah_ref, wal_ref, g2_ref, kk_ref, ka_ref,
         rk_ref, lg_ref, lb_ref, v0_ref, v1_ref, v2_ref,
         y_ref, st_ref, prev_ref, at_ref, rt_ref, bt_ref, kt_ref, bh_ref, kh_ref, vv_ref,
         gdec_ref, bonus_ref, gate_ref, yy_ref, y0_ref, p_ref, rb_ref, z_ref, rhs_ref, dl_ref) = refs
        vout_ref = None
    else:
        (cols_ref, mu_ref, w0_ref, a0_ref, wah_ref, wal_ref, g2_ref, kk_ref, ka_ref,
         rk_ref, lg_ref, lb_ref,
         y_ref, vout_ref, st_ref, prev_ref, at_ref, rt_ref, bt_ref, kt_ref, bh_ref, kh_ref, vv_ref,
         gdec_ref, bonus_ref, gate_ref, yy_ref, y0_ref, p_ref, rb_ref, z_ref, rhs_ref, dl_ref) = refs

    tb = cols_ref.shape[1]
    c = RWKV_CHUNK
    n_chunks = tb // c
    hd = HEAD_DIM

    @pl.when(pl.program_id(1) == 0)
    def _():
        st_ref[...] = jnp.zeros_like(st_ref)
        prev_ref[...] = jnp.zeros_like(prev_ref)

    cols = cols_ref[0]
    row = lax.broadcasted_iota(jnp.int32, cols.shape, 0)
    shifted = jnp.where(row == 0, prev_ref[...], pltpu.roll(cols, 1, 0))
    prev_ref[...] = cols[tb - 1:tb, :]
    mixed = cols + (shifted - cols) * mu_ref[...]

    r = mixed[:, 0:MIX_DIM]
    k = mixed[:, MIX_DIM:2 * MIX_DIM]
    v = mixed[:, 2 * MIX_DIM:3 * MIX_DIM]
    wa = mixed[:, 3 * MIX_DIM:3 * MIX_DIM + 128]
    gd = mixed[:, 3 * MIX_DIM + 128:3 * MIX_DIM + 256]
    lane = lax.broadcasted_iota(jnp.int32, wa.shape, 1)
    wa = jnp.where(lane < 64, jnp.tanh(wa), wa)
    wa_out = _dot_x3(wa, wah_ref[...], wal_ref[...])
    u_dec = w0_ref[...] + wa_out[:, 0:MIX_DIM]
    a = _sigmoid(a0_ref[...] + wa_out[:, MIX_DIM:2 * MIX_DIM])
    gate_ref[...] = _bdot(_sigmoid(gd), g2_ref[...])
    if has_vmix:
        vm = _bdot(_bdot(v, v1_ref[...]), v2_ref[...])
        v = v + (vf_ref[0] - v) * _sigmoid(v0_ref[...] + vm)
    else:
        vout_ref[0] = v

    hb = _head_ones(MIX_DIM, hd)
    kk = k * kk_ref[...]
    kk = kk / jnp.maximum(jnp.sqrt(_dot_x2(kk * kk, hb)), 1e-12)
    k = k * (1.0 + (a - 1.0) * ka_ref[...])
    bonus_ref[...] = _bdot(r * k * rk_ref[...], hb) * v
    ll = -_sigmoid(u_dec) * float(np.exp(-0.5))

    ri = lax.broadcasted_iota(jnp.int32, (tb, tb), 0)
    ci = lax.broadcasted_iota(jnp.int32, (tb, tb), 1)
    same = (ri // c) == (ci // c)
    strict_u = (same & (ri < ci)).astype(F32)
    incl_u = (same & (ri <= ci)).astype(F32)
    ll_hi, ll_lo = _split(ll)
    tri = (same & (ri >= ci)).astype(BF16)
    cum = _dot(tri, ll_hi) + _dot(tri, ll_lo)
    tot_rows = [cum[(ch + 1) * c - 1:(ch + 1) * c, :] for ch in range(n_chunks)]
    tot = jnp.concatenate([jnp.broadcast_to(t_row, (c, MIX_DIM)) for t_row in tot_rows], axis=0)
    for ch in range(n_chunks):
        gdec_ref[ch:ch + 1, :] = jnp.exp(tot_rows[ch])
    g_inv = jnp.exp(-cum)
    g_tail = jnp.exp(tot - cum)
    beta = kk * a
    at_ref[...] = (-kk * jnp.exp(cum - ll)).T.astype(BF16)
    rt_ref[...] = (r * jnp.exp(cum)).T.astype(BF16)
    vv_ref[...] = v.T.astype(BF16)
    bt_ref[...] = (beta * g_inv).astype(BF16)
    kt_ref[...] = (k * g_inv).astype(BF16)
    bh_ref[...] = (beta * g_tail).astype(BF16)
    kh_ref[...] = (k * g_tail).astype(BF16)

    n_levels = int(np.log2(c))
    heads = [slice(h * hd, (h + 1) * hd) for h in range(N_HEADS)]

    for h, cs in enumerate(heads):
        b_t = bt_ref[:, cs]
        k_t = kt_ref[:, cs]
        a_tr = at_ref[cs, :]
        r_tr = rt_ref[cs, :]
        v_tr = vv_ref[cs, :]
        p_ref[h] = (_dot(b_t, a_tr) * strict_u).astype(BF16)
        m_tr = (_dot(k_t, a_tr) * strict_u).astype(BF16)
        rb_ref[h] = (_dot(b_t, r_tr) * incl_u).astype(BF16)
        rk_tr = (_dot(k_t, r_tr) * incl_u).astype(BF16)
        y0_ref[cs, :] = _dot(v_tr, rk_tr)
        z_ref[h] = jnp.concatenate([a_tr.astype(F32), _dot(v_tr, m_tr)], axis=0)

    for lev in range(n_levels):
        for h in range(N_HEADS):
            p = p_ref[h]
            z = z_ref[h]
            z_ref[h] = z + _dot(z.astype(BF16), p)
            if lev < n_levels - 1:
                p_ref[h] = _dot(p, p).astype(BF16)

    for h, cs in enumerate(heads):
        z_b = z_ref[h].astype(BF16)
        w = _dot(z_b, rb_ref[h])
        r_bar = (rt_ref[cs, :].astype(F32) + w[0:hd]).astype(BF16)
        y0_ref[cs, :] = y0_ref[cs, :] + w[hd:2 * hd]
        for ch in range(n_chunks):
            cols = slice(ch * c, (ch + 1) * c)
            pd = _dot(z_b[:, cols], bh_ref[cols, cs])
            d2 = _dot(vv_ref[cs, cols], kh_ref[cols, cs])
            rhs_ref[h, ch] = jnp.concatenate([pd[0:hd].astype(BF16), r_bar[:, cols]], axis=1)
            dl_ref[h, ch] = pd[hd:2 * hd] + d2

    for ch in range(n_chunks):
        cols = slice(ch * c, (ch + 1) * c)
        for h, cs in enumerate(heads):
            st = st_ref[h]
            res = _dot(st.astype(BF16), rhs_ref[h, ch])
            yy_ref[cs, cols] = res[:, hd:hd + c] + y0_ref[cs, cols]
            st_ref[h] = st * gdec_ref[ch:ch + 1, cs] + res[:, 0:hd] + dl_ref[h, ch]

    y = yy_ref[...].T
    mean = _bdot(y, hb) * (1.0 / hd)
    yc = y - mean
    var = _bdot(yc * yc, hb) * (1.0 / hd)
    yn = yc * lax.rsqrt(var + RWKV_LNX_EPS) * lg_ref[...] + lb_ref[...]
    y_ref[0] = ((yn + bonus_ref[...]) * gate_ref[...]).astype(y_ref.dtype)


def _rwkv_params(w, l):
    p = {k: w["rwkv_" + k][l] for k in ("mu", "w0", "w2", "a0", "a2", "g2", "kk", "ka", "rk", "lnx_g", "lnx_b")}
    if l > 0:
        p.update({k: w["rwkv_" + k][l - 1] for k in ("v0", "v1", "v2")})
    return p


def _rwkv_branch(cols, v_first, p):
    b, s, nc = cols.shape
    tb = min(RWKV_BLOCK, s)
    has_vmix = v_first is not None
    row = lambda a: a.reshape(1, -1)
    pair = lambda a: list(_split(a))
    tok = lambda w: pl.BlockSpec((1, tb, w), lambda i, j: (i, j, 0))
    wa2 = jnp.zeros((128, 2 * MIX_DIM), F32)
    wa2 = wa2.at[0:64, 0:MIX_DIM].set(p["w2"]).at[64:128, MIX_DIM:].set(p["a2"])
    ins = [cols] + ([v_first] if has_vmix else []) + [
        row(p["mu"]), row(p["w0"]), row(p["a0"])] + pair(wa2) + [p["g2"].astype(BF16)] + [
        row(p["kk"]), row(p["ka"]), row(p["rk"]), row(p["lnx_g"]), row(p["lnx_b"])]
    if has_vmix:
        ins += [row(p["v0"]), p["v1"].astype(BF16), p["v2"].astype(BF16)]
    in_specs = [tok(nc)] + ([tok(MIX_DIM)] if has_vmix else []) + [_full(a.shape) for a in ins[1 + has_vmix:]]
    y_shape = jax.ShapeDtypeStruct((b, s, MIX_DIM), F32)
    out_shape = y_shape if has_vmix else (y_shape, y_shape)
    out_specs = tok(MIX_DIM) if has_vmix else (tok(MIX_DIM), tok(MIX_DIM))
    n_ch = tb // RWKV_CHUNK
    tok_bf = pltpu.VMEM((tb, MIX_DIM), BF16)
    chan_bf = pltpu.VMEM((MIX_DIM, tb), BF16)
    tok_f32 = pltpu.VMEM((tb, MIX_DIM), F32)
    chan_f32 = pltpu.VMEM((MIX_DIM, tb), F32)
    scratch = [pltpu.VMEM((N_HEADS, HEAD_DIM, HEAD_DIM), F32), pltpu.VMEM((1, nc), F32),
               chan_bf, chan_bf, tok_bf, tok_bf, tok_bf, tok_bf, chan_bf,
               pltpu.VMEM((8, MIX_DIM), F32), tok_f32, tok_f32, chan_f32, chan_f32,
               pltpu.VMEM((N_HEADS, tb, tb), BF16), pltpu.VMEM((N_HEADS, tb, tb), BF16),
               pltpu.VMEM((N_HEADS, 2 * HEAD_DIM, tb), F32),
               pltpu.VMEM((N_HEADS, n_ch, HEAD_DIM, 2 * RWKV_CHUNK), BF16),
               pltpu.VMEM((N_HEADS, n_ch, HEAD_DIM, HEAD_DIM), F32)]
    out = pl.pallas_call(
        functools.partial(_rwkv_kernel, has_vmix),
        grid=(b, s // tb),
        in_specs=in_specs,
        out_specs=out_specs,
        out_shape=out_shape,
        scratch_shapes=scratch,
        compiler_params=_params("parallel", "arbitrary"),
        name="rwkv7_vmix" if has_vmix else "rwkv7_first",
    )(*ins)
    return (out, v_first) if has_vmix else out


DSA_COLS_PAD = 1280
_DSA_Q, _DSA_IQ, _DSA_KIK, _DSA_VW = 0, 512, 1024, 1152


def _rope_table_kernel(pos_ref, freq_ref, cos_ref, sin_ref):
    ang = pos_ref[...] * freq_ref[...]
    lane = lax.broadcasted_iota(jnp.int32, ang.shape, 1)
    cos_ref[...] = jnp.cos(ang)
    sin_ref[...] = jnp.where(lane % HEAD_DIM < HEAD_DIM // 2, -jnp.sin(ang), jnp.sin(ang))


def _rope_tables(positions):
    t = positions.size
    tm = min(ROW_TILE, t)
    inv_freq = 1.0 / (ROPE_THETA ** (jnp.arange(0, HEAD_DIM, 2, dtype=F32) / HEAD_DIM))
    freq = jnp.tile(inv_freq, 4).reshape(1, 128)
    pos = positions.reshape(t, 1).astype(F32)
    spec = pl.BlockSpec((tm, 128), lambda i: (i, 0))
    return pl.pallas_call(
        _rope_table_kernel,
        grid=(t // tm,),
        in_specs=[pl.BlockSpec((tm, 1), lambda i: (i, 0)), _full((1, 128))],
        out_specs=(spec, spec),
        out_shape=(jax.ShapeDtypeStruct((t, 128), F32),) * 2,
        compiler_params=_params("parallel"),
        name="rope_tables",
    )(pos, freq)


def _rope128(x, cos, sin):
    lane = lax.broadcasted_iota(jnp.int32, x.shape, 1)
    half = HEAD_DIM // 2
    rot = jnp.where(lane % HEAD_DIM < half, pltpu.roll(x, 128 - half, 1), pltpu.roll(x, half, 1))
    return x * cos + rot * sin


def _dsa_prep_kernel(cols_ref, cos_ref, sin_ref, qn_ref, kn_ref, q_ref, iq_ref, kik_ref, vw_ref):
    cos, sin = cos_ref[...], sin_ref[...]
    q = cols_ref[:, _DSA_Q:_DSA_Q + MIX_DIM]
    ss = _dot_x2(q * q, _head_ones(MIX_DIM, HEAD_DIM)) * (1.0 / HEAD_DIM)
    q = q * lax.rsqrt(ss + NORM_EPS) * qn_ref[...]
    kik = cols_ref[:, _DSA_KIK:_DSA_KIK + 128]
    ss = _dot_x2(kik * kik, _head_ones(128, HEAD_DIM)) * (1.0 / HEAD_DIM)
    kik_ref[...] = _rope128(kik * lax.rsqrt(ss + NORM_EPS) * kn_ref[...], cos, sin)
    for j in range(MIX_DIM // 128):
        sl = slice(j * 128, (j + 1) * 128)
        q_ref[j] = _rope128(q[:, sl], cos, sin) * (HEAD_DIM ** -0.5 * LOG2_E)
        iq_ref[:, sl] = _rope128(cols_ref[:, _DSA_IQ + j * 128:_DSA_IQ + (j + 1) * 128], cos, sin) * (HEAD_DIM ** -0.5)
    vw = cols_ref[:, _DSA_VW:_DSA_VW + 128]
    vw_ref[...] = jnp.where(lax.broadcasted_iota(jnp.int32, vw.shape, 1) == 127, 1.0, vw)


def _dsa_prep(cols, cos, sin, q_norm, k_norm, idx_k_norm):
    t = cols.shape[0]
    tm = min(ROW_TILE, t)
    qn = jnp.tile(q_norm, N_HEADS).reshape(1, MIX_DIM)
    kn = jnp.concatenate([k_norm, idx_k_norm]).reshape(1, 128)
    tok = lambda w: pl.BlockSpec((tm, w), lambda i: (i, 0))
    sds = lambda w: jax.ShapeDtypeStruct((t, w), F32)
    return pl.pallas_call(
        _dsa_prep_kernel,
        grid=(t // tm,),
        in_specs=[tok(DSA_COLS_PAD), tok(128), tok(128), _full((1, MIX_DIM)), _full((1, 128))],
        out_specs=(pl.BlockSpec((N_HEADS // 2, tm, 128), lambda i: (0, i, 0)), tok(MIX_DIM), tok(128), tok(128)),
        out_shape=(jax.ShapeDtypeStruct((N_HEADS // 2, t, 128), F32), sds(MIX_DIM), sds(128), sds(128)),
        compiler_params=_params("parallel"),
        name="dsa_prep",
    )(cols, cos, sin, qn, kn)


def _count(mask):
    return jnp.sum(jnp.where(mask, 1.0, 0.0), axis=1, keepdims=True)


def _dsa_block(top_k, n_keys, q_ref, iq_ref, vwq_ref, kik_ref, vw_ref, o_ref, key_ref, s_ref):
    qb = iq_ref.shape[1]
    start = pl.program_id(1) * qb
    k = kik_ref[0, 0:n_keys, 0:HEAD_DIM].astype(BF16)
    ik = kik_ref[0, 0:n_keys, HEAD_DIM:2 * HEAD_DIM].astype(BF16)
    v_ones = vw_ref[0, 0:n_keys, :].astype(BF16)
    iw = vwq_ref[0, :, HEAD_DIM:HEAD_DIM + N_HEADS] * (N_HEADS ** -0.5)

    score = jnp.zeros((qb, n_keys), F32)
    for h in range(N_HEADS):
        lg = _dot_nt(iq_ref[0, :, h * HEAD_DIM:(h + 1) * HEAD_DIM].astype(BF16), ik)
        score = score + jnp.maximum(lg, 0.0) * iw[:, h:h + 1]
    key_pos = lax.broadcasted_iota(jnp.int32, (qb, n_keys), 1)
    q_pos = start + lax.broadcasted_iota(jnp.int32, (qb, 1), 0)
    limit = (q_pos // CHUNK + 1) * CHUNK
    adm = key_pos < limit
    score = jnp.where(adm, score, NEG_INF)
    score = jnp.where(score == 0.0, 0.0, score)

    bits = pltpu.bitcast(score, jnp.int32)
    key = bits ^ ((bits >> 31) & jnp.int32(0x7FFFFFFF))
    kf = float(top_k)
    key_ref[:, 0:n_keys] = key
    gr = qb // DSA_ROW_GROUPS
    digit_bits = 2 if n_keys <= DSA_RADIX4_MAX_KEYS else 1
    n_steps = 32 // digit_bits

    def theta_body(i, ths):
        unit = jnp.int32(1) << (32 - digit_bits - digit_bits * i)
        out = []
        for g in range(DSA_ROW_GROUPS):
            keys = key_ref[g * gr:(g + 1) * gr, 0:n_keys]
            passed = jnp.zeros((gr, 1), jnp.int32)
            for d in range(1, 2 ** digit_bits):
                cnt = _count(keys >= ths[g] + unit * d)
                passed = passed + jnp.where(cnt >= kf, 1, 0)
            out.append(ths[g] + unit * passed)
        return tuple(out)

    n_pairs = N_HEADS // 2
    per_pair = n_steps // n_pairs

    def pair_step(hp, ths):
        q2 = q_ref[hp].astype(BF16)
        s_ref[2 * hp, :, 0:n_keys] = _dot_nt(q2[:, 0:HEAD_DIM], k)
        s_ref[2 * hp + 1, :, 0:n_keys] = _dot_nt(q2[:, HEAD_DIM:2 * HEAD_DIM], k)
        for j in range(per_pair):
            ths = theta_body(hp * per_pair + j, ths)
        return ths

    lowest = jnp.full((gr, 1), -2 ** 31, jnp.int32)
    ths = lax.fori_loop(0, n_pairs, pair_step, (lowest,) * DSA_ROW_GROUPS)
    theta = jnp.concatenate(ths, axis=0)
    gt = key > theta
    eq = key == theta
    n_gt = _count(gt)
    need = kf - n_gt
    n_bits = int(np.ceil(np.log2(n_keys))) + 1

    def tie_search():
        def tie_body(i, jb):
            cand = jb + (jnp.int32(1) << (n_bits - 1 - i))
            ok = (cand <= n_keys) & (_count(eq & (key_pos < cand)) <= need)
            return jnp.where(ok, cand, jb)

        return lax.fori_loop(0, n_bits, tie_body, jnp.zeros((qb, 1), jnp.int32))

    has_tie = jnp.max(n_gt + _count(eq)) > kf
    bound = lax.cond(has_tie, tie_search, lambda: jnp.full((qb, 1), n_keys, jnp.int32))
    bias = jnp.where((gt | (eq & (key_pos < bound))) & adm, 0.0, NEG_INF)

    tops = []
    for h in range(N_HEADS):
        s = s_ref[h, :, 0:n_keys] + bias
        s_ref[h, :, 0:n_keys] = s
        tops.append(jnp.max(s, axis=1, keepdims=True))
    for h in range(N_HEADS):
        p = jnp.exp2(s_ref[h, :, 0:n_keys] - tops[h])
        o = _dot(p.astype(BF16), v_ones)
        o_ref[0, :, h * HEAD_DIM:(h + 1) * HEAD_DIM] = o[:, 0:HEAD_DIM] / o[:, 127:128]


def _dsa_attn_kernel(top_k, n_classes, q_ref, iq_ref, vwq_ref, kik_ref, vw_ref, o_ref, key_ref, s_ref):
    qb = iq_ref.shape[1]
    per_class = (kik_ref.shape[1] // qb) // n_classes
    cls = pl.program_id(1) // per_class
    for n in range(n_classes):
        pl.when(cls == n)(functools.partial(
            _dsa_block, top_k, (n + 1) * per_class * qb, q_ref, iq_ref, vwq_ref, kik_ref, vw_ref, o_ref, key_ref, s_ref))


def _dsa_attention(q, iq, kik, vw, b, s):
    top_k = min(DSA_TOPK_MAX, s // 4)
    qb = min(Q_BLOCK, s)
    n_classes = min(DSA_KEY_CLASSES, s // qb)
    r3 = lambda a: a.reshape(b, s, a.shape[-1])
    blk = lambda w: pl.BlockSpec((1, qb, w), lambda i, j: (i, j, 0))
    seq = lambda w: pl.BlockSpec((1, s, w), lambda i, j: (i, 0, 0))
    q_heads = pl.BlockSpec((N_HEADS // 2, qb, 128), lambda i, j: (0, i * (s // qb) + j, 0))
    return pl.pallas_call(
        functools.partial(_dsa_attn_kernel, top_k, n_classes),
        grid=(b, s // qb),
        in_specs=[q_heads, blk(MIX_DIM), blk(128), seq(128), seq(128)],
        out_specs=blk(MIX_DIM),
        out_shape=jax.ShapeDtypeStruct((b, s, MIX_DIM), F32),
        scratch_shapes=[pltpu.VMEM((qb, s), jnp.int32), pltpu.VMEM((N_HEADS, qb, s), F32)],
        compiler_params=_params("parallel", "arbitrary"),
        name="dsa_attention",
    )(q, r3(iq), r3(vw), r3(kik), r3(vw))


def _dsa_weight(w_dsa):
    d = w_dsa.shape[0]
    q, k, v, iq, ik, iw = jnp.split(w_dsa, np.cumsum([512, 64, 64, 512, 64, 8])[:-1].tolist(), axis=1)
    pad = jnp.zeros((d, DSA_COLS_PAD - 1224), w_dsa.dtype)
    return jnp.concatenate([q, iq, k, ik, v, iw, pad], axis=1)


def _merge_kernel(x_ref, g_ref, ya_ref, yb_ref, wa_ref, wb_ref, wo_ref, o_ref):
    d = x_ref.shape[1]
    ga = _sigmoid(g_ref[:, 0:d])
    gb = _sigmoid(g_ref[:, d:2 * d])
    merged = ga * _bdot(ya_ref[...], wa_ref[...]) + gb * _bdot(yb_ref[...], wb_ref[...])
    o_ref[...] = x_ref[...] + _bdot(merged, wo_ref[...])


def _merge(x, gates, ya, yb, wa, wb, wo):
    t, d = x.shape
    tm = min(ROW_TILE, t)
    tok = lambda w: pl.BlockSpec((tm, w), lambda i: (i, 0))
    return pl.pallas_call(
        _merge_kernel,
        grid=(t // tm,),
        in_specs=[tok(d), tok(2 * d), tok(MIX_DIM), tok(MIX_DIM), _full(wa.shape), _full(wb.shape), _full(wo.shape)],
        out_specs=tok(d),
        out_shape=jax.ShapeDtypeStruct((t, d), F32),
        compiler_params=_params("parallel"),
        name="merge_out",
    )(x, gates, ya, yb, wa, wb, wo)


def _mem_kv_kernel(mem_ref, g_ref, wkv_ref, kn_ref, k_ref, v_ref):
    dm = k_ref.shape[2]
    kv = _dot(_rms(mem_ref[0], g_ref[...]).astype(BF16), wkv_ref[...])
    k = kv[:, 0:dm]
    ss = _dot_x2(k * k, _head_ones(dm, MEM_HEAD_DIM)) * (1.0 / MEM_HEAD_DIM)
    k_ref[0] = k * lax.rsqrt(ss + NORM_EPS) * kn_ref[...]
    v_ref[0] = kv[:, dm:2 * dm]


def _mem_kv(mem, g, wkv, k_norm):
    b, m, d = mem.shape
    dm = MEM_HEADS * MEM_HEAD_DIM
    kn = jnp.tile(k_norm, MEM_HEADS).reshape(1, dm)
    spec = pl.BlockSpec((1, m, dm), lambda i: (i, 0, 0))
    return pl.pallas_call(
        _mem_kv_kernel,
        grid=(b,),
        in_specs=[pl.BlockSpec((1, m, d), lambda i: (i, 0, 0)), _full((1, d)), _full(wkv.shape), _full((1, dm))],
        out_specs=(spec, spec),
        out_shape=(jax.ShapeDtypeStruct((b, m, dm), F32),) * 2,
        compiler_params=_params("parallel"),
        name="mem_kv",
    )(mem, g.reshape(1, d), wkv, kn)


def _mem_attn_kernel(x_ref, g_ref, wq_ref, qn_ref, k_ref, v_ref, wo_ref, o_ref):
    x = x_ref[0]
    dm = wq_ref.shape[1]
    q = _dot(_rms(x, g_ref[...]).astype(BF16), wq_ref[...])
    ss = _dot_x2(q * q, _head_ones(dm, MEM_HEAD_DIM)) * (1.0 / MEM_HEAD_DIM)
    q = q * lax.rsqrt(ss + NORM_EPS) * qn_ref[...] * (MEM_HEAD_DIM ** -0.5)
    outs = []
    for h in range(MEM_HEADS):
        sl = slice(h * MEM_HEAD_DIM, (h + 1) * MEM_HEAD_DIM)
        s = _dot_nt(q[:, sl].astype(BF16), k_ref[0, :, sl].astype(BF16))
        p = jnp.exp(s - jnp.max(s, axis=1, keepdims=True))
        p = p / jnp.sum(p, axis=1, keepdims=True)
        outs.append(_dot(p.astype(BF16), v_ref[0, :, sl].astype(BF16)))
    o = jnp.concatenate(outs, axis=1)
    o_ref[0] = x + _bdot(o, wo_ref[...])


def _mem_attention(x, g, wq, q_norm, k, v, wo):
    b, s, d = x.shape
    m, dm = k.shape[1], k.shape[2]
    tm = min(ROW_TILE, s)
    qn = jnp.tile(q_norm, MEM_HEADS).reshape(1, dm)
    tok = pl.BlockSpec((1, tm, d), lambda i, j: (i, j, 0))
    kv = pl.BlockSpec((1, m, dm), lambda i, j: (i, 0, 0))
    return pl.pallas_call(
        _mem_attn_kernel,
        grid=(b, s // tm),
        in_specs=[tok, _full((1, d)), _full(wq.shape), _full((1, dm)), kv, kv, _full(wo.shape)],
        out_specs=tok,
        out_shape=jax.ShapeDtypeStruct((b, s, d), F32),
        compiler_params=_params("parallel", "parallel"),
        name="mem_attention",
    )(x, g.reshape(1, d), wq, qn, k, v, wo)


def _silu(x):
    return x * _sigmoid(x)


def _ffn_kernel(x_ref, g_ref, wg_ref, wu_ref, wd_ref, o_ref, h_ref):
    j = pl.program_id(1)

    @pl.when(j == 0)
    def _():
        x = x_ref[...]
        h_ref[...] = _rms(x, g_ref[...]).astype(BF16)
        o_ref[...] = x

    h = h_ref[...]
    act = _silu(_dot(h, wg_ref[...])) * _dot(h, wu_ref[...])
    o_ref[...] += _bdot(act, wd_ref[...])


def _ffn(x, g, wg, wu, wd, n_f=2):
    t, d = x.shape
    f = wg.shape[1]
    tm = min(FFN_ROW_TILE, t)
    tf = f // n_f
    tok = pl.BlockSpec((tm, d), lambda i, j: (i, 0))
    return pl.pallas_call(
        _ffn_kernel,
        grid=(t // tm, n_f),
        in_specs=[tok, _full((1, d)), pl.BlockSpec((d, tf), lambda i, j: (0, j)),
                  pl.BlockSpec((d, tf), lambda i, j: (0, j)), pl.BlockSpec((tf, d), lambda i, j: (j, 0))],
        out_specs=tok,
        out_shape=jax.ShapeDtypeStruct((t, d), F32),
        scratch_shapes=[pltpu.VMEM((tm, d), BF16)],
        compiler_params=_params("parallel", "arbitrary"),
        name="ffn_swiglu",
    )(x, g.reshape(1, d), wg, wu, wd)


def _moe_kernel(x_ref, g_ref, rh_ref, rl_ref, b_ref, wg_ref, wu_ref, wd_ref, o_ref, h_ref, gate_ref, rank_ref,
                gate_t_ref, rank_t_ref):
    e = pl.program_id(1)
    tm = x_ref.shape[0]

    @pl.when(e == 0)
    def _():
        x = x_ref[...]
        hn = _rms(x, g_ref[...])
        h_ref[...] = hn.astype(BF16)
        o_ref[...] = x
        logits = _dot_x3(hn, rh_ref[...], rl_ref[...]) + b_ref[...]
        lane = lax.broadcasted_iota(jnp.int32, logits.shape, 1)
        logits = jnp.where(lane < N_EXPERTS, logits, -jnp.inf)
        m1 = jnp.max(logits, axis=1, keepdims=True)
        i1 = jnp.min(jnp.where(logits == m1, lane, 128), axis=1, keepdims=True)
        rest = jnp.where(lane == i1, -jnp.inf, logits)
        m2 = jnp.max(rest, axis=1, keepdims=True)
        i2 = jnp.min(jnp.where(rest == m2, lane, 128), axis=1, keepdims=True)
        e2 = jnp.exp(m2 - m1)
        w1 = 1.0 / (1.0 + e2)
        w2 = e2 / (1.0 + e2)
        gates = jnp.where(lane == i1, w1, 0.0) + jnp.where(lane == i2, w2, 0.0)
        gate_ref[...] = gates
        member = jnp.where(gates != 0.0, 1.0, 0.0).astype(BF16)
        earlier = jnp.where(lax.broadcasted_iota(jnp.int32, (tm, tm), 1)
                            < lax.broadcasted_iota(jnp.int32, (tm, tm), 0), 1.0, 0.0).astype(BF16)
        rank = _dot(earlier, member)
        rank_ref[...] = rank
        gate_t_ref[...] = gates.T
        rank_t_ref[...] = rank.T

    lane = lax.broadcasted_iota(jnp.int32, gate_ref.shape, 1)
    gate_col = jnp.sum(jnp.where(lane == e, gate_ref[...], 0.0), axis=1, keepdims=True)
    rank_col = jnp.sum(jnp.where(lane == e, rank_ref[...], 0.0), axis=1, keepdims=True)
    gate_row = gate_t_ref[pl.ds(e, 1), :]
    rank_row = rank_t_ref[pl.ds(e, 1), :]
    n_tok = jnp.sum(jnp.where(gate_col != 0.0, 1.0, 0.0)).astype(jnp.int32)
    sub = min(MOE_SUB, tm)

    def sub_tile(s_idx, carry):
        base = (s_idx * sub).astype(F32)
        slot_col = base + lax.broadcasted_iota(jnp.int32, (sub, 1), 0).astype(F32)
        slot_row = base + lax.broadcasted_iota(jnp.int32, (1, sub), 1).astype(F32)
        pick = (gate_row != 0.0) & (rank_row == slot_col)
        place = (gate_col != 0.0) & (rank_col == slot_row)
        hs = _dot(jnp.where(pick, 1.0, 0.0).astype(BF16), h_ref[...]).astype(BF16)
        act = _silu(_dot(hs, wg_ref[0])) * _dot(hs, wu_ref[0])
        out = _bdot(act, wd_ref[0]) * jnp.sum(jnp.where(pick, gate_row, 0.0), axis=1, keepdims=True)
        o_ref[...] += _dot(jnp.where(place, 1.0, 0.0).astype(BF16), out.astype(BF16))
        return carry

    lax.fori_loop(0, (n_tok + sub - 1) // sub, sub_tile, 0)


def _moe(x, g, router, bias, wg, wu, wd):
    t, d = x.shape
    n_e, _, f = wg.shape
    tm = min(FFN_ROW_TILE, t)
    r_pad = jnp.zeros((d, 128), F32).at[:, 0:n_e].set(router)
    b_pad = jnp.zeros((1, 128), F32).at[0, 0:n_e].set(bias)
    tok = pl.BlockSpec((tm, d), lambda i, j: (i, 0))
    return pl.pallas_call(
        _moe_kernel,
        grid=(t // tm, n_e),
        in_specs=[tok, _full((1, d)), _full((d, 128)), _full((d, 128)), _full((1, 128)),
                  pl.BlockSpec((1, d, f), lambda i, j: (j, 0, 0)), pl.BlockSpec((1, d, f), lambda i, j: (j, 0, 0)),
                  pl.BlockSpec((1, f, d), lambda i, j: (j, 0, 0))],
        out_specs=tok,
        out_shape=jax.ShapeDtypeStruct((t, d), F32),
        scratch_shapes=[pltpu.VMEM((tm, d), BF16), pltpu.VMEM((tm, 128), F32), pltpu.VMEM((tm, 128), F32),
                        pltpu.VMEM((128, tm), F32), pltpu.VMEM((128, tm), F32)],
        compiler_params=_params("parallel", "arbitrary"),
        name="moe_swiglu",
    )(x, g.reshape(1, d), *_split(r_pad), b_pad, wg, wu, wd)


def kernel(x, mem, positions, norm_mix, w_in, rwkv_mu, rwkv_w0, rwkv_w2, rwkv_a0, rwkv_a2, rwkv_g2, rwkv_v0, rwkv_v1, rwkv_v2, rwkv_kk, rwkv_ka, rwkv_rk, rwkv_lnx_g, rwkv_lnx_b, dsa_q_norm, dsa_k_norm, idx_k_norm, w_branch_a, w_branch_b, w_out, norm_mem, mem_tok_norm, mem_wq, mem_wkv, mem_q_norm, mem_k_norm, mem_wo, norm_ffn, ffn_wg, ffn_wu, ffn_wd, moe_router, moe_bias, moe_wg, moe_wu, moe_wd):
    w = dict(rwkv_mu=rwkv_mu, rwkv_w0=rwkv_w0, rwkv_w2=rwkv_w2, rwkv_a0=rwkv_a0, rwkv_a2=rwkv_a2,
             rwkv_g2=rwkv_g2, rwkv_v0=rwkv_v0, rwkv_v1=rwkv_v1, rwkv_v2=rwkv_v2, rwkv_kk=rwkv_kk,
             rwkv_ka=rwkv_ka, rwkv_rk=rwkv_rk, rwkv_lnx_g=rwkv_lnx_g, rwkv_lnx_b=rwkv_lnx_b)
    b, s, d = x.shape
    t = b * s
    depth = w_in.shape[0]
    n_a = w_in.shape[2] - 1224 - 2 * d
    bf = lambda a: a.astype(BF16)
    cos, sin = _rope_tables(positions)
    x = x.reshape(t, d)
    v_first = None
    for l in range(depth):
        w_a = bf(w_in[l, :, 0:n_a])
        w_b = bf(_dsa_weight(w_in[l, :, n_a:n_a + 1224]))
        w_g = bf(w_in[l, :, n_a + 1224:])
        cols_a, cols_b, gates = _norm_matmul(x, norm_mix[l], (w_a, w_b, w_g))
        y_a, v_first = _rwkv_branch(cols_a.reshape(b, s, n_a), v_first, _rwkv_params(w, l))
        q, iq, kik, vw = _dsa_prep(cols_b, cos, sin, dsa_q_norm[l], dsa_k_norm[l], idx_k_norm[l])
        y_b = _dsa_attention(q, iq, kik, vw, b, s)
        x = _merge(x, gates, y_a.reshape(t, MIX_DIM), y_b.reshape(t, MIX_DIM),
                   bf(w_branch_a[l]), bf(w_branch_b[l]), bf(w_out[l]))
        mk, mv = _mem_kv(mem, mem_tok_norm[l], bf(mem_wkv[l]), mem_k_norm[l])
        x = _mem_attention(x.reshape(b, s, d), norm_mem[l], bf(mem_wq[l]), mem_q_norm[l], mk, mv,
                           bf(mem_wo[l])).reshape(t, d)
        j = l // 2
        if l % 2 == 0:
            x = _ffn(x, norm_ffn[l], bf(ffn_wg[j]), bf(ffn_wu[j]), bf(ffn_wd[j]))
        else:
            x = _moe(x, norm_ffn[l], moe_router[j], moe_bias[j], bf(moe_wg[j]), bf(moe_wu[j]), bf(moe_wd[j]))
    return x.reshape(b, s, d)
```

```python
import functools

import jax
import jax.numpy as jnp
import numpy as np
from jax import lax
from jax.experimental import pallas as pl
from jax.experimental.pallas import tpu as pltpu

F32 = jnp.float32
BF16 = jnp.bfloat16
HIGHEST = lax.Precision.HIGHEST

NORM_EPS = 1e-6
NEG_INF = -1e30
ROPE_THETA = 10000.0
LOG2_E = float(np.log2(np.e))

CHUNK = 64
Q_BLOCK = 128
DSA_TOPK_MAX = 256
DSA_KEY_CLASSES = 16
DSA_ROW_GROUPS = 4
DSA_RADIX4_MAX_KEYS = 768
HEAD_DIM = 64
N_HEADS = 8
MIX_DIM = N_HEADS * HEAD_DIM
RWKV_LNX_EPS = 1e-5 * HEAD_DIM
MEM_HEADS = 4
MEM_HEAD_DIM = 128
N_EXPERTS = 8

RWKV_CHUNK = 64
RWKV_BLOCK = 256
ROW_TILE = 512
FFN_ROW_TILE = 1024
MOE_SUB = 288
VMEM_LIMIT = 56 * 1024 * 1024


def _params(*sem):
    return pltpu.CompilerParams(dimension_semantics=sem, vmem_limit_bytes=VMEM_LIMIT)


def _full(shape):
    nd = len(shape)
    return pl.BlockSpec(shape, lambda *_: (0,) * nd)


def _dot(a, b, precision=None):
    return jnp.dot(a, b, preferred_element_type=F32, precision=precision)


def _dot_nt(a, b, precision=None):
    return lax.dot_general(a, b, (((1,), (1,)), ((), ())), preferred_element_type=F32,
                           precision=precision)


def _dot_tn(a, b, precision=None):
    return lax.dot_general(a, b, (((0,), (0,)), ((), ())), preferred_element_type=F32,
                           precision=precision)


def _bdot(a, b):
    return _dot(a.astype(BF16), b.astype(BF16))


def _split(x):
    hi = x.astype(BF16)
    return hi, (x - hi.astype(F32)).astype(BF16)


def _dot_x2(x, m):
    hi, lo = _split(x)
    return _dot(hi, m) + _dot(lo, m)


def _dot_x3(x, w_hi, w_lo):
    hi, lo = _split(x)
    return _dot(hi, w_hi) + (_dot(lo, w_hi) + _dot(hi, w_lo))


def _rms(x, g):
    return x * lax.rsqrt(jnp.mean(x * x, axis=-1, keepdims=True) + NORM_EPS) * g


def _sigmoid(x):
    return 1.0 / (1.0 + jnp.exp(-x))


def _head_ones(width, head):
    r = lax.broadcasted_iota(jnp.int32, (width, width), 0) // head
    c = lax.broadcasted_iota(jnp.int32, (width, width), 1) // head
    return (r == c).astype(BF16)


def _norm_matmul_kernel(x_ref, g_ref, *refs):
    n = len(refs) // 2
    h = _rms(x_ref[...], g_ref[...]).astype(BF16)
    for w_ref, o_ref in zip(refs[:n], refs[n:]):
        o_ref[...] = _dot(h, w_ref[...])


def _norm_matmul(x, g, ws, tm=ROW_TILE):
    t, d = x.shape
    tm = min(tm, t)
    return pl.pallas_call(
        _norm_matmul_kernel,
        grid=(t // tm,),
        in_specs=[pl.BlockSpec((tm, d), lambda i: (i, 0)), _full((1, d))] + [_full(w.shape) for w in ws],
        out_specs=tuple(pl.BlockSpec((tm, w.shape[1]), lambda i: (i, 0)) for w in ws),
        out_shape=tuple(jax.ShapeDtypeStruct((t, w.shape[1]), F32) for w in ws),
        compiler_params=_params("parallel"),
        name="norm_matmul",
    )(x, g.reshape(1, d), *ws)


def _rwkv_kernel(has_vmix, *refs):
    if has_vmix:
        (cols_ref, vf_ref, mu_ref, w0_ref, a0_ref, wah_ref, wal_ref, g2_ref, kk_ref, ka_ref,
         rk_ref, lg_ref, lb_ref, v0_ref, v1_ref, v2_ref,
         y_ref, st_ref, prev_ref, at_ref, rt_ref, bt_ref, kt_ref, bh_ref, kh_ref, vv_ref,
         gdec_ref, bonus_ref, gate_ref, yy_ref, y0_ref, p_ref, rb_ref, z_ref, rhs_ref, dl_ref) = refs
        vout_ref = None
    else:
        (cols_ref, mu_ref, w0_ref, a0_ref, wah_ref, wal_ref, g2_ref, kk_ref, ka_ref,
         rk_ref, lg_ref, lb_ref,
         y_ref, vout_ref, st_ref, prev_ref, at_ref, rt_ref, bt_ref, kt_ref, bh_ref, kh_ref, vv_ref,
         gdec_ref, bonus_ref, gate_ref, yy_ref, y0_ref, p_ref, rb_ref, z_ref, rhs_ref, dl_ref) = refs

    tb = cols_ref.shape[1]
    c = RWKV_CHUNK
    n_chunks = tb // c
    hd = HEAD_DIM

    @pl.when(pl.program_id(1) == 0)
    def _():
        st_ref[...] = jnp.zeros_like(st_ref)
        prev_ref[...] = jnp.zeros_like(prev_ref)

    cols = cols_ref[0]
    row = lax.broadcasted_iota(jnp.int32, cols.shape, 0)
    shifted = jnp.where(row == 0, prev_ref[...], pltpu.roll(cols, 1, 0))
    prev_ref[...] = cols[tb - 1:tb, :]
    mixed = cols + (shifted - cols) * mu_ref[...]

    r = mixed[:, 0:MIX_DIM]
    k = mixed[:, MIX_DIM:2 * MIX_DIM]
    v = mixed[:, 2 * MIX_DIM:3 * MIX_DIM]
    wa = mixed[:, 3 * MIX_DIM:3 * MIX_DIM + 128]
    gd = mixed[:, 3 * MIX_DIM + 128:3 * MIX_DIM + 256]
    lane = lax.broadcasted_iota(jnp.int32, wa.shape, 1)
    wa = jnp.where(lane < 64, jnp.tanh(wa), wa)
    wa_out = _dot_x3(wa, wah_ref[...], wal_ref[...])
    u_dec = w0_ref[...] + wa_out[:, 0:MIX_DIM]
    a = _sigmoid(a0_ref[...] + wa_out[:, MIX_DIM:2 * MIX_DIM])
    gate_ref[...] = _bdot(_sigmoid(gd), g2_ref[...])
    if has_vmix:
        vm = _bdot(_bdot(v, v1_ref[...]), v2_ref[...])
        v = v + (vf_ref[0] - v) * _sigmoid(v0_ref[...] + vm)
    else:
        vout_ref[0] = v

    hb = _head_ones(MIX_DIM, hd)
    kk = k * kk_ref[...]
    kk = kk / jnp.maximum(jnp.sqrt(_dot_x2(kk * kk, hb)), 1e-12)
    k = k * (1.0 + (a - 1.0) * ka_ref[...])
    bonus_ref[...] = _bdot(r * k * rk_ref[...], hb) * v
    ll = -_sigmoid(u_dec) * float(np.exp(-0.5))

    ri = lax.broadcasted_iota(jnp.int32, (tb, tb), 0)
    ci = lax.broadcasted_iota(jnp.int32, (tb, tb), 1)
    same = (ri // c) == (ci // c)
    strict_u = (same & (ri < ci)).astype(F32)
    incl_u = (same & (ri <= ci)).astype(F32)
    ll_hi, ll_lo = _split(ll)
    tri = (same & (ri >= ci)).astype(BF16)
    cum = _dot(tri, ll_hi) + _dot(tri, ll_lo)
    tot_rows = [cum[(ch + 1) * c - 1:(ch + 1) * c, :] for ch in range(n_chunks)]
    tot = jnp.concatenate([jnp.broadcast_to(t_row, (c, MIX_DIM)) for t_row in tot_rows], axis=0)
    for ch in range(n_chunks):
        gdec_ref[ch:ch + 1, :] = jnp.exp(tot_rows[ch])
    g_inv = jnp.exp(-cum)
    g_tail = jnp.exp(tot - cum)
    beta = kk * a
    at_ref[...] = (-kk * jnp.exp(cum - ll)).T.astype(BF16)
    rt_ref[...] = (r * jnp.exp(cum)).T.astype(BF16)
    vv_ref[...] = v.T.astype(BF16)
    bt_ref[...] = (beta * g_inv).astype(BF16)
    kt_ref[...] = (k * g_inv).astype(BF16)
    bh_ref[...] = (beta * g_tail).astype(BF16)
    kh_ref[...] = (k * g_tail).astype(BF16)

    n_levels = int(np.log2(c))
    heads = [slice(h * hd, (h + 1) * hd) for h in range(N_HEADS)]

    for h, cs in enumerate(heads):
        b_t = bt_ref[:, cs]
        k_t = kt_ref[:, cs]
        a_tr = at_ref[cs, :]
        r_tr = rt_ref[cs, :]
        v_tr = vv_ref[cs, :]
        p_ref[h] = (_dot(b_t, a_tr) * strict_u).astype(BF16)
        m_tr = (_dot(k_t, a_tr) * strict_u).astype(BF16)
        rb_ref[h] = (_dot(b_t, r_tr) * incl_u).astype(BF16)
        rk_tr = (_dot(k_t, r_tr) * incl_u).astype(BF16)
        y0_ref[cs, :] = _dot(v_tr, rk_tr)
        z_ref[h] = jnp.concatenate([a_tr.astype(F32), _dot(v_tr, m_tr)], axis=0)

    for lev in range(n_levels):
        for h in range(N_HEADS):
            p = p_ref[h]
            z = z_ref[h]
            z_ref[h] = z + _dot(z.astype(BF16), p)
            if lev < n_levels - 1:
                p_ref[h] = _dot(p, p).astype(BF16)

    for h, cs in enumerate(heads):
        z_b = z_ref[h].astype(BF16)
        w = _dot(z_b, rb_ref[h])
        r_bar = (rt_ref[cs, :].astype(F32) + w[0:hd]).astype(BF16)
        y0_ref[cs, :] = y0_ref[cs, :] + w[hd:2 * hd]
        for ch in range(n_chunks):
            cols = slice(ch * c, (ch + 1) * c)
            pd = _dot(z_b[:, cols], bh_ref[cols, cs])
            d2 = _dot(vv_ref[cs, cols], kh_ref[cols, cs])
            rhs_ref[h, ch] = jnp.concatenate([pd[0:hd].astype(BF16), r_bar[:, cols]], axis=1)
            dl_ref[h, ch] = pd[hd:2 * hd] + d2

    for ch in range(n_chunks):
        cols = slice(ch * c, (ch + 1) * c)
        for h, cs in enumerate(heads):
            st = st_ref[h]
            res = _dot(st.astype(BF16), rhs_ref[h, ch])
            yy_ref[cs, cols] = res[:, hd:hd + c] + y0_ref[cs, cols]
            st_ref[h] = st * gdec_ref[ch:ch + 1, cs] + res[:, 0:hd] + dl_ref[h, ch]

    y = yy_ref[...].T
    mean = _bdot(y, hb) * (1.0 / hd)
    yc = y - mean
    var = _bdot(yc * yc, hb) * (1.0 / hd)
    yn = yc * lax.rsqrt(var + RWKV_LNX_EPS) * lg_ref[...] + lb_ref[...]
    y_ref[0] = ((yn + bonus_ref[...]) * gate_ref[...]).astype(y_ref.dtype)


def _rwkv_params(w, l):
    p = {k: w["rwkv_" + k][l] for k in ("mu", "w0", "w2", "a0", "a2", "g2", "kk", "ka", "rk", "lnx_g", "lnx_b")}
    if l > 0:
        p.update({k: w["rwkv_" + k][l - 1] for k in ("v0", "v1", "v2")})
    return p


def _rwkv_branch(cols, v_first, p):
    b, s, nc = cols.shape
    tb = min(RWKV_BLOCK, s)
    has_vmix = v_first is not None
    row = lambda a: a.reshape(1, -1)
    pair = lambda a: list(_split(a))
    tok = lambda w: pl.BlockSpec((1, tb, w), lambda i, j: (i, j, 0))
    wa2 = jnp.zeros((128, 2 * MIX_DIM), F32)
    wa2 = wa2.at[0:64, 0:MIX_DIM].set(p["w2"]).at[64:128, MIX_DIM:].set(p["a2"])
    ins = [cols] + ([v_first] if has_vmix else []) + [
        row(p["mu"]), row(p["w0"]), row(p["a0"])] + pair(wa2) + [p["g2"].astype(BF16)] + [
        row(p["kk"]), row(p["ka"]), row(p["rk"]), row(p["lnx_g"]), row(p["lnx_b"])]
    if has_vmix:
        ins += [row(p["v0"]), p["v1"].astype(BF16), p["v2"].astype(BF16)]
    in_specs = [tok(nc)] + ([tok(MIX_DIM)] if has_vmix else []) + [_full(a.shape) for a in ins[1 + has_vmix:]]
    y_shape = jax.ShapeDtypeStruct((b, s, MIX_DIM), F32)
    out_shape = y_shape if has_vmix else (y_shape, y_shape)
    out_specs = tok(MIX_DIM) if has_vmix else (tok(MIX_DIM), tok(MIX_DIM))
    n_ch = tb // RWKV_CHUNK
    tok_bf = pltpu.VMEM((tb, MIX_DIM), BF16)
    chan_bf = pltpu.VMEM((MIX_DIM, tb), BF16)
    tok_f32 = pltpu.VMEM((tb, MIX_DIM), F32)
    chan_f32 = pltpu.VMEM((MIX_DIM, tb), F32)
    scratch = [pltpu.VMEM((N_HEADS, HEAD_DIM, HEAD_DIM), F32), pltpu.VMEM((1, nc), F32),
               chan_bf, chan_bf, tok_bf, tok_bf, tok_bf, tok_bf, chan_bf,
               pltpu.VMEM((8, MIX_DIM), F32), tok_f32, tok_f32, chan_f32, chan_f32,
               pltpu.VMEM((N_HEADS, tb, tb), BF16), pltpu.VMEM((N_HEADS, tb, tb), BF16),
               pltpu.VMEM((N_HEADS, 2 * HEAD_DIM, tb), F32),
               pltpu.VMEM((N_HEADS, n_ch, HEAD_DIM, 2 * RWKV_CHUNK), BF16),
               pltpu.VMEM((N_HEADS, n_ch, HEAD_DIM, HEAD_DIM), F32)]
    out = pl.pallas_call(
        functools.partial(_rwkv_kernel, has_vmix),
        grid=(b, s // tb),
        in_specs=in_specs,
        out_specs=out_specs,
        out_shape=out_shape,
        scratch_shapes=scratch,
        compiler_params=_params("parallel", "arbitrary"),
        name="rwkv7_vmix" if has_vmix else "rwkv7_first",
    )(*ins)
    return (out, v_first) if has_vmix else out


DSA_COLS_PAD = 1280
_DSA_Q, _DSA_IQ, _DSA_KIK, _DSA_VW = 0, 512, 1024, 1152


def _rope_table_kernel(pos_ref, freq_ref, cos_ref, sin_ref):
    ang = pos_ref[...] * freq_ref[...]
    lane = lax.broadcasted_iota(jnp.int32, ang.shape, 1)
    cos_ref[...] = jnp.cos(ang)
    sin_ref[...] = jnp.where(lane % HEAD_DIM < HEAD_DIM // 2, -jnp.sin(ang), jnp.sin(ang))


def _rope_tables(positions):
    t = positions.size
    tm = min(ROW_TILE, t)
    inv_freq = 1.0 / (ROPE_THETA ** (jnp.arange(0, HEAD_DIM, 2, dtype=F32) / HEAD_DIM))
    freq = jnp.tile(inv_freq, 4).reshape(1, 128)
    pos = positions.reshape(t, 1).astype(F32)
    spec = pl.BlockSpec((tm, 128), lambda i: (i, 0))
    return pl.pallas_call(
        _rope_table_kernel,
        grid=(t // tm,),
        in_specs=[pl.BlockSpec((tm, 1), lambda i: (i, 0)), _full((1, 128))],
        out_specs=(spec, spec),
        out_shape=(jax.ShapeDtypeStruct((t, 128), F32),) * 2,
        compiler_params=_params("parallel"),
        name="rope_tables",
    )(pos, freq)


def _rope128(x, cos, sin):
    lane = lax.broadcasted_iota(jnp.int32, x.shape, 1)
    half = HEAD_DIM // 2
    rot = jnp.where(lane % HEAD_DIM < half, pltpu.roll(x, 128 - half, 1), pltpu.roll(x, half, 1))
    return x * cos + rot * sin


def _dsa_prep_kernel(cols_ref, cos_ref, sin_ref, qn_ref, kn_ref, q_ref, iq_ref, kik_ref, vw_ref):
    cos, sin = cos_ref[...], sin_ref[...]
    q = cols_ref[:, _DSA_Q:_DSA_Q + MIX_DIM]
    ss = _dot_x2(q * q, _head_ones(MIX_DIM, HEAD_DIM)) * (1.0 / HEAD_DIM)
    q = q * lax.rsqrt(ss + NORM_EPS) * qn_ref[...]
    kik = cols_ref[:, _DSA_KIK:_DSA_KIK + 128]
    ss = _dot_x2(kik * kik, _head_ones(128, HEAD_DIM)) * (1.0 / HEAD_DIM)
    kik_ref[...] = _rope128(kik * lax.rsqrt(ss + NORM_EPS) * kn_ref[...], cos, sin)
    for j in range(MIX_DIM // 128):
        sl = slice(j * 128, (j + 1) * 128)
        q_ref[j] = _rope128(q[:, sl], cos, sin) * (HEAD_DIM ** -0.5 * LOG2_E)
        iq_ref[:, sl] = _rope128(cols_ref[:, _DSA_IQ + j * 128:_DSA_IQ + (j + 1) * 128], cos, sin) * (HEAD_DIM ** -0.5)
    vw = cols_ref[:, _DSA_VW:_DSA_VW + 128]
    vw_ref[...] = jnp.where(lax.broadcasted_iota(jnp.int32, vw.shape, 1) == 127, 1.0, vw)


def _dsa_prep(cols, cos, sin, q_norm, k_norm, idx_k_norm):
    t = cols.shape[0]
    tm = min(ROW_TILE, t)
    qn = jnp.tile(q_norm, N_HEADS).reshape(1, MIX_DIM)
    kn = jnp.concatenate([k_norm, idx_k_norm]).reshape(1, 128)
    tok = lambda w: pl.BlockSpec((tm, w), lambda i: (i, 0))
    sds = lambda w: jax.ShapeDtypeStruct((t, w), F32)
    return pl.pallas_call(
        _dsa_prep_kernel,
        grid=(t // tm,),
        in_specs=[tok(DSA_COLS_PAD), tok(128), tok(128), _full((1, MIX_DIM)), _full((1, 128))],
        out_specs=(pl.BlockSpec((N_HEADS // 2, tm, 128), lambda i: (0, i, 0)), tok(MIX_DIM), tok(128), tok(128)),
        out_shape=(jax.ShapeDtypeStruct((N_HEADS // 2, t, 128), F32), sds(MIX_DIM), sds(128), sds(128)),
        compiler_params=_params("parallel"),
        name="dsa_prep",
    )(cols, cos, sin, qn, kn)


def _count(mask):
    return jnp.sum(jnp.where(mask, 1.0, 0.0), axis=1, keepdims=True)


def _dsa_block(top_k, n_keys, q_ref, iq_ref, vwq_ref, kik_ref, vw_ref, o_ref, key_ref, s_ref):
    qb = iq_ref.shape[1]
    start = pl.program_id(1) * qb
    k = kik_ref[0, 0:n_keys, 0:HEAD_DIM].astype(BF16)
    ik = kik_ref[0, 0:n_keys, HEAD_DIM:2 * HEAD_DIM].astype(BF16)
    v_ones = vw_ref[0, 0:n_keys, :].astype(BF16)
    iw = vwq_ref[0, :, HEAD_DIM:HEAD_DIM + N_HEADS] * (N_HEADS ** -0.5)

    score = jnp.zeros((qb, n_keys), F32)
    for h in range(N_HEADS):
        lg = _dot_nt(iq_ref[0, :, h * HEAD_DIM:(h + 1) * HEAD_DIM].astype(BF16), ik)
        score = score + jnp.maximum(lg, 0.0) * iw[:, h:h + 1]
    key_pos = lax.broadcasted_iota(jnp.int32, (qb, n_keys), 1)
    q_pos = start + lax.broadcasted_iota(jnp.int32, (qb, 1), 0)
    limit = (q_pos // CHUNK + 1) * CHUNK
    adm = key_pos < limit
    score = jnp.where(adm, score, NEG_INF)
    score = jnp.where(score == 0.0, 0.0, score)

    bits = pltpu.bitcast(score, jnp.int32)
    key = bits ^ ((bits >> 31) & jnp.int32(0x7FFFFFFF))
    kf = float(top_k)
    key_ref[:, 0:n_keys] = key
    gr = qb // DSA_ROW_GROUPS
    digit_bits = 2 if n_keys <= DSA_RADIX4_MAX_KEYS else 1
    n_steps = 32 // digit_bits

    def theta_body(i, ths):
        unit = jnp.int32(1) << (32 - digit_bits - digit_bits * i)
        out = []
        for g in range(DSA_ROW_GROUPS):
            keys = key_ref[g * gr:(g + 1) * gr, 0:n_keys]
            passed = jnp.zeros((gr, 1), jnp.int32)
            for d in range(1, 2 ** digit_bits):
                cnt = _count(keys >= ths[g] + unit * d)
                passed = passed + jnp.where(cnt >= kf, 1, 0)
            out.append(ths[g] + unit * passed)
        return tuple(out)

    n_pairs = N_HEADS // 2
    per_pair = n_steps // n_pairs

    def pair_step(hp, ths):
        q2 = q_ref[hp].astype(BF16)
        s_ref[2 * hp, :, 0:n_keys] = _dot_nt(q2[:, 0:HEAD_DIM], k)
        s_ref[2 * hp + 1, :, 0:n_keys] = _dot_nt(q2[:, HEAD_DIM:2 * HEAD_DIM], k)
        for j in range(per_pair):
            ths = theta_body(hp * per_pair + j, ths)
        return ths

    lowest = jnp.full((gr, 1), -2 ** 31, jnp.int32)
    ths = lax.fori_loop(0, n_pairs, pair_step, (lowest,) * DSA_ROW_GROUPS)
    theta = jnp.concatenate(ths, axis=0)
    gt = key > theta
    eq = key == theta
    n_gt = _count(gt)
    need = kf - n_gt
    n_bits = int(np.ceil(np.log2(n_keys))) + 1

    def tie_search():
        def tie_body(i, jb):
            cand = jb + (jnp.int32(1) << (n_bits - 1 - i))
            ok = (cand <= n_keys) & (_count(eq & (key_pos < cand)) <= need)
            return jnp.where(ok, cand, jb)

        return lax.fori_loop(0, n_bits, tie_body, jnp.zeros((qb, 1), jnp.int32))

    has_tie = jnp.max(n_gt + _count(eq & adm)) > kf
    bound = lax.cond(has_tie, tie_search, lambda: jnp.full((qb, 1), n_keys, jnp.int32))
    bias = jnp.where((gt | (eq & (key_pos < bound))) & adm, 0.0, NEG_INF)

    tops = []
    for h in range(N_HEADS):
        s = s_ref[h, :, 0:n_keys] + bias
        s_ref[h, :, 0:n_keys] = s
        tops.append(jnp.max(s, axis=1, keepdims=True))
    for h in range(N_HEADS):
        p = jnp.exp2(s_ref[h, :, 0:n_keys] - tops[h])
        o = _dot(p.astype(BF16), v_ones)
        o_ref[0, :, h * HEAD_DIM:(h + 1) * HEAD_DIM] = o[:, 0:HEAD_DIM] / o[:, 127:128]


def _dsa_attn_kernel(top_k, n_classes, q_ref, iq_ref, vwq_ref, kik_ref, vw_ref, o_ref, key_ref, s_ref):
    qb = iq_ref.shape[1]
    per_class = (kik_ref.shape[1] // qb) // n_classes
    cls = pl.program_id(1) // per_class
    for n in range(n_classes):
        pl.when(cls == n)(functools.partial(
            _dsa_block, top_k, (n + 1) * per_class * qb, q_ref, iq_ref, vwq_ref, kik_ref, vw_ref, o_ref, key_ref, s_ref))


def _dsa_attention(q, iq, kik, vw, b, s):
    top_k = min(DSA_TOPK_MAX, s // 4)
    qb = min(Q_BLOCK, s)
    n_classes = min(DSA_KEY_CLASSES, s // qb)
    r3 = lambda a: a.reshape(b, s, a.shape[-1])
    blk = lambda w: pl.BlockSpec((1, qb, w), lambda i, j: (i, j, 0))
    seq = lambda w: pl.BlockSpec((1, s, w), lambda i, j: (i, 0, 0))
    q_heads = pl.BlockSpec((N_HEADS // 2, qb, 128), lambda i, j: (0, i * (s // qb) + j, 0))
    return pl.pallas_call(
        functools.partial(_dsa_attn_kernel, top_k, n_classes),
        grid=(b, s // qb),
        in_specs=[q_heads, blk(MIX_DIM), blk(128), seq(128), seq(128)],
        out_specs=blk(MIX_DIM),
        out_shape=jax.ShapeDtypeStruct((b, s, MIX_DIM), F32),
        scratch_shapes=[pltpu.VMEM((qb, s), jnp.int32), pltpu.VMEM((N_HEADS, qb, s), F32)],
        compiler_params=_params("parallel", "arbitrary"),
        name="dsa_attention",
    )(q, r3(iq), r3(vw), r3(kik), r3(vw))


def _dsa_weight(w_dsa):
    d = w_dsa.shape[0]
    q, k, v, iq, ik, iw = jnp.split(w_dsa, np.cumsum([512, 64, 64, 512, 64, 8])[:-1].tolist(), axis=1)
    pad = jnp.zeros((d, DSA_COLS_PAD - 1224), w_dsa.dtype)
    return jnp.concatenate([q, iq, k, ik, v, iw, pad], axis=1)


def _merge_kernel(x_ref, g_ref, ya_ref, yb_ref, wa_ref, wb_ref, wo_ref, o_ref):
    d = x_ref.shape[1]
    ga = _sigmoid(g_ref[:, 0:d])
    gb = _sigmoid(g_ref[:, d:2 * d])
    merged = ga * _bdot(ya_ref[...], wa_ref[...]) + gb * _bdot(yb_ref[...], wb_ref[...])
    o_ref[...] = x_ref[...] + _bdot(merged, wo_ref[...])


def _merge(x, gates, ya, yb, wa, wb, wo):
    t, d = x.shape
    tm = min(ROW_TILE, t)
    tok = lambda w: pl.BlockSpec((tm, w), lambda i: (i, 0))
    return pl.pallas_call(
        _merge_kernel,
        grid=(t // tm,),
        in_specs=[tok(d), tok(2 * d), tok(MIX_DIM), tok(MIX_DIM), _full(wa.shape), _full(wb.shape), _full(wo.shape)],
        out_specs=tok(d),
        out_shape=jax.ShapeDtypeStruct((t, d), F32),
        compiler_params=_params("parallel"),
        name="merge_out",
    )(x, gates, ya, yb, wa, wb, wo)


def _mem_kv_kernel(mem_ref, g_ref, wkv_ref, kn_ref, k_ref, v_ref):
    dm = k_ref.shape[2]
    kv = _dot(_rms(mem_ref[0], g_ref[...]).astype(BF16), wkv_ref[...])
    k = kv[:, 0:dm]
    ss = _dot_x2(k * k, _head_ones(dm, MEM_HEAD_DIM)) * (1.0 / MEM_HEAD_DIM)
    k_ref[0] = k * lax.rsqrt(ss + NORM_EPS) * kn_ref[...]
    v_ref[0] = kv[:, dm:2 * dm]


def _mem_kv(mem, g, wkv, k_norm):
    b, m, d = mem.shape
    dm = MEM_HEADS * MEM_HEAD_DIM
    kn = jnp.tile(k_norm, MEM_HEADS).reshape(1, dm)
    spec = pl.BlockSpec((1, m, dm), lambda i: (i, 0, 0))
    return pl.pallas_call(
        _mem_kv_kernel,
        grid=(b,),
        in_specs=[pl.BlockSpec((1, m, d), lambda i: (i, 0, 0)), _full((1, d)), _full(wkv.shape), _full((1, dm))],
        out_specs=(spec, spec),
        out_shape=(jax.ShapeDtypeStruct((b, m, dm), F32),) * 2,
        compiler_params=_params("parallel"),
        name="mem_kv",
    )(mem, g.reshape(1, d), wkv, kn)


def _mem_attn_kernel(x_ref, g_ref, wq_ref, qn_ref, k_ref, v_ref, wo_ref, o_ref):
    x = x_ref[0]
    dm = wq_ref.shape[1]
    q = _dot(_rms(x, g_ref[...]).astype(BF16), wq_ref[...])
    ss = _dot_x2(q * q, _head_ones(dm, MEM_HEAD_DIM)) * (1.0 / MEM_HEAD_DIM)
    q = q * lax.rsqrt(ss + NORM_EPS) * qn_ref[...] * (MEM_HEAD_DIM ** -0.5)
    outs = []
    for h in range(MEM_HEADS):
        sl = slice(h * MEM_HEAD_DIM, (h + 1) * MEM_HEAD_DIM)
        s = _dot_nt(q[:, sl].astype(BF16), k_ref[0, :, sl].astype(BF16))
        p = jnp.exp(s - jnp.max(s, axis=1, keepdims=True))
        p = p / jnp.sum(p, axis=1, keepdims=True)
        outs.append(_dot(p.astype(BF16), v_ref[0, :, sl].astype(BF16)))
    o = jnp.concatenate(outs, axis=1)
    o_ref[0] = x + _bdot(o, wo_ref[...])


def _mem_attention(x, g, wq, q_norm, k, v, wo):
    b, s, d = x.shape
    m, dm = k.shape[1], k.shape[2]
    tm = min(ROW_TILE, s)
    qn = jnp.tile(q_norm, MEM_HEADS).reshape(1, dm)
    tok = pl.BlockSpec((1, tm, d), lambda i, j: (i, j, 0))
    kv = pl.BlockSpec((1, m, dm), lambda i, j: (i, 0, 0))
    return pl.pallas_call(
        _mem_attn_kernel,
        grid=(b, s // tm),
        in_specs=[tok, _full((1, d)), _full(wq.shape), _full((1, dm)), kv, kv, _full(wo.shape)],
        out_specs=tok,
        out_shape=jax.ShapeDtypeStruct((b, s, d), F32),
        compiler_params=_params("parallel", "parallel"),
        name="mem_attention",
    )(x, g.reshape(1, d), wq, qn, k, v, wo)


def _silu(x):
    return x * _sigmoid(x)


def _ffn_kernel(x_ref, g_ref, wg_ref, wu_ref, wd_ref, o_ref, h_ref):
    j = pl.program_id(1)

    @pl.when(j == 0)
    def _():
        x = x_ref[...]
        h_ref[...] = _rms(x, g_ref[...]).astype(BF16)
        o_ref[...] = x

    h = h_ref[...]
    act = _silu(_dot(h, wg_ref[...])) * _dot(h, wu_ref[...])
    o_ref[...] += _bdot(act, wd_ref[...])


def _ffn(x, g, wg, wu, wd, n_f=2):
    t, d = x.shape
    f = wg.shape[1]
    tm = min(FFN_ROW_TILE, t)
    tf = f // n_f
    tok = pl.BlockSpec((tm, d), lambda i, j: (i, 0))
    return pl.pallas_call(
        _ffn_kernel,
        grid=(t // tm, n_f),
        in_specs=[tok, _full((1, d)), pl.BlockSpec((d, tf), lambda i, j: (0, j)),
                  pl.BlockSpec((d, tf), lambda i, j: (0, j)), pl.BlockSpec((tf, d), lambda i, j: (j, 0))],
        out_specs=tok,
        out_shape=jax.ShapeDtypeStruct((t, d), F32),
        scratch_shapes=[pltpu.VMEM((tm, d), BF16)],
        compiler_params=_params("parallel", "arbitrary"),
        name="ffn_swiglu",
    )(x, g.reshape(1, d), wg, wu, wd)


def _moe_kernel(x_ref, g_ref, rh_ref, rl_ref, b_ref, wg_ref, wu_ref, wd_ref, o_ref, h_ref, gate_ref, rank_ref,
                gate_t_ref, rank_t_ref):
    e = pl.program_id(1)
    tm = x_ref.shape[0]

    @pl.when(e == 0)
    def _():
        x = x_ref[...]
        hn = _rms(x, g_ref[...])
        h_ref[...] = hn.astype(BF16)
        o_ref[...] = x
        logits = _dot_x3(hn, rh_ref[...], rl_ref[...]) + b_ref[...]
        lane = lax.broadcasted_iota(jnp.int32, logits.shape, 1)
        logits = jnp.where(lane < N_EXPERTS, logits, -jnp.inf)
        m1 = jnp.max(logits, axis=1, keepdims=True)
        i1 = jnp.min(jnp.where(logits == m1, lane, 128), axis=1, keepdims=True)
        rest = jnp.where(lane == i1, -jnp.inf, logits)
        m2 = jnp.max(rest, axis=1, keepdims=True)
        i2 = jnp.min(jnp.where(rest == m2, lane, 128), axis=1, keepdims=True)
        e2 = jnp.exp(m2 - m1)
        w1 = 1.0 / (1.0 + e2)
        w2 = e2 / (1.0 + e2)
        gates = jnp.where(lane == i1, w1, 0.0) + jnp.where(lane == i2, w2, 0.0)
        gate_ref[...] = gates
        member = jnp.where(gates != 0.0, 1.0, 0.0).astype(BF16)
        earlier = jnp.where(lax.broadcasted_iota(jnp.int32, (tm, tm), 1)
                            < lax.broadcasted_iota(jnp.int32, (tm, tm), 0), 1.0, 0.0).astype(BF16)
        rank = _dot(earlier, member)
        rank_ref[...] = rank
        gate_t_ref[...] = gates.T
        rank_t_ref[...] = rank.T

    lane = lax.broadcasted_iota(jnp.int32, gate_ref.shape, 1)
    gate_col = jnp.sum(jnp.where(lane == e, gate_ref[...], 0.0), axis=1, keepdims=True)
    rank_col = jnp.sum(jnp.where(lane == e, rank_ref[...], 0.0), axis=1, keepdims=True)
    gate_row = gate_t_ref[pl.ds(e, 1), :]
    rank_row = rank_t_ref[pl.ds(e, 1), :]
    n_tok = jnp.sum(jnp.where(gate_col != 0.0, 1.0, 0.0)).astype(jnp.int32)
    sub = min(MOE_SUB, tm)

    def sub_tile(s_idx, carry):
        base = (s_idx * sub).astype(F32)
        slot_col = base + lax.broadcasted_iota(jnp.int32, (sub, 1), 0).astype(F32)
        slot_row = base + lax.broadcasted_iota(jnp.int32, (1, sub), 1).astype(F32)
        pick = (gate_row != 0.0) & (rank_row == slot_col)
        place = (gate_col != 0.0) & (rank_col == slot_row)
        hs = _dot(jnp.where(pick, 1.0, 0.0).astype(BF16), h_ref[...]).astype(BF16)
        act = _silu(_dot(hs, wg_ref[0])) * _dot(hs, wu_ref[0])
        out = _bdot(act, wd_ref[0]) * jnp.sum(jnp.where(pick, gate_row, 0.0), axis=1, keepdims=True)
        o_ref[...] += _dot(jnp.where(place, 1.0, 0.0).astype(BF16), out.astype(BF16))
        return carry

    lax.fori_loop(0, (n_tok + sub - 1) // sub, sub_tile, 0)


def _moe(x, g, router, bias, wg, wu, wd):
    t, d = x.shape
    n_e, _, f = wg.shape
    tm = min(FFN_ROW_TILE, t)
    r_pad = jnp.zeros((d, 128), F32).at[:, 0:n_e].set(router)
    b_pad = jnp.zeros((1, 128), F32).at[0, 0:n_e].set(bias)
    tok = pl.BlockSpec((tm, d), lambda i, j: (i, 0))
    return pl.pallas_call(
        _moe_kernel,
        grid=(t // tm, n_e),
        in_specs=[tok, _full((1, d)), _full((d, 128)), _full((d, 128)), _full((1, 128)),
                  pl.BlockSpec((1, d, f), lambda i, j: (j, 0, 0)), pl.BlockSpec((1, d, f), lambda i, j: (j, 0, 0)),
                  pl.BlockSpec((1, f, d), lambda i, j: (j, 0, 0))],
        out_specs=tok,
        out_shape=jax.ShapeDtypeStruct((t, d), F32),
        scratch_shapes=[pltpu.VMEM((tm, d), BF16), pltpu.VMEM((tm, 128), F32), pltpu.VMEM((tm, 128), F32),
                        pltpu.VMEM((128, tm), F32), pltpu.VMEM((128, tm), F32)],
        compiler_params=_params("parallel", "arbitrary"),
        name="moe_swiglu",
    )(x, g.reshape(1, d), *_split(r_pad), b_pad, wg, wu, wd)


def kernel(x, mem, positions, norm_mix, w_in, rwkv_mu, rwkv_w0, rwkv_w2, rwkv_a0, rwkv_a2, rwkv_g2, rwkv_v0, rwkv_v1, rwkv_v2, rwkv_kk, rwkv_ka, rwkv_rk, rwkv_lnx_g, rwkv_lnx_b, dsa_q_norm, dsa_k_norm, idx_k_norm, w_branch_a, w_branch_b, w_out, norm_mem, mem_tok_norm, mem_wq, mem_wkv, mem_q_norm, mem_k_norm, mem_wo, norm_ffn, ffn_wg, ffn_wu, ffn_wd, moe_router, moe_bias, moe_wg, moe_wu, moe_wd):
    w = dict(rwkv_mu=rwkv_mu, rwkv_w0=rwkv_w0, rwkv_w2=rwkv_w2, rwkv_a0=rwkv_a0, rwkv_a2=rwkv_a2,
             rwkv_g2=rwkv_g2, rwkv_v0=rwkv_v0, rwkv_v1=rwkv_v1, rwkv_v2=rwkv_v2, rwkv_kk=rwkv_kk,
             rwkv_ka=rwkv_ka, rwkv_rk=rwkv_rk, rwkv_lnx_g=rwkv_lnx_g, rwkv_lnx_b=rwkv_lnx_b)
    b, s, d = x.shape
    t = b * s
    depth = w_in.shape[0]
    n_a = w_in.shape[2] - 1224 - 2 * d
    bf = lambda a: a.astype(BF16)
    cos, sin = _rope_tables(positions)
    x = x.reshape(t, d)
    v_first = None
    for l in range(depth):
        w_a = bf(w_in[l, :, 0:n_a])
        w_b = bf(_dsa_weight(w_in[l, :, n_a:n_a + 1224]))
        w_g = bf(w_in[l, :, n_a + 1224:])
        cols_a, cols_b, gates = _norm_matmul(x, norm_mix[l], (w_a, w_b, w_g))
        y_a, v_first = _rwkv_branch(cols_a.reshape(b, s, n_a), v_first, _rwkv_params(w, l))
        q, iq, kik, vw = _dsa_prep(cols_b, cos, sin, dsa_q_norm[l], dsa_k_norm[l], idx_k_norm[l])
        y_b = _dsa_attention(q, iq, kik, vw, b, s)
        x = _merge(x, gates, y_a.reshape(t, MIX_DIM), y_b.reshape(t, MIX_DIM),
                   bf(w_branch_a[l]), bf(w_branch_b[l]), bf(w_out[l]))
        mk, mv = _mem_kv(mem, mem_tok_norm[l], bf(mem_wkv[l]), mem_k_norm[l])
        x = _mem_attention(x.reshape(b, s, d), norm_mem[l], bf(mem_wq[l]), mem_q_norm[l], mk, mv,
                           bf(mem_wo[l])).reshape(t, d)
        j = l // 2
        if l % 2 == 0:
            x = _ffn(x, norm_ffn[l], bf(ffn_wg[j]), bf(ffn_wu[j]), bf(ffn_wd[j]))
        else:
            x = _moe(x, norm_ffn[l], moe_router[j], moe_bias[j], bf(moe_wg[j]), bf(moe_wu[j]), bf(moe_wd[j]))
    return x.reshape(b, s, d)
```

```python
import functools

import jax
import jax.numpy as jnp
import numpy as np
from jax import lax
from jax.experimental import pallas as pl
from jax.experimental.pallas import tpu as pltpu

F32 = jnp.float32
BF16 = jnp.bfloat16
HIGHEST = lax.Precision.HIGHEST

NORM_EPS = 1e-6
NEG_INF = -1e30
ROPE_THETA = 10000.0
LOG2_E = float(np.log2(np.e))

CHUNK = 64
Q_BLOCK = 128
DSA_TOPK_MAX = 256
DSA_KEY_CLASSES = 8
DSA_ROW_GROUPS = 4
DSA_RADIX4_MAX_KEYS = 768
HEAD_DIM = 64
N_HEADS = 8
MIX_DIM = N_HEADS * HEAD_DIM
RWKV_LNX_EPS = 1e-5 * HEAD_DIM
MEM_HEADS = 4
MEM_HEAD_DIM = 128
N_EXPERTS = 8

RWKV_CHUNK = 64
RWKV_BLOCK = 256
ROW_TILE = 512
FFN_ROW_TILE = 1024
MOE_SUB = 288
VMEM_LIMIT = 56 * 1024 * 1024


def _params(*sem):
    return pltpu.CompilerParams(dimension_semantics=sem, vmem_limit_bytes=VMEM_LIMIT)


def _full(shape):
    nd = len(shape)
    return pl.BlockSpec(shape, lambda *_: (0,) * nd)


def _dot(a, b, precision=None):
    return jnp.dot(a, b, preferred_element_type=F32, precision=precision)


def _dot_nt(a, b, precision=None):
    return lax.dot_general(a, b, (((1,), (1,)), ((), ())), preferred_element_type=F32,
                           precision=precision)


def _dot_tn(a, b, precision=None):
    return lax.dot_general(a, b, (((0,), (0,)), ((), ())), preferred_element_type=F32,
                           precision=precision)


def _bdot(a, b):
    return _dot(a.astype(BF16), b.astype(BF16))


def _split(x):
    hi = x.astype(BF16)
    return hi, (x - hi.astype(F32)).astype(BF16)


def _dot_x2(x, m):
    hi, lo = _split(x)
    return _dot(hi, m) + _dot(lo, m)


def _dot_x3(x, w_hi, w_lo):
    hi, lo = _split(x)
    return _dot(hi, w_hi) + (_dot(lo, w_hi) + _dot(hi, w_lo))


def _rms(x, g):
    return x * lax.rsqrt(jnp.mean(x * x, axis=-1, keepdims=True) + NORM_EPS) * g


def _sigmoid(x):
    return 1.0 / (1.0 + jnp.exp(-x))


def _head_ones(width, head):
    r = lax.broadcasted_iota(jnp.int32, (width, width), 0) // head
    c = lax.broadcasted_iota(jnp.int32, (width, width), 1) // head
    return (r == c).astype(BF16)


def _norm_matmul_kernel(x_ref, g_ref, *refs):
    n = len(refs) // 2
    h = _rms(x_ref[...], g_ref[...]).astype(BF16)
    for w_ref, o_ref in zip(refs[:n], refs[n:]):
        o_ref[...] = _dot(h, w_ref[...])


def _norm_matmul(x, g, ws, tm=ROW_TILE):
    t, d = x.shape
    tm = min(tm, t)
    return pl.pallas_call(
        _norm_matmul_kernel,
        grid=(t // tm,),
        in_specs=[pl.BlockSpec((tm, d), lambda i: (i, 0)), _full((1, d))] + [_full(w.shape) for w in ws],
        out_specs=tuple(pl.BlockSpec((tm, w.shape[1]), lambda i: (i, 0)) for w in ws),
        out_shape=tuple(jax.ShapeDtypeStruct((t, w.shape[1]), F32) for w in ws),
        compiler_params=_params("parallel"),
        name="norm_matmul",
    )(x, g.reshape(1, d), *ws)


def _rwkv_kernel(has_vmix, *refs):
    if has_vmix:
        (cols_ref, vf_ref, mu_ref, w0_ref, a0_ref, wah_ref, wal_ref, g2_ref, kk_ref, ka_ref,
         rk_ref, lg_ref, lb_ref, v0_ref, v1_ref, v2_ref,
         y_ref, st_ref, prev_ref, at_ref, rt_ref, bt_ref, kt_ref, bh_ref, kh_ref, vv_ref,
         gdec_ref, bonus_ref, gate_ref, yy_ref, y0_ref, p_ref, rb_ref, z_ref, rhs_ref, dl_ref) = refs
        vout_ref = None
    else:
        (cols_ref, mu_ref, w0_ref, a0_ref, wah_ref, wal_ref, g2_ref, kk_ref, ka_ref,
         rk_ref, lg_ref, lb_ref,
         y_ref, vout_ref, st_ref, prev_ref, at_ref, rt_ref, bt_ref, kt_ref, bh_ref, kh_ref, vv_ref,
         gdec_ref, bonus_ref, gate_ref, yy_ref, y0_ref, p_ref, rb_ref, z_ref, rhs_ref, dl_ref) = refs

    tb = cols_ref.shape[1]
    c = RWKV_CHUNK
    n_chunks = tb // c
    hd = HEAD_DIM

    @pl.when(pl.program_id(1) == 0)
    def _():
        st_ref[...] = jnp.zeros_like(st_ref)
        prev_ref[...] = jnp.zeros_like(prev_ref)

    cols = cols_ref[0]
    row = lax.broadcasted_iota(jnp.int32, cols.shape, 0)
    shifted = jnp.where(row == 0, prev_ref[...], pltpu.roll(cols, 1, 0))
    prev_ref[...] = cols[tb - 1:tb, :]
    mixed = cols + (shifted - cols) * mu_ref[...]

    r = mixed[:, 0:MIX_DIM]
    k = mixed[:, MIX_DIM:2 * MIX_DIM]
    v = mixed[:, 2 * MIX_DIM:3 * MIX_DIM]
    wa = mixed[:, 3 * MIX_DIM:3 * MIX_DIM + 128]
    gd = mixed[:, 3 * MIX_DIM + 128:3 * MIX_DIM + 256]
    lane = lax.broadcasted_iota(jnp.int32, wa.shape, 1)
    wa = jnp.where(lane < 64, jnp.tanh(wa), wa)
    wa_out = _dot_x3(wa, wah_ref[...], wal_ref[...])
    u_dec = w0_ref[...] + wa_out[:, 0:MIX_DIM]
    a = _sigmoid(a0_ref[...] + wa_out[:, MIX_DIM:2 * MIX_DIM])
    gate_ref[...] = _bdot(_sigmoid(gd), g2_ref[...])
    if has_vmix:
        vm = _bdot(_bdot(v, v1_ref[...]), v2_ref[...])
        v = v + (vf_ref[0] - v) * _sigmoid(v0_ref[...] + vm)
    else:
        vout_ref[0] = v

    hb = _head_ones(MIX_DIM, hd)
    kk = k * kk_ref[...]
    kk = kk / jnp.maximum(jnp.sqrt(_dot_x2(kk * kk, hb)), 1e-12)
    k = k * (1.0 + (a - 1.0) * ka_ref[...])
    bonus_ref[...] = _bdot(r * k * rk_ref[...], hb) * v
    ll = -_sigmoid(u_dec) * float(np.exp(-0.5))

    ri = lax.broadcasted_iota(jnp.int32, (tb, tb), 0)
    ci = lax.broadcasted_iota(jnp.int32, (tb, tb), 1)
    same = (ri // c) == (ci // c)
    strict_u = (same & (ri < ci)).astype(F32)
    incl_u = (same & (ri <= ci)).astype(F32)
    ll_hi, ll_lo = _split(ll)
    tri = (same & (ri >= ci)).astype(BF16)
    cum = _dot(tri, ll_hi) + _dot(tri, ll_lo)
    tot_rows = [cum[(ch + 1) * c - 1:(ch + 1) * c, :] for ch in range(n_chunks)]
    tot = jnp.concatenate([jnp.broadcast_to(t_row, (c, MIX_DIM)) for t_row in tot_rows], axis=0)
    for ch in range(n_chunks):
        gdec_ref[ch:ch + 1, :] = jnp.exp(tot_rows[ch])
    g_inv = jnp.exp(-cum)
    g_tail = jnp.exp(tot - cum)
    beta = kk * a
    at_ref[...] = (-kk * jnp.exp(cum - ll)).T.astype(BF16)
    rt_ref[...] = (r * jnp.exp(cum)).T.astype(BF16)
    vv_ref[...] = v.T.astype(BF16)
    bt_ref[...] = (beta * g_inv).astype(BF16)
    kt_ref[...] = (k * g_inv).astype(BF16)
    bh_ref[...] = (beta * g_tail).astype(BF16)
    kh_ref[...] = (k * g_tail).astype(BF16)

    n_levels = int(np.log2(c))
    heads = [slice(h * hd, (h + 1) * hd) for h in range(N_HEADS)]

    for h, cs in enumerate(heads):
        b_t = bt_ref[:, cs]
        k_t = kt_ref[:, cs]
        a_tr = at_ref[cs, :]
        r_tr = rt_ref[cs, :]
        v_tr = vv_ref[cs, :]
        p_ref[h] = (_dot(b_t, a_tr) * strict_u).astype(BF16)
        m_tr = (_dot(k_t, a_tr) * strict_u).astype(BF16)
        rb_ref[h] = (_dot(b_t, r_tr) * incl_u).astype(BF16)
        rk_tr = (_dot(k_t, r_tr) * incl_u).astype(BF16)
        y0_ref[cs, :] = _dot(v_tr, rk_tr)
        z_ref[h] = jnp.concatenate([a_tr.astype(F32), _dot(v_tr, m_tr)], axis=0)

    for lev in range(n_levels):
        for h in range(N_HEADS):
            p = p_ref[h]
            z = z_ref[h]
            z_ref[h] = z + _dot(z.astype(BF16), p)
            if lev < n_levels - 1:
                p_ref[h] = _dot(p, p).astype(BF16)

    for h, cs in enumerate(heads):
        z_b = z_ref[h].astype(BF16)
        w = _dot(z_b, rb_ref[h])
        r_bar = (rt_ref[cs, :].astype(F32) + w[0:hd]).astype(BF16)
        y0_ref[cs, :] = y0_ref[cs, :] + w[hd:2 * hd]
        for ch in range(n_chunks):
            cols = slice(ch * c, (ch + 1) * c)
            pd = _dot(z_b[:, cols], bh_ref[cols, cs])
            d2 = _dot(vv_ref[cs, cols], kh_ref[cols, cs])
            rhs_ref[h, ch] = jnp.concatenate([pd[0:hd].astype(BF16), r_bar[:, cols]], axis=1)
            dl_ref[h, ch] = pd[hd:2 * hd] + d2

    for ch in range(n_chunks):
        cols = slice(ch * c, (ch + 1) * c)
        for h, cs in enumerate(heads):
            st = st_ref[h]
            res = _dot(st.astype(BF16), rhs_ref[h, ch])
            yy_ref[cs, cols] = res[:, hd:hd + c] + y0_ref[cs, cols]
            st_ref[h] = st * gdec_ref[ch:ch + 1, cs] + res[:, 0:hd] + dl_ref[h, ch]

    y = yy_ref[...].T
    mean = _bdot(y, hb) * (1.0 / hd)
    yc = y - mean
    var = _bdot(yc * yc, hb) * (1.0 / hd)
    yn = yc * lax.rsqrt(var + RWKV_LNX_EPS) * lg_ref[...] + lb_ref[...]
    y_ref[0] = ((yn + bonus_ref[...]) * gate_ref[...]).astype(y_ref.dtype)


def _rwkv_params(w, l):
    p = {k: w["rwkv_" + k][l] for k in ("mu", "w0", "w2", "a0", "a2", "g2", "kk", "ka", "rk", "lnx_g", "lnx_b")}
    if l > 0:
        p.update({k: w["rwkv_" + k][l - 1] for k in ("v0", "v1", "v2")})
    return p


def _rwkv_branch(cols, v_first, p):
    b, s, nc = cols.shape
    tb = min(RWKV_BLOCK, s)
    has_vmix = v_first is not None
    row = lambda a: a.reshape(1, -1)
    pair = lambda a: list(_split(a))
    tok = lambda w: pl.BlockSpec((1, tb, w), lambda i, j: (i, j, 0))
    wa2 = jnp.zeros((128, 2 * MIX_DIM), F32)
    wa2 = wa2.at[0:64, 0:MIX_DIM].set(p["w2"]).at[64:128, MIX_DIM:].set(p["a2"])
    ins = [cols] + ([v_first] if has_vmix else []) + [
        row(p["mu"]), row(p["w0"]), row(p["a0"])] + pair(wa2) + [p["g2"].astype(BF16)] + [
        row(p["kk"]), row(p["ka"]), row(p["rk"]), row(p["lnx_g"]), row(p["lnx_b"])]
    if has_vmix:
        ins += [row(p["v0"]), p["v1"].astype(BF16), p["v2"].astype(BF16)]
    in_specs = [tok(nc)] + ([tok(MIX_DIM)] if has_vmix else []) + [_full(a.shape) for a in ins[1 + has_vmix:]]
    y_shape = jax.ShapeDtypeStruct((b, s, MIX_DIM), F32)
    out_shape = y_shape if has_vmix else (y_shape, y_shape)
    out_specs = tok(MIX_DIM) if has_vmix else (tok(MIX_DIM), tok(MIX_DIM))
    n_ch = tb // RWKV_CHUNK
    tok_bf = pltpu.VMEM((tb, MIX_DIM), BF16)
    chan_bf = pltpu.VMEM((MIX_DIM, tb), BF16)
    tok_f32 = pltpu.VMEM((tb, MIX_DIM), F32)
    chan_f32 = pltpu.VMEM((MIX_DIM, tb), F32)
    scratch = [pltpu.VMEM((N_HEADS, HEAD_DIM, HEAD_DIM), F32), pltpu.VMEM((1, nc), F32),
               chan_bf, chan_bf, tok_bf, tok_bf, tok_bf, tok_bf, chan_bf,
               pltpu.VMEM((8, MIX_DIM), F32), tok_f32, tok_f32, chan_f32, chan_f32,
               pltpu.VMEM((N_HEADS, tb, tb), BF16), pltpu.VMEM((N_HEADS, tb, tb), BF16),
               pltpu.VMEM((N_HEADS, 2 * HEAD_DIM, tb), F32),
               pltpu.VMEM((N_HEADS, n_ch, HEAD_DIM, 2 * RWKV_CHUNK), BF16),
               pltpu.VMEM((N_HEADS, n_ch, HEAD_DIM, HEAD_DIM), F32)]
    out = pl.pallas_call(
        functools.partial(_rwkv_kernel, has_vmix),
        grid=(b, s // tb),
        in_specs=in_specs,
        out_specs=out_specs,
        out_shape=out_shape,
        scratch_shapes=scratch,
        compiler_params=_params("parallel", "arbitrary"),
        name="rwkv7_vmix" if has_vmix else "rwkv7_first",
    )(*ins)
    return (out, v_first) if has_vmix else out


DSA_COLS_PAD = 1280
_DSA_Q, _DSA_IQ, _DSA_KIK, _DSA_VW = 0, 512, 1024, 1152


def _rope_table_kernel(pos_ref, freq_ref, cos_ref, sin_ref):
    ang = pos_ref[...] * freq_ref[...]
    lane = lax.broadcasted_iota(jnp.int32, ang.shape, 1)
    cos_ref[...] = jnp.cos(ang)
    sin_ref[...] = jnp.where(lane % HEAD_DIM < HEAD_DIM // 2, -jnp.sin(ang), jnp.sin(ang))


def _rope_tables(positions):
    t = positions.size
    tm = min(ROW_TILE, t)
    inv_freq = 1.0 / (ROPE_THETA ** (jnp.arange(0, HEAD_DIM, 2, dtype=F32) / HEAD_DIM))
    freq = jnp.tile(inv_freq, 4).reshape(1, 128)
    pos = positions.reshape(t, 1).astype(F32)
    spec = pl.BlockSpec((tm, 128), lambda i: (i, 0))
    return pl.pallas_call(
        _rope_table_kernel,
        grid=(t // tm,),
        in_specs=[pl.BlockSpec((tm, 1), lambda i: (i, 0)), _full((1, 128))],
        out_specs=(spec, spec),
        out_shape=(jax.ShapeDtypeStruct((t, 128), F32),) * 2,
        compiler_params=_params("parallel"),
        name="rope_tables",
    )(pos, freq)


def _rope128(x, cos, sin):
    lane = lax.broadcasted_iota(jnp.int32, x.shape, 1)
    half = HEAD_DIM // 2
    rot = jnp.where(lane % HEAD_DIM < half, pltpu.roll(x, 128 - half, 1), pltpu.roll(x, half, 1))
    return x * cos + rot * sin


def _dsa_prep_kernel(cols_ref, cos_ref, sin_ref, qn_ref, kn_ref, q_ref, iq_ref, kik_ref, vw_ref):
    cos, sin = cos_ref[...], sin_ref[...]
    q = cols_ref[:, _DSA_Q:_DSA_Q + MIX_DIM]
    ss = _dot_x2(q * q, _head_ones(MIX_DIM, HEAD_DIM)) * (1.0 / HEAD_DIM)
    q = q * lax.rsqrt(ss + NORM_EPS) * qn_ref[...]
    kik = cols_ref[:, _DSA_KIK:_DSA_KIK + 128]
    ss = _dot_x2(kik * kik, _head_ones(128, HEAD_DIM)) * (1.0 / HEAD_DIM)
    kik_ref[...] = _rope128(kik * lax.rsqrt(ss + NORM_EPS) * kn_ref[...], cos, sin)
    for j in range(MIX_DIM // 128):
        sl = slice(j * 128, (j + 1) * 128)
        q_ref[j] = _rope128(q[:, sl], cos, sin) * (HEAD_DIM ** -0.5 * LOG2_E)
        iq_ref[:, sl] = _rope128(cols_ref[:, _DSA_IQ + j * 128:_DSA_IQ + (j + 1) * 128], cos, sin) * (HEAD_DIM ** -0.5)
    vw = cols_ref[:, _DSA_VW:_DSA_VW + 128]
    vw_ref[...] = jnp.where(lax.broadcasted_iota(jnp.int32, vw.shape, 1) == 127, 1.0, vw)


def _dsa_prep(cols, cos, sin, q_norm, k_norm, idx_k_norm):
    t = cols.shape[0]
    tm = min(ROW_TILE, t)
    qn = jnp.tile(q_norm, N_HEADS).reshape(1, MIX_DIM)
    kn = jnp.concatenate([k_norm, idx_k_norm]).reshape(1, 128)
    tok = lambda w: pl.BlockSpec((tm, w), lambda i: (i, 0))
    sds = lambda w: jax.ShapeDtypeStruct((t, w), F32)
    return pl.pallas_call(
        _dsa_prep_kernel,
        grid=(t // tm,),
        in_specs=[tok(DSA_COLS_PAD), tok(128), tok(128), _full((1, MIX_DIM)), _full((1, 128))],
        out_specs=(pl.BlockSpec((N_HEADS // 2, tm, 128), lambda i: (0, i, 0)), tok(MIX_DIM), tok(128), tok(128)),
        out_shape=(jax.ShapeDtypeStruct((N_HEADS // 2, t, 128), F32), sds(MIX_DIM), sds(128), sds(128)),
        compiler_params=_params("parallel"),
        name="dsa_prep",
    )(cols, cos, sin, qn, kn)


def _count(mask):
    return jnp.sum(jnp.where(mask, 1.0, 0.0), axis=1, keepdims=True)


def _dsa_block(top_k, n_keys, q_ref, iq_ref, vwq_ref, kik_ref, vw_ref, o_ref, key_ref, s_ref):
    qb = iq_ref.shape[1]
    start = pl.program_id(1) * qb
    k = kik_ref[0, 0:n_keys, 0:HEAD_DIM].astype(BF16)
    ik = kik_ref[0, 0:n_keys, HEAD_DIM:2 * HEAD_DIM].astype(BF16)
    v_ones = vw_ref[0, 0:n_keys, :].astype(BF16)
    iw = vwq_ref[0, :, HEAD_DIM:HEAD_DIM + N_HEADS] * (N_HEADS ** -0.5)

    score = jnp.zeros((qb, n_keys), F32)
    for h in range(N_HEADS):
        lg = _dot_nt(iq_ref[0, :, h * HEAD_DIM:(h + 1) * HEAD_DIM].astype(BF16), ik)
        score = score + jnp.maximum(lg, 0.0) * iw[:, h:h + 1]
    key_pos = lax.broadcasted_iota(jnp.int32, (qb, n_keys), 1)
    q_pos = start + lax.broadcasted_iota(jnp.int32, (qb, 1), 0)
    limit = (q_pos // CHUNK + 1) * CHUNK
    adm = key_pos < limit
    score = jnp.where(adm, score, NEG_INF)
    score = jnp.where(score == 0.0, 0.0, score)

    bits = pltpu.bitcast(score, jnp.int32)
    key = bits ^ ((bits >> 31) & jnp.int32(0x7FFFFFFF))
    kf = float(top_k)
    key_ref[:, 0:n_keys] = key
    gr = qb // DSA_ROW_GROUPS
    digit_bits = 2 if n_keys <= DSA_RADIX4_MAX_KEYS else 1
    n_steps = 32 // digit_bits

    def theta_body(i, ths):
        unit = jnp.int32(1) << (32 - digit_bits - digit_bits * i)
        out = []
        for g in range(DSA_ROW_GROUPS):
            keys = key_ref[g * gr:(g + 1) * gr, 0:n_keys]
            passed = jnp.zeros((gr, 1), jnp.int32)
            for d in range(1, 2 ** digit_bits):
                cnt = _count(keys >= ths[g] + unit * d)
                passed = passed + jnp.where(cnt >= kf, 1, 0)
            out.append(ths[g] + unit * passed)
        return tuple(out)

    n_pairs = N_HEADS // 2
    per_pair = n_steps // n_pairs

    def pair_step(hp, ths):
        q2 = q_ref[hp].astype(BF16)
        s_ref[2 * hp, :, 0:n_keys] = _dot_nt(q2[:, 0:HEAD_DIM], k)
        s_ref[2 * hp + 1, :, 0:n_keys] = _dot_nt(q2[:, HEAD_DIM:2 * HEAD_DIM], k)
        for j in range(per_pair):
            ths = theta_body(hp * per_pair + j, ths)
        return ths

    lowest = jnp.full((gr, 1), -2 ** 31, jnp.int32)
    ths = lax.fori_loop(0, n_pairs, pair_step, (lowest,) * DSA_ROW_GROUPS)
    theta = jnp.concatenate(ths, axis=0)
    gt = key > theta
    eq = key == theta
    n_gt = _count(gt)
    need = kf - n_gt
    n_bits = int(np.ceil(np.log2(n_keys))) + 1

    def tie_search():
        def tie_body(i, jb):
            cand = jb + (jnp.int32(1) << (n_bits - 1 - i))
            ok = (cand <= n_keys) & (_count(eq & (key_pos < cand)) <= need)
            return jnp.where(ok, cand, jb)

        return lax.fori_loop(0, n_bits, tie_body, jnp.zeros((qb, 1), jnp.int32))

    has_tie = jnp.max(n_gt + _count(eq & adm)) > kf
    bound = lax.cond(has_tie, tie_search, lambda: jnp.full((qb, 1), n_keys, jnp.int32))
    bias = jnp.where((gt | (eq & (key_pos < bound))) & adm, 0.0, NEG_INF)

    tops = []
    for h in range(N_HEADS):
        s = s_ref[h, :, 0:n_keys] + bias
        s_ref[h, :, 0:n_keys] = s
        tops.append(jnp.max(s, axis=1, keepdims=True))
    for h in range(N_HEADS):
        p = jnp.exp2(s_ref[h, :, 0:n_keys] - tops[h])
        o = _dot(p.astype(BF16), v_ones)
        o_ref[0, :, h * HEAD_DIM:(h + 1) * HEAD_DIM] = o[:, 0:HEAD_DIM] / o[:, 127:128]


def _dsa_attn_kernel(top_k, n_classes, q_ref, iq_ref, vwq_ref, kik_ref, vw_ref, o_ref, key_ref, s_ref):
    qb = iq_ref.shape[1]
    per_class = (kik_ref.shape[1] // qb) // n_classes
    cls = pl.program_id(1) // per_class
    for n in range(n_classes):
        pl.when(cls == n)(functools.partial(
            _dsa_block, top_k, (n + 1) * per_class * qb, q_ref, iq_ref, vwq_ref, kik_ref, vw_ref, o_ref, key_ref, s_ref))


def _dsa_attention(q, iq, kik, vw, b, s):
    top_k = min(DSA_TOPK_MAX, s // 4)
    qb = min(Q_BLOCK, s)
    n_classes = min(DSA_KEY_CLASSES, s // qb)
    r3 = lambda a: a.reshape(b, s, a.shape[-1])
    blk = lambda w: pl.BlockSpec((1, qb, w), lambda i, j: (i, j, 0))
    seq = lambda w: pl.BlockSpec((1, s, w), lambda i, j: (i, 0, 0))
    q_heads = pl.BlockSpec((N_HEADS // 2, qb, 128), lambda i, j: (0, i * (s // qb) + j, 0))
    return pl.pallas_call(
        functools.partial(_dsa_attn_kernel, top_k, n_classes),
        grid=(b, s // qb),
        in_specs=[q_heads, blk(MIX_DIM), blk(128), seq(128), seq(128)],
        out_specs=blk(MIX_DIM),
        out_shape=jax.ShapeDtypeStruct((b, s, MIX_DIM), F32),
        scratch_shapes=[pltpu.VMEM((qb, s), jnp.int32), pltpu.VMEM((N_HEADS, qb, s), F32)],
        compiler_params=_params("parallel", "arbitrary"),
        name="dsa_attention",
    )(q, r3(iq), r3(vw), r3(kik), r3(vw))


def _dsa_weight(w_dsa):
    d = w_dsa.shape[0]
    q, k, v, iq, ik, iw = jnp.split(w_dsa, np.cumsum([512, 64, 64, 512, 64, 8])[:-1].tolist(), axis=1)
    pad = jnp.zeros((d, DSA_COLS_PAD - 1224), w_dsa.dtype)
    return jnp.concatenate([q, iq, k, ik, v, iw, pad], axis=1)


def _merge_kernel(x_ref, g_ref, ya_ref, yb_ref, wa_ref, wb_ref, wo_ref, o_ref):
    d = x_ref.shape[1]
    ga = _sigmoid(g_ref[:, 0:d])
    gb = _sigmoid(g_ref[:, d:2 * d])
    merged = ga * _bdot(ya_ref[...], wa_ref[...]) + gb * _bdot(yb_ref[...], wb_ref[...])
    o_ref[...] = x_ref[...] + _bdot(merged, wo_ref[...])


def _merge(x, gates, ya, yb, wa, wb, wo):
    t, d = x.shape
    tm = min(ROW_TILE, t)
    tok = lambda w: pl.BlockSpec((tm, w), lambda i: (i, 0))
    return pl.pallas_call(
        _merge_kernel,
        grid=(t // tm,),
        in_specs=[tok(d), tok(2 * d), tok(MIX_DIM), tok(MIX_DIM), _full(wa.shape), _full(wb.shape), _full(wo.shape)],
        out_specs=tok(d),
        out_shape=jax.ShapeDtypeStruct((t, d), F32),
        compiler_params=_params("parallel"),
        name="merge_out",
    )(x, gates, ya, yb, wa, wb, wo)


def _mem_kv_kernel(mem_ref, g_ref, wkv_ref, kn_ref, k_ref, v_ref):
    dm = k_ref.shape[2]
    kv = _dot(_rms(mem_ref[0], g_ref[...]).astype(BF16), wkv_ref[...])
    k = kv[:, 0:dm]
    ss = _dot_x2(k * k, _head_ones(dm, MEM_HEAD_DIM)) * (1.0 / MEM_HEAD_DIM)
    k_ref[0] = k * lax.rsqrt(ss + NORM_EPS) * kn_ref[...]
    v_ref[0] = kv[:, dm:2 * dm]


def _mem_kv(mem, g, wkv, k_norm):
    b, m, d = mem.shape
    dm = MEM_HEADS * MEM_HEAD_DIM
    kn = jnp.tile(k_norm, MEM_HEADS).reshape(1, dm)
    spec = pl.BlockSpec((1, m, dm), lambda i: (i, 0, 0))
    return pl.pallas_call(
        _mem_kv_kernel,
        grid=(b,),
        in_specs=[pl.BlockSpec((1, m, d), lambda i: (i, 0, 0)), _full((1, d)), _full(wkv.shape), _full((1, dm))],
        out_specs=(spec, spec),
        out_shape=(jax.ShapeDtypeStruct((b, m, dm), F32),) * 2,
        compiler_params=_params("parallel"),
        name="mem_kv",
    )(mem, g.reshape(1, d), wkv, kn)


def _mem_attn_kernel(x_ref, g_ref, wq_ref, qn_ref, k_ref, v_ref, wo_ref, o_ref):
    x = x_ref[0]
    dm = wq_ref.shape[1]
    q = _dot(_rms(x, g_ref[...]).astype(BF16), wq_ref[...])
    ss = _dot_x2(q * q, _head_ones(dm, MEM_HEAD_DIM)) * (1.0 / MEM_HEAD_DIM)
    q = q * lax.rsqrt(ss + NORM_EPS) * qn_ref[...] * (MEM_HEAD_DIM ** -0.5)
    outs = []
    for h in range(MEM_HEADS):
        sl = slice(h * MEM_HEAD_DIM, (h + 1) * MEM_HEAD_DIM)
        s = _dot_nt(q[:, sl].astype(BF16), k_ref[0, :, sl].astype(BF16))
        p = jnp.exp(s - jnp.max(s, axis=1, keepdims=True))
        p = p / jnp.sum(p, axis=1, keepdims=True)
        outs.append(_dot(p.astype(BF16), v_ref[0, :, sl].astype(BF16)))
    o = jnp.concatenate(outs, axis=1)
    o_ref[0] = x + _bdot(o, wo_ref[...])


def _mem_attention(x, g, wq, q_norm, k, v, wo):
    b, s, d = x.shape
    m, dm = k.shape[1], k.shape[2]
    tm = min(ROW_TILE, s)
    qn = jnp.tile(q_norm, MEM_HEADS).reshape(1, dm)
    tok = pl.BlockSpec((1, tm, d), lambda i, j: (i, j, 0))
    kv = pl.BlockSpec((1, m, dm), lambda i, j: (i, 0, 0))
    return pl.pallas_call(
        _mem_attn_kernel,
        grid=(b, s // tm),
        in_specs=[tok, _full((1, d)), _full(wq.shape), _full((1, dm)), kv, kv, _full(wo.shape)],
        out_specs=tok,
        out_shape=jax.ShapeDtypeStruct((b, s, d), F32),
        compiler_params=_params("parallel", "parallel"),
        name="mem_attention",
    )(x, g.reshape(1, d), wq, qn, k, v, wo)


def _silu(x):
    return x * _sigmoid(x)


def _ffn_kernel(x_ref, g_ref, wg_ref, wu_ref, wd_ref, o_ref, h_ref):
    j = pl.program_id(1)

    @pl.when(j == 0)
    def _():
        x = x_ref[...]
        h_ref[...] = _rms(x, g_ref[...]).astype(BF16)
        o_ref[...] = x

    h = h_ref[...]
    act = _silu(_dot(h, wg_ref[...])) * _dot(h, wu_ref[...])
    o_ref[...] += _bdot(act, wd_ref[...])


def _ffn(x, g, wg, wu, wd, n_f=2):
    t, d = x.shape
    f = wg.shape[1]
    tm = min(FFN_ROW_TILE, t)
    tf = f // n_f
    tok = pl.BlockSpec((tm, d), lambda i, j: (i, 0))
    return pl.pallas_call(
        _ffn_kernel,
        grid=(t // tm, n_f),
        in_specs=[tok, _full((1, d)), pl.BlockSpec((d, tf), lambda i, j: (0, j)),
                  pl.BlockSpec((d, tf), lambda i, j: (0, j)), pl.BlockSpec((tf, d), lambda i, j: (j, 0))],
        out_specs=tok,
        out_shape=jax.ShapeDtypeStruct((t, d), F32),
        scratch_shapes=[pltpu.VMEM((tm, d), BF16)],
        compiler_params=_params("parallel", "arbitrary"),
        name="ffn_swiglu",
    )(x, g.reshape(1, d), wg, wu, wd)


def _moe_kernel(x_ref, g_ref, rh_ref, rl_ref, b_ref, wg_ref, wu_ref, wd_ref, o_ref, h_ref, gate_ref, rank_ref,
                gate_t_ref, rank_t_ref):
    e = pl.program_id(1)
    tm = x_ref.shape[0]

    @pl.when(e == 0)
    def _():
        x = x_ref[...]
        hn = _rms(x, g_ref[...])
        h_ref[...] = hn.astype(BF16)
        o_ref[...] = x
        logits = _dot_x3(hn, rh_ref[...], rl_ref[...]) + b_ref[...]
        lane = lax.broadcasted_iota(jnp.int32, logits.shape, 1)
        logits = jnp.where(lane < N_EXPERTS, logits, -jnp.inf)
        m1 = jnp.max(logits, axis=1, keepdims=True)
        i1 = jnp.min(jnp.where(logits == m1, lane, 128), axis=1, keepdims=True)
        rest = jnp.where(lane == i1, -jnp.inf, logits)
        m2 = jnp.max(rest, axis=1, keepdims=True)
        i2 = jnp.min(jnp.where(rest == m2, lane, 128), axis=1, keepdims=True)
        e2 = jnp.exp(m2 - m1)
        w1 = 1.0 / (1.0 + e2)
        w2 = e2 / (1.0 + e2)
        gates = jnp.where(lane == i1, w1, 0.0) + jnp.where(lane == i2, w2, 0.0)
        gate_ref[...] = gates
        member = jnp.where(gates != 0.0, 1.0, 0.0).astype(BF16)
        earlier = jnp.where(lax.broadcasted_iota(jnp.int32, (tm, tm), 1)
                            < lax.broadcasted_iota(jnp.int32, (tm, tm), 0), 1.0, 0.0).astype(BF16)
        rank = _dot(earlier, member)
        rank_ref[...] = rank
        gate_t_ref[...] = gates.T
        rank_t_ref[...] = rank.T

    lane = lax.broadcasted_iota(jnp.int32, gate_ref.shape, 1)
    gate_col = jnp.sum(jnp.where(lane == e, gate_ref[...], 0.0), axis=1, keepdims=True)
    rank_col = jnp.sum(jnp.where(lane == e, rank_ref[...], 0.0), axis=1, keepdims=True)
    gate_row = gate_t_ref[pl.ds(e, 1), :]
    rank_row = rank_t_ref[pl.ds(e, 1), :]
    n_tok = jnp.sum(jnp.where(gate_col != 0.0, 1.0, 0.0)).astype(jnp.int32)
    sub = min(MOE_SUB, tm)

    def sub_tile(s_idx, carry):
        base = (s_idx * sub).astype(F32)
        slot_col = base + lax.broadcasted_iota(jnp.int32, (sub, 1), 0).astype(F32)
        slot_row = base + lax.broadcasted_iota(jnp.int32, (1, sub), 1).astype(F32)
        pick = (gate_row != 0.0) & (rank_row == slot_col)
        place = (gate_col != 0.0) & (rank_col == slot_row)
        hs = _dot(jnp.where(pick, 1.0, 0.0).astype(BF16), h_ref[...]).astype(BF16)
        act = _silu(_dot(hs, wg_ref[0])) * _dot(hs, wu_ref[0])
        out = _bdot(act, wd_ref[0]) * jnp.sum(jnp.where(pick, gate_row, 0.0), axis=1, keepdims=True)
        o_ref[...] += _dot(jnp.where(place, 1.0, 0.0).astype(BF16), out.astype(BF16))
        return carry

    lax.fori_loop(0, (n_tok + sub - 1) // sub, sub_tile, 0)


def _moe(x, g, router, bias, wg, wu, wd):
    t, d = x.shape
    n_e, _, f = wg.shape
    tm = min(FFN_ROW_TILE, t)
    r_pad = jnp.zeros((d, 128), F32).at[:, 0:n_e].set(router)
    b_pad = jnp.zeros((1, 128), F32).at[0, 0:n_e].set(bias)
    tok = pl.BlockSpec((tm, d), lambda i, j: (i, 0))
    return pl.pallas_call(
        _moe_kernel,
        grid=(t // tm, n_e),
        in_specs=[tok, _full((1, d)), _full((d, 128)), _full((d, 128)), _full((1, 128)),
                  pl.BlockSpec((1, d, f), lambda i, j: (j, 0, 0)), pl.BlockSpec((1, d, f), lambda i, j: (j, 0, 0)),
                  pl.BlockSpec((1, f, d), lambda i, j: (j, 0, 0))],
        out_specs=tok,
        out_shape=jax.ShapeDtypeStruct((t, d), F32),
        scratch_shapes=[pltpu.VMEM((tm, d), BF16), pltpu.VMEM((tm, 128), F32), pltpu.VMEM((tm, 128), F32),
                        pltpu.VMEM((128, tm), F32), pltpu.VMEM((128, tm), F32)],
        compiler_params=_params("parallel", "arbitrary"),
        name="moe_swiglu",
    )(x, g.reshape(1, d), *_split(r_pad), b_pad, wg, wu, wd)


def kernel(x, mem, positions, norm_mix, w_in, rwkv_mu, rwkv_w0, rwkv_w2, rwkv_a0, rwkv_a2, rwkv_g2, rwkv_v0, rwkv_v1, rwkv_v2, rwkv_kk, rwkv_ka, rwkv_rk, rwkv_lnx_g, rwkv_lnx_b, dsa_q_norm, dsa_k_norm, idx_k_norm, w_branch_a, w_branch_b, w_out, norm_mem, mem_tok_norm, mem_wq, mem_wkv, mem_q_norm, mem_k_norm, mem_wo, norm_ffn, ffn_wg, ffn_wu, ffn_wd, moe_router, moe_bias, moe_wg, moe_wu, moe_wd):
    w = dict(rwkv_mu=rwkv_mu, rwkv_w0=rwkv_w0, rwkv_w2=rwkv_w2, rwkv_a0=rwkv_a0, rwkv_a2=rwkv_a2,
             rwkv_g2=rwkv_g2, rwkv_v0=rwkv_v0, rwkv_v1=rwkv_v1, rwkv_v2=rwkv_v2, rwkv_kk=rwkv_kk,
             rwkv_ka=rwkv_ka, rwkv_rk=rwkv_rk, rwkv_lnx_g=rwkv_lnx_g, rwkv_lnx_b=rwkv_lnx_b)
    b, s, d = x.shape
    t = b * s
    depth = w_in.shape[0]
    n_a = w_in.shape[2] - 1224 - 2 * d
    bf = lambda a: a.astype(BF16)
    cos, sin = _rope_tables(positions)
    x = x.reshape(t, d)
    v_first = None
    for l in range(depth):
        w_a = bf(w_in[l, :, 0:n_a])
        w_b = bf(_dsa_weight(w_in[l, :, n_a:n_a + 1224]))
        w_g = bf(w_in[l, :, n_a + 1224:])
        cols_a, cols_b, gates = _norm_matmul(x, norm_mix[l], (w_a, w_b, w_g))
        y_a, v_first = _rwkv_branch(cols_a.reshape(b, s, n_a), v_first, _rwkv_params(w, l))
        q, iq, kik, vw = _dsa_prep(cols_b, cos, sin, dsa_q_norm[l], dsa_k_norm[l], idx_k_norm[l])
        y_b = _dsa_attention(q, iq, kik, vw, b, s)
        x = _merge(x, gates, y_a.reshape(t, MIX_DIM), y_b.reshape(t, MIX_DIM),
                   bf(w_branch_a[l]), bf(w_branch_b[l]), bf(w_out[l]))
        mk, mv = _mem_kv(mem, mem_tok_norm[l], bf(mem_wkv[l]), mem_k_norm[l])
        x = _mem_attention(x.reshape(b, s, d), norm_mem[l], bf(mem_wq[l]), mem_q_norm[l], mk, mv,
                           bf(mem_wo[l])).reshape(t, d)
        j = l // 2
        if l % 2 == 0:
            x = _ffn(x, norm_ffn[l], bf(ffn_wg[j]), bf(ffn_wu[j]), bf(ffn_wd[j]))
        else:
            x = _moe(x, norm_ffn[l], moe_router[j], moe_bias[j], bf(moe_wg[j]), bf(moe_wu[j]), bf(moe_wd[j]))
    return x.reshape(b, s, d)
```

```python
import functools

import jax
import jax.numpy as jnp
import numpy as np
from jax import lax
from jax.experimental import pallas as pl
from jax.experimental.pallas import tpu as pltpu

F32 = jnp.float32
BF16 = jnp.bfloat16
HIGHEST = lax.Precision.HIGHEST

NORM_EPS = 1e-6
NEG_INF = -1e30
ROPE_THETA = 10000.0
LOG2_E = float(np.log2(np.e))

CHUNK = 64
Q_BLOCK = 128
DSA_TOPK_MAX = 256
DSA_KEY_CLASSES = 8
DSA_ROW_GROUPS = 4
DSA_RADIX4_MAX_KEYS = 768
HEAD_DIM = 64
N_HEADS = 8
MIX_DIM = N_HEADS * HEAD_DIM
RWKV_LNX_EPS = 1e-5 * HEAD_DIM
MEM_HEADS = 4
MEM_HEAD_DIM = 128
N_EXPERTS = 8

RWKV_CHUNK = 64
RWKV_BLOCK = 256
ROW_TILE = 512
WIDE_ROW_TILE = 1024
FFN_ROW_TILE = 1024
MOE_SUB = 288
VMEM_LIMIT = 56 * 1024 * 1024


def _params(*sem):
    return pltpu.CompilerParams(dimension_semantics=sem, vmem_limit_bytes=VMEM_LIMIT)


def _full(shape):
    nd = len(shape)
    return pl.BlockSpec(shape, lambda *_: (0,) * nd)


def _dot(a, b, precision=None):
    return jnp.dot(a, b, preferred_element_type=F32, precision=precision)


def _dot_nt(a, b, precision=None):
    return lax.dot_general(a, b, (((1,), (1,)), ((), ())), preferred_element_type=F32,
                           precision=precision)


def _dot_tn(a, b, precision=None):
    return lax.dot_general(a, b, (((0,), (0,)), ((), ())), preferred_element_type=F32,
                           precision=precision)


def _bdot(a, b):
    return _dot(a.astype(BF16), b.astype(BF16))


def _split(x):
    hi = x.astype(BF16)
    return hi, (x - hi.astype(F32)).astype(BF16)


def _dot_x2(x, m):
    hi, lo = _split(x)
    return _dot(hi, m) + _dot(lo, m)


def _dot_x3(x, w_hi, w_lo):
    hi, lo = _split(x)
    return _dot(hi, w_hi) + (_dot(lo, w_hi) + _dot(hi, w_lo))


def _rms(x, g):
    return x * lax.rsqrt(jnp.mean(x * x, axis=-1, keepdims=True) + NORM_EPS) * g


def _sigmoid(x):
    return 1.0 / (1.0 + jnp.exp(-x))


def _head_ones(width, head):
    r = lax.broadcasted_iota(jnp.int32, (width, width), 0) // head
    c = lax.broadcasted_iota(jnp.int32, (width, width), 1) // head
    return (r == c).astype(BF16)


def _norm_matmul_kernel(x_ref, g_ref, *refs):
    n = len(refs) // 2
    h = _rms(x_ref[...], g_ref[...]).astype(BF16)
    for w_ref, o_ref in zip(refs[:n], refs[n:]):
        o_ref[...] = _dot(h, w_ref[...])


def _norm_matmul(x, g, ws, tm=ROW_TILE):
    t, d = x.shape
    tm = min(tm, t)
    return pl.pallas_call(
        _norm_matmul_kernel,
        grid=(t // tm,),
        in_specs=[pl.BlockSpec((tm, d), lambda i: (i, 0)), _full((1, d))] + [_full(w.shape) for w in ws],
        out_specs=tuple(pl.BlockSpec((tm, w.shape[1]), lambda i: (i, 0)) for w in ws),
        out_shape=tuple(jax.ShapeDtypeStruct((t, w.shape[1]), F32) for w in ws),
        compiler_params=_params("parallel"),
        name="norm_matmul",
    )(x, g.reshape(1, d), *ws)


def _rwkv_kernel(has_vmix, *refs):
    if has_vmix:
        (cols_ref, vf_ref, mu_ref, w0_ref, a0_ref, wah_ref, wal_ref, g2_ref, kk_ref, ka_ref,
         rk_ref, lg_ref, lb_ref, v0_ref, v1_ref, v2_ref,
         y_ref, st_ref, prev_ref, at_ref, rt_ref, bt_ref, kt_ref, bh_ref, kh_ref, vv_ref,
         gdec_ref, bonus_ref, gate_ref, yy_ref, y0_ref, p_ref, rb_ref, z_ref, rhs_ref, dl_ref) = refs
        vout_ref = None
    else:
        (cols_ref, mu_ref, w0_ref, a0_ref, wah_ref, wal_ref, g2_ref, kk_ref, ka_ref,
         rk_ref, lg_ref, lb_ref,
         y_ref, vout_ref, st_ref, prev_ref, at_ref, rt_ref, bt_ref, kt_ref, bh_ref, kh_ref, vv_ref,
         gdec_ref, bonus_ref, gate_ref, yy_ref, y0_ref, p_ref, rb_ref, z_ref, rhs_ref, dl_ref) = refs

    tb = cols_ref.shape[1]
    c = RWKV_CHUNK
    n_chunks = tb // c
    hd = HEAD_DIM

    @pl.when(pl.program_id(1) == 0)
    def _():
        st_ref[...] = jnp.zeros_like(st_ref)
        prev_ref[...] = jnp.zeros_like(prev_ref)

    cols = cols_ref[0]
    row = lax.broadcasted_iota(jnp.int32, cols.shape, 0)
    shifted = jnp.where(row == 0, prev_ref[...], pltpu.roll(cols, 1, 0))
    prev_ref[...] = cols[tb - 1:tb, :]
    mixed = cols + (shifted - cols) * mu_ref[...]

    r = mixed[:, 0:MIX_DIM]
    k = mixed[:, MIX_DIM:2 * MIX_DIM]
    v = mixed[:, 2 * MIX_DIM:3 * MIX_DIM]
    wa = mixed[:, 3 * MIX_DIM:3 * MIX_DIM + 128]
    gd = mixed[:, 3 * MIX_DIM + 128:3 * MIX_DIM + 256]
    lane = lax.broadcasted_iota(jnp.int32, wa.shape, 1)
    wa = jnp.where(lane < 64, jnp.tanh(wa), wa)
    wa_out = _dot_x3(wa, wah_ref[...], wal_ref[...])
    u_dec = w0_ref[...] + wa_out[:, 0:MIX_DIM]
    a = _sigmoid(a0_ref[...] + wa_out[:, MIX_DIM:2 * MIX_DIM])
    gate_ref[...] = _bdot(_sigmoid(gd), g2_ref[...])
    if has_vmix:
        vm = _bdot(_bdot(v, v1_ref[...]), v2_ref[...])
        v = v + (vf_ref[0] - v) * _sigmoid(v0_ref[...] + vm)
    else:
        vout_ref[0] = v

    hb = _head_ones(MIX_DIM, hd)
    kk = k * kk_ref[...]
    kk = kk / jnp.maximum(jnp.sqrt(_dot_x2(kk * kk, hb)), 1e-12)
    k = k * (1.0 + (a - 1.0) * ka_ref[...])
    bonus_ref[...] = _bdot(r * k * rk_ref[...], hb) * v
    ll = -_sigmoid(u_dec) * float(np.exp(-0.5))

    ri = lax.broadcasted_iota(jnp.int32, (tb, tb), 0)
    ci = lax.broadcasted_iota(jnp.int32, (tb, tb), 1)
    same = (ri // c) == (ci // c)
    strict_u = (same & (ri < ci)).astype(F32)
    incl_u = (same & (ri <= ci)).astype(F32)
    ll_hi, ll_lo = _split(ll)
    tri = (same & (ri >= ci)).astype(BF16)
    cum = _dot(tri, ll_hi) + _dot(tri, ll_lo)
    tot_rows = [cum[(ch + 1) * c - 1:(ch + 1) * c, :] for ch in range(n_chunks)]
    tot = jnp.concatenate([jnp.broadcast_to(t_row, (c, MIX_DIM)) for t_row in tot_rows], axis=0)
    for ch in range(n_chunks):
        gdec_ref[ch:ch + 1, :] = jnp.exp(tot_rows[ch])
    g_inv = jnp.exp(-cum)
    g_tail = jnp.exp(tot - cum)
    beta = kk * a
    at_ref[...] = (-kk * jnp.exp(cum - ll)).T.astype(BF16)
    rt_ref[...] = (r * jnp.exp(cum)).T.astype(BF16)
    vv_ref[...] = v.T.astype(BF16)
    bt_ref[...] = (beta * g_inv).astype(BF16)
    kt_ref[...] = (k * g_inv).astype(BF16)
    bh_ref[...] = (beta * g_tail).astype(BF16)
    kh_ref[...] = (k * g_tail).astype(BF16)

    n_levels = int(np.log2(c))
    heads = [slice(h * hd, (h + 1) * hd) for h in range(N_HEADS)]

    for h, cs in enumerate(heads):
        b_t = bt_ref[:, cs]
        k_t = kt_ref[:, cs]
        a_tr = at_ref[cs, :]
        r_tr = rt_ref[cs, :]
        v_tr = vv_ref[cs, :]
        p_ref[h] = (_dot(b_t, a_tr) * strict_u).astype(BF16)
        m_tr = (_dot(k_t, a_tr) * strict_u).astype(BF16)
        rb_ref[h] = (_dot(b_t, r_tr) * incl_u).astype(BF16)
        rk_tr = (_dot(k_t, r_tr) * incl_u).astype(BF16)
        y0_ref[cs, :] = _dot(v_tr, rk_tr)
        z_ref[h] = jnp.concatenate([a_tr.astype(F32), _dot(v_tr, m_tr)], axis=0)

    for lev in range(n_levels):
        for h in range(N_HEADS):
            p = p_ref[h]
            z = z_ref[h]
            z_ref[h] = z + _dot(z.astype(BF16), p)
            if lev < n_levels - 1:
                p_ref[h] = _dot(p, p).astype(BF16)

    for h, cs in enumerate(heads):
        z_b = z_ref[h].astype(BF16)
        w = _dot(z_b, rb_ref[h])
        r_bar = (rt_ref[cs, :].astype(F32) + w[0:hd]).astype(BF16)
        y0_ref[cs, :] = y0_ref[cs, :] + w[hd:2 * hd]
        for ch in range(n_chunks):
            cols = slice(ch * c, (ch + 1) * c)
            pd = _dot(z_b[:, cols], bh_ref[cols, cs])
            d2 = _dot(vv_ref[cs, cols], kh_ref[cols, cs])
            rhs_ref[h, ch] = jnp.concatenate([pd[0:hd].astype(BF16), r_bar[:, cols]], axis=1)
            dl_ref[h, ch] = pd[hd:2 * hd] + d2

    for ch in range(n_chunks):
        cols = slice(ch * c, (ch + 1) * c)
        for h, cs in enumerate(heads):
            st = st_ref[h]
            res = _dot(st.astype(BF16), rhs_ref[h, ch])
            yy_ref[cs, cols] = res[:, hd:hd + c] + y0_ref[cs, cols]
            st_ref[h] = st * gdec_ref[ch:ch + 1, cs] + res[:, 0:hd] + dl_ref[h, ch]

    y = yy_ref[...].T
    mean = _bdot(y, hb) * (1.0 / hd)
    yc = y - mean
    var = _bdot(yc * yc, hb) * (1.0 / hd)
    yn = yc * lax.rsqrt(var + RWKV_LNX_EPS) * lg_ref[...] + lb_ref[...]
    y_ref[0] = ((yn + bonus_ref[...]) * gate_ref[...]).astype(y_ref.dtype)


def _rwkv_params(w, l):
    p = {k: w["rwkv_" + k][l] for k in ("mu", "w0", "w2", "a0", "a2", "g2", "kk", "ka", "rk", "lnx_g", "lnx_b")}
    if l > 0:
        p.update({k: w["rwkv_" + k][l - 1] for k in ("v0", "v1", "v2")})
    return p


def _rwkv_branch(cols, v_first, p):
    b, s, nc = cols.shape
    tb = min(RWKV_BLOCK, s)
    has_vmix = v_first is not None
    row = lambda a: a.reshape(1, -1)
    pair = lambda a: list(_split(a))
    tok = lambda w: pl.BlockSpec((1, tb, w), lambda i, j: (i, j, 0))
    wa2 = jnp.zeros((128, 2 * MIX_DIM), F32)
    wa2 = wa2.at[0:64, 0:MIX_DIM].set(p["w2"]).at[64:128, MIX_DIM:].set(p["a2"])
    ins = [cols] + ([v_first] if has_vmix else []) + [
        row(p["mu"]), row(p["w0"]), row(p["a0"])] + pair(wa2) + [p["g2"].astype(BF16)] + [
        row(p["kk"]), row(p["ka"]), row(p["rk"]), row(p["lnx_g"]), row(p["lnx_b"])]
    if has_vmix:
        ins += [row(p["v0"]), p["v1"].astype(BF16), p["v2"].astype(BF16)]
    in_specs = [tok(nc)] + ([tok(MIX_DIM)] if has_vmix else []) + [_full(a.shape) for a in ins[1 + has_vmix:]]
    y_shape = jax.ShapeDtypeStruct((b, s, MIX_DIM), F32)
    out_shape = y_shape if has_vmix else (y_shape, y_shape)
    out_specs = tok(MIX_DIM) if has_vmix else (tok(MIX_DIM), tok(MIX_DIM))
    n_ch = tb // RWKV_CHUNK
    tok_bf = pltpu.VMEM((tb, MIX_DIM), BF16)
    chan_bf = pltpu.VMEM((MIX_DIM, tb), BF16)
    tok_f32 = pltpu.VMEM((tb, MIX_DIM), F32)
    chan_f32 = pltpu.VMEM((MIX_DIM, tb), F32)
    scratch = [pltpu.VMEM((N_HEADS, HEAD_DIM, HEAD_DIM), F32), pltpu.VMEM((1, nc), F32),
               chan_bf, chan_bf, tok_bf, tok_bf, tok_bf, tok_bf, chan_bf,
               pltpu.VMEM((8, MIX_DIM), F32), tok_f32, tok_f32, chan_f32, chan_f32,
               pltpu.VMEM((N_HEADS, tb, tb), BF16), pltpu.VMEM((N_HEADS, tb, tb), BF16),
               pltpu.VMEM((N_HEADS, 2 * HEAD_DIM, tb), F32),
               pltpu.VMEM((N_HEADS, n_ch, HEAD_DIM, 2 * RWKV_CHUNK), BF16),
               pltpu.VMEM((N_HEADS, n_ch, HEAD_DIM, HEAD_DIM), F32)]
    out = pl.pallas_call(
        functools.partial(_rwkv_kernel, has_vmix),
        grid=(b, s // tb),
        in_specs=in_specs,
        out_specs=out_specs,
        out_shape=out_shape,
        scratch_shapes=scratch,
        compiler_params=_params("parallel", "arbitrary"),
        name="rwkv7_vmix" if has_vmix else "rwkv7_first",
    )(*ins)
    return (out, v_first) if has_vmix else out


DSA_COLS_PAD = 1280
_DSA_Q, _DSA_IQ, _DSA_KIK, _DSA_VW = 0, 512, 1024, 1152


def _rope_table_kernel(pos_ref, freq_ref, cos_ref, sin_ref):
    ang = pos_ref[...] * freq_ref[...]
    lane = lax.broadcasted_iota(jnp.int32, ang.shape, 1)
    cos_ref[...] = jnp.cos(ang)
    sin_ref[...] = jnp.where(lane % HEAD_DIM < HEAD_DIM // 2, -jnp.sin(ang), jnp.sin(ang))


def _rope_tables(positions):
    t = positions.size
    tm = min(WIDE_ROW_TILE, t)
    inv_freq = 1.0 / (ROPE_THETA ** (jnp.arange(0, HEAD_DIM, 2, dtype=F32) / HEAD_DIM))
    freq = jnp.tile(inv_freq, 4).reshape(1, 128)
    pos = positions.reshape(t, 1).astype(F32)
    spec = pl.BlockSpec((tm, 128), lambda i: (i, 0))
    return pl.pallas_call(
        _rope_table_kernel,
        grid=(t // tm,),
        in_specs=[pl.BlockSpec((tm, 1), lambda i: (i, 0)), _full((1, 128))],
        out_specs=(spec, spec),
        out_shape=(jax.ShapeDtypeStruct((t, 128), F32),) * 2,
        compiler_params=_params("parallel"),
        name="rope_tables",
    )(pos, freq)


def _rope128(x, cos, sin):
    lane = lax.broadcasted_iota(jnp.int32, x.shape, 1)
    half = HEAD_DIM // 2
    rot = jnp.where(lane % HEAD_DIM < half, pltpu.roll(x, 128 - half, 1), pltpu.roll(x, half, 1))
    return x * cos + rot * sin


def _dsa_prep_kernel(cols_ref, cos_ref, sin_ref, qn_ref, kn_ref, q_ref, iq_ref, kik_ref, vw_ref):
    cos, sin = cos_ref[...], sin_ref[...]
    q = cols_ref[:, _DSA_Q:_DSA_Q + MIX_DIM]
    ss = _dot_x2(q * q, _head_ones(MIX_DIM, HEAD_DIM)) * (1.0 / HEAD_DIM)
    q = q * lax.rsqrt(ss + NORM_EPS) * qn_ref[...]
    kik = cols_ref[:, _DSA_KIK:_DSA_KIK + 128]
    ss = _dot_x2(kik * kik, _head_ones(128, HEAD_DIM)) * (1.0 / HEAD_DIM)
    kik_ref[...] = _rope128(kik * lax.rsqrt(ss + NORM_EPS) * kn_ref[...], cos, sin)
    for j in range(MIX_DIM // 128):
        sl = slice(j * 128, (j + 1) * 128)
        q_ref[j] = _rope128(q[:, sl], cos, sin) * (HEAD_DIM ** -0.5 * LOG2_E)
        iq_ref[:, sl] = _rope128(cols_ref[:, _DSA_IQ + j * 128:_DSA_IQ + (j + 1) * 128], cos, sin) * (HEAD_DIM ** -0.5)
    vw = cols_ref[:, _DSA_VW:_DSA_VW + 128]
    vw_ref[...] = jnp.where(lax.broadcasted_iota(jnp.int32, vw.shape, 1) == 127, 1.0, vw)


def _dsa_prep(cols, cos, sin, q_norm, k_norm, idx_k_norm):
    t = cols.shape[0]
    tm = min(WIDE_ROW_TILE, t)
    qn = jnp.tile(q_norm, N_HEADS).reshape(1, MIX_DIM)
    kn = jnp.concatenate([k_norm, idx_k_norm]).reshape(1, 128)
    tok = lambda w: pl.BlockSpec((tm, w), lambda i: (i, 0))
    sds = lambda w: jax.ShapeDtypeStruct((t, w), F32)
    return pl.pallas_call(
        _dsa_prep_kernel,
        grid=(t // tm,),
        in_specs=[tok(DSA_COLS_PAD), tok(128), tok(128), _full((1, MIX_DIM)), _full((1, 128))],
        out_specs=(pl.BlockSpec((N_HEADS // 2, tm, 128), lambda i: (0, i, 0)), tok(MIX_DIM), tok(128), tok(128)),
        out_shape=(jax.ShapeDtypeStruct((N_HEADS // 2, t, 128), F32), sds(MIX_DIM), sds(128), sds(128)),
        compiler_params=_params("parallel"),
        name="dsa_prep",
    )(cols, cos, sin, qn, kn)


def _count(mask):
    return jnp.sum(jnp.where(mask, 1.0, 0.0), axis=1, keepdims=True)


def _dsa_block(top_k, n_keys, q_ref, iq_ref, vwq_ref, kik_ref, vw_ref, o_ref, key_ref, s_ref):
    qb = iq_ref.shape[1]
    start = pl.program_id(1) * qb
    k = kik_ref[0, 0:n_keys, 0:HEAD_DIM].astype(BF16)
    ik = kik_ref[0, 0:n_keys, HEAD_DIM:2 * HEAD_DIM].astype(BF16)
    v_ones = vw_ref[0, 0:n_keys, :].astype(BF16)
    iw = vwq_ref[0, :, HEAD_DIM:HEAD_DIM + N_HEADS] * (N_HEADS ** -0.5)

    score = jnp.zeros((qb, n_keys), F32)
    for h in range(N_HEADS):
        lg = _dot_nt(iq_ref[0, :, h * HEAD_DIM:(h + 1) * HEAD_DIM].astype(BF16), ik)
        score = score + jnp.maximum(lg, 0.0) * iw[:, h:h + 1]
    key_pos = lax.broadcasted_iota(jnp.int32, (qb, n_keys), 1)
    q_pos = start + lax.broadcasted_iota(jnp.int32, (qb, 1), 0)
    limit = (q_pos // CHUNK + 1) * CHUNK
    adm = key_pos < limit
    score = jnp.where(adm, score, NEG_INF)
    score = jnp.where(score == 0.0, 0.0, score)

    bits = pltpu.bitcast(score, jnp.int32)
    key = bits ^ ((bits >> 31) & jnp.int32(0x7FFFFFFF))
    kf = float(top_k)
    key_ref[:, 0:n_keys] = key
    gr = qb // DSA_ROW_GROUPS
    digit_bits = 2 if n_keys <= DSA_RADIX4_MAX_KEYS else 1
    n_steps = 32 // digit_bits

    def theta_body(i, ths):
        unit = jnp.int32(1) << (32 - digit_bits - digit_bits * i)
        out = []
        for g in range(DSA_ROW_GROUPS):
            keys = key_ref[g * gr:(g + 1) * gr, 0:n_keys]
            passed = jnp.zeros((gr, 1), jnp.int32)
            for d in range(1, 2 ** digit_bits):
                cnt = _count(keys >= ths[g] + unit * d)
                passed = passed + jnp.where(cnt >= kf, 1, 0)
            out.append(ths[g] + unit * passed)
        return tuple(out)

    n_pairs = N_HEADS // 2
    per_pair = n_steps // n_pairs

    def pair_step(hp, ths):
        q2 = q_ref[hp].astype(BF16)
        s_ref[2 * hp, :, 0:n_keys] = _dot_nt(q2[:, 0:HEAD_DIM], k)
        s_ref[2 * hp + 1, :, 0:n_keys] = _dot_nt(q2[:, HEAD_DIM:2 * HEAD_DIM], k)
        for j in range(per_pair):
            ths = theta_body(hp * per_pair + j, ths)
        return ths

    lowest = jnp.full((gr, 1), -2 ** 31, jnp.int32)
    ths = lax.fori_loop(0, n_pairs, pair_step, (lowest,) * DSA_ROW_GROUPS)
    theta = jnp.concatenate(ths, axis=0)
    gt = key > theta
    eq = key == theta
    n_gt = _count(gt)
    need = kf - n_gt
    n_bits = int(np.ceil(np.log2(n_keys))) + 1

    def tie_search():
        def tie_body(i, jb):
            cand = jb + (jnp.int32(1) << (n_bits - 1 - i))
            ok = (cand <= n_keys) & (_count(eq & (key_pos < cand)) <= need)
            return jnp.where(ok, cand, jb)

        return lax.fori_loop(0, n_bits, tie_body, jnp.zeros((qb, 1), jnp.int32))

    has_tie = jnp.max(n_gt + _count(eq & adm)) > kf
    bound = lax.cond(has_tie, tie_search, lambda: jnp.full((qb, 1), n_keys, jnp.int32))
    bias = jnp.where((gt | (eq & (key_pos < bound))) & adm, 0.0, NEG_INF)

    tops = []
    for h in range(N_HEADS):
        s = s_ref[h, :, 0:n_keys] + bias
        s_ref[h, :, 0:n_keys] = s
        tops.append(jnp.max(s, axis=1, keepdims=True))
    for h in range(N_HEADS):
        p = jnp.exp2(s_ref[h, :, 0:n_keys] - tops[h])
        o = _dot(p.astype(BF16), v_ones)
        o_ref[0, :, h * HEAD_DIM:(h + 1) * HEAD_DIM] = o[:, 0:HEAD_DIM] / o[:, 127:128]


def _dsa_attn_kernel(top_k, n_classes, q_ref, iq_ref, vwq_ref, kik_ref, vw_ref, o_ref, key_ref, s_ref):
    qb = iq_ref.shape[1]
    per_class = (kik_ref.shape[1] // qb) // n_classes
    cls = pl.program_id(1) // per_class
    for n in range(n_classes):
        pl.when(cls == n)(functools.partial(
            _dsa_block, top_k, (n + 1) * per_class * qb, q_ref, iq_ref, vwq_ref, kik_ref, vw_ref, o_ref, key_ref, s_ref))


def _dsa_attention(q, iq, kik, vw, b, s):
    top_k = min(DSA_TOPK_MAX, s // 4)
    qb = min(Q_BLOCK, s)
    n_classes = min(DSA_KEY_CLASSES, s // qb)
    r3 = lambda a: a.reshape(b, s, a.shape[-1])
    blk = lambda w: pl.BlockSpec((1, qb, w), lambda i, j: (i, j, 0))
    seq = lambda w: pl.BlockSpec((1, s, w), lambda i, j: (i, 0, 0))
    q_heads = pl.BlockSpec((N_HEADS // 2, qb, 128), lambda i, j: (0, i * (s // qb) + j, 0))
    return pl.pallas_call(
        functools.partial(_dsa_attn_kernel, top_k, n_classes),
        grid=(b, s // qb),
        in_specs=[q_heads, blk(MIX_DIM), blk(128), seq(128), seq(128)],
        out_specs=blk(MIX_DIM),
        out_shape=jax.ShapeDtypeStruct((b, s, MIX_DIM), F32),
        scratch_shapes=[pltpu.VMEM((qb, s), jnp.int32), pltpu.VMEM((N_HEADS, qb, s), F32)],
        compiler_params=_params("parallel", "arbitrary"),
        name="dsa_attention",
    )(q, r3(iq), r3(vw), r3(kik), r3(vw))


def _dsa_weight(w_dsa):
    d = w_dsa.shape[0]
    q, k, v, iq, ik, iw = jnp.split(w_dsa, np.cumsum([512, 64, 64, 512, 64, 8])[:-1].tolist(), axis=1)
    pad = jnp.zeros((d, DSA_COLS_PAD - 1224), w_dsa.dtype)
    return jnp.concatenate([q, iq, k, ik, v, iw, pad], axis=1)


def _merge_kernel(x_ref, g_ref, ya_ref, yb_ref, wa_ref, wb_ref, wo_ref, o_ref):
    d = x_ref.shape[1]
    ga = _sigmoid(g_ref[:, 0:d])
    gb = _sigmoid(g_ref[:, d:2 * d])
    merged = ga * _bdot(ya_ref[...], wa_ref[...]) + gb * _bdot(yb_ref[...], wb_ref[...])
    o_ref[...] = x_ref[...] + _bdot(merged, wo_ref[...])


def _merge(x, gates, ya, yb, wa, wb, wo):
    t, d = x.shape
    tm = min(WIDE_ROW_TILE, t)
    tok = lambda w: pl.BlockSpec((tm, w), lambda i: (i, 0))
    return pl.pallas_call(
        _merge_kernel,
        grid=(t // tm,),
        in_specs=[tok(d), tok(2 * d), tok(MIX_DIM), tok(MIX_DIM), _full(wa.shape), _full(wb.shape), _full(wo.shape)],
        out_specs=tok(d),
        out_shape=jax.ShapeDtypeStruct((t, d), F32),
        compiler_params=_params("parallel"),
        name="merge_out",
    )(x, gates, ya, yb, wa, wb, wo)


def _mem_kv_kernel(mem_ref, g_ref, wkv_ref, kn_ref, k_ref, v_ref):
    dm = k_ref.shape[2]
    kv = _dot(_rms(mem_ref[0], g_ref[...]).astype(BF16), wkv_ref[...])
    k = kv[:, 0:dm]
    ss = _dot_x2(k * k, _head_ones(dm, MEM_HEAD_DIM)) * (1.0 / MEM_HEAD_DIM)
    k_ref[0] = k * lax.rsqrt(ss + NORM_EPS) * kn_ref[...]
    v_ref[0] = kv[:, dm:2 * dm]


def _mem_kv(mem, g, wkv, k_norm):
    b, m, d = mem.shape
    dm = MEM_HEADS * MEM_HEAD_DIM
    kn = jnp.tile(k_norm, MEM_HEADS).reshape(1, dm)
    spec = pl.BlockSpec((1, m, dm), lambda i: (i, 0, 0))
    return pl.pallas_call(
        _mem_kv_kernel,
        grid=(b,),
        in_specs=[pl.BlockSpec((1, m, d), lambda i: (i, 0, 0)), _full((1, d)), _full(wkv.shape), _full((1, dm))],
        out_specs=(spec, spec),
        out_shape=(jax.ShapeDtypeStruct((b, m, dm), F32),) * 2,
        compiler_params=_params("parallel"),
        name="mem_kv",
    )(mem, g.reshape(1, d), wkv, kn)


def _mem_attn_kernel(x_ref, g_ref, wq_ref, qn_ref, k_ref, v_ref, wo_ref, o_ref):
    x = x_ref[0]
    dm = wq_ref.shape[1]
    q = _dot(_rms(x, g_ref[...]).astype(BF16), wq_ref[...])
    ss = _dot_x2(q * q, _head_ones(dm, MEM_HEAD_DIM)) * (1.0 / MEM_HEAD_DIM)
    q = q * lax.rsqrt(ss + NORM_EPS) * qn_ref[...] * (MEM_HEAD_DIM ** -0.5)
    outs = []
    for h in range(MEM_HEADS):
        sl = slice(h * MEM_HEAD_DIM, (h + 1) * MEM_HEAD_DIM)
        s = _dot_nt(q[:, sl].astype(BF16), k_ref[0, :, sl].astype(BF16))
        p = jnp.exp(s - jnp.max(s, axis=1, keepdims=True))
        p = p / jnp.sum(p, axis=1, keepdims=True)
        outs.append(_dot(p.astype(BF16), v_ref[0, :, sl].astype(BF16)))
    o = jnp.concatenate(outs, axis=1)
    o_ref[0] = x + _bdot(o, wo_ref[...])


def _mem_attention(x, g, wq, q_norm, k, v, wo):
    b, s, d = x.shape
    m, dm = k.shape[1], k.shape[2]
    tm = min(WIDE_ROW_TILE, s)
    qn = jnp.tile(q_norm, MEM_HEADS).reshape(1, dm)
    tok = pl.BlockSpec((1, tm, d), lambda i, j: (i, j, 0))
    kv = pl.BlockSpec((1, m, dm), lambda i, j: (i, 0, 0))
    return pl.pallas_call(
        _mem_attn_kernel,
        grid=(b, s // tm),
        in_specs=[tok, _full((1, d)), _full(wq.shape), _full((1, dm)), kv, kv, _full(wo.shape)],
        out_specs=tok,
        out_shape=jax.ShapeDtypeStruct((b, s, d), F32),
        compiler_params=_params("parallel", "parallel"),
        name="mem_attention",
    )(x, g.reshape(1, d), wq, qn, k, v, wo)


def _silu(x):
    return x * _sigmoid(x)


def _ffn_kernel(x_ref, g_ref, wg_ref, wu_ref, wd_ref, o_ref, h_ref):
    j = pl.program_id(1)

    @pl.when(j == 0)
    def _():
        x = x_ref[...]
        h_ref[...] = _rms(x, g_ref[...]).astype(BF16)
        o_ref[...] = x

    h = h_ref[...]
    act = _silu(_dot(h, wg_ref[...])) * _dot(h, wu_ref[...])
    o_ref[...] += _bdot(act, wd_ref[...])


def _ffn(x, g, wg, wu, wd, n_f=2):
    t, d = x.shape
    f = wg.shape[1]
    tm = min(FFN_ROW_TILE, t)
    tf = f // n_f
    tok = pl.BlockSpec((tm, d), lambda i, j: (i, 0))
    return pl.pallas_call(
        _ffn_kernel,
        grid=(t // tm, n_f),
        in_specs=[tok, _full((1, d)), pl.BlockSpec((d, tf), lambda i, j: (0, j)),
                  pl.BlockSpec((d, tf), lambda i, j: (0, j)), pl.BlockSpec((tf, d), lambda i, j: (j, 0))],
        out_specs=tok,
        out_shape=jax.ShapeDtypeStruct((t, d), F32),
        scratch_shapes=[pltpu.VMEM((tm, d), BF16)],
        compiler_params=_params("parallel", "arbitrary"),
        name="ffn_swiglu",
    )(x, g.reshape(1, d), wg, wu, wd)


def _moe_kernel(x_ref, g_ref, rh_ref, rl_ref, b_ref, wg_ref, wu_ref, wd_ref, o_ref, h_ref, gate_ref, rank_ref,
                gate_t_ref, rank_t_ref):
    e = pl.program_id(1)
    tm = x_ref.shape[0]

    @pl.when(e == 0)
    def _():
        x = x_ref[...]
        hn = _rms(x, g_ref[...])
        h_ref[...] = hn.astype(BF16)
        o_ref[...] = x
        logits = _dot_x3(hn, rh_ref[...], rl_ref[...]) + b_ref[...]
        lane = lax.broadcasted_iota(jnp.int32, logits.shape, 1)
        logits = jnp.where(lane < N_EXPERTS, logits, -jnp.inf)
        m1 = jnp.max(logits, axis=1, keepdims=True)
        i1 = jnp.min(jnp.where(logits == m1, lane, 128), axis=1, keepdims=True)
        rest = jnp.where(lane == i1, -jnp.inf, logits)
        m2 = jnp.max(rest, axis=1, keepdims=True)
        i2 = jnp.min(jnp.where(rest == m2, lane, 128), axis=1, keepdims=True)
        e2 = jnp.exp(m2 - m1)
        w1 = 1.0 / (1.0 + e2)
        w2 = e2 / (1.0 + e2)
        gates = jnp.where(lane == i1, w1, 0.0) + jnp.where(lane == i2, w2, 0.0)
        gate_ref[...] = gates
        member = jnp.where(gates != 0.0, 1.0, 0.0).astype(BF16)
        earlier = jnp.where(lax.broadcasted_iota(jnp.int32, (tm, tm), 1)
                            < lax.broadcasted_iota(jnp.int32, (tm, tm), 0), 1.0, 0.0).astype(BF16)
        rank = _dot(earlier, member)
        rank_ref[...] = rank
        gate_t_ref[...] = gates.T
        rank_t_ref[...] = rank.T

    lane = lax.broadcasted_iota(jnp.int32, gate_ref.shape, 1)
    gate_col = jnp.sum(jnp.where(lane == e, gate_ref[...], 0.0), axis=1, keepdims=True)
    rank_col = jnp.sum(jnp.where(lane == e, rank_ref[...], 0.0), axis=1, keepdims=True)
    gate_row = gate_t_ref[pl.ds(e, 1), :]
    rank_row = rank_t_ref[pl.ds(e, 1), :]
    n_tok = jnp.sum(jnp.where(gate_col != 0.0, 1.0, 0.0)).astype(jnp.int32)
    sub = min(MOE_SUB, tm)

    def sub_tile(s_idx, carry):
        base = (s_idx * sub).astype(F32)
        slot_col = base + lax.broadcasted_iota(jnp.int32, (sub, 1), 0).astype(F32)
        slot_row = base + lax.broadcasted_iota(jnp.int32, (1, sub), 1).astype(F32)
        pick = (gate_row != 0.0) & (rank_row == slot_col)
        place = (gate_col != 0.0) & (rank_col == slot_row)
        hs = _dot(jnp.where(pick, 1.0, 0.0).astype(BF16), h_ref[...]).astype(BF16)
        act = _silu(_dot(hs, wg_ref[0])) * _dot(hs, wu_ref[0])
        out = _bdot(act, wd_ref[0]) * jnp.sum(jnp.where(pick, gate_row, 0.0), axis=1, keepdims=True)
        o_ref[...] += _dot(jnp.where(place, 1.0, 0.0).astype(BF16), out.astype(BF16))
        return carry

    lax.fori_loop(0, (n_tok + sub - 1) // sub, sub_tile, 0)


def _moe(x, g, router, bias, wg, wu, wd):
    t, d = x.shape
    n_e, _, f = wg.shape
    tm = min(FFN_ROW_TILE, t)
    r_pad = jnp.zeros((d, 128), F32).at[:, 0:n_e].set(router)
    b_pad = jnp.zeros((1, 128), F32).at[0, 0:n_e].set(bias)
    tok = pl.BlockSpec((tm, d), lambda i, j: (i, 0))
    return pl.pallas_call(
        _moe_kernel,
        grid=(t // tm, n_e),
        in_specs=[tok, _full((1, d)), _full((d, 128)), _full((d, 128)), _full((1, 128)),
                  pl.BlockSpec((1, d, f), lambda i, j: (j, 0, 0)), pl.BlockSpec((1, d, f), lambda i, j: (j, 0, 0)),
                  pl.BlockSpec((1, f, d), lambda i, j: (j, 0, 0))],
        out_specs=tok,
        out_shape=jax.ShapeDtypeStruct((t, d), F32),
        scratch_shapes=[pltpu.VMEM((tm, d), BF16), pltpu.VMEM((tm, 128), F32), pltpu.VMEM((tm, 128), F32),
                        pltpu.VMEM((128, tm), F32), pltpu.VMEM((128, tm), F32)],
        compiler_params=_params("parallel", "arbitrary"),
        name="moe_swiglu",
    )(x, g.reshape(1, d), *_split(r_pad), b_pad, wg, wu, wd)


def kernel(x, mem, positions, norm_mix, w_in, rwkv_mu, rwkv_w0, rwkv_w2, rwkv_a0, rwkv_a2, rwkv_g2, rwkv_v0, rwkv_v1, rwkv_v2, rwkv_kk, rwkv_ka, rwkv_rk, rwkv_lnx_g, rwkv_lnx_b, dsa_q_norm, dsa_k_norm, idx_k_norm, w_branch_a, w_branch_b, w_out, norm_mem, mem_tok_norm, mem_wq, mem_wkv, mem_q_norm, mem_k_norm, mem_wo, norm_ffn, ffn_wg, ffn_wu, ffn_wd, moe_router, moe_bias, moe_wg, moe_wu, moe_wd):
    w = dict(rwkv_mu=rwkv_mu, rwkv_w0=rwkv_w0, rwkv_w2=rwkv_w2, rwkv_a0=rwkv_a0, rwkv_a2=rwkv_a2,
             rwkv_g2=rwkv_g2, rwkv_v0=rwkv_v0, rwkv_v1=rwkv_v1, rwkv_v2=rwkv_v2, rwkv_kk=rwkv_kk,
             rwkv_ka=rwkv_ka, rwkv_rk=rwkv_rk, rwkv_lnx_g=rwkv_lnx_g, rwkv_lnx_b=rwkv_lnx_b)
    b, s, d = x.shape
    t = b * s
    depth = w_in.shape[0]
    n_a = w_in.shape[2] - 1224 - 2 * d
    bf = lambda a: a.astype(BF16)
    cos, sin = _rope_tables(positions)
    x = x.reshape(t, d)
    v_first = None
    for l in range(depth):
        w_a = bf(w_in[l, :, 0:n_a])
        w_b = bf(_dsa_weight(w_in[l, :, n_a:n_a + 1224]))
        w_g = bf(w_in[l, :, n_a + 1224:])
        cols_a, cols_b, gates = _norm_matmul(x, norm_mix[l], (w_a, w_b, w_g))
        y_a, v_first = _rwkv_branch(cols_a.reshape(b, s, n_a), v_first, _rwkv_params(w, l))
        q, iq, kik, vw = _dsa_prep(cols_b, cos, sin, dsa_q_norm[l], dsa_k_norm[l], idx_k_norm[l])
        y_b = _dsa_attention(q, iq, kik, vw, b, s)
        x = _merge(x, gates, y_a.reshape(t, MIX_DIM), y_b.reshape(t, MIX_DIM),
                   bf(w_branch_a[l]), bf(w_branch_b[l]), bf(w_out[l]))
        mk, mv = _mem_kv(mem, mem_tok_norm[l], bf(mem_wkv[l]), mem_k_norm[l])
        x = _mem_attention(x.reshape(b, s, d), norm_mem[l], bf(mem_wq[l]), mem_q_norm[l], mk, mv,
                           bf(mem_wo[l])).reshape(t, d)
        j = l // 2
        if l % 2 == 0:
            x = _ffn(x, norm_ffn[l], bf(ffn_wg[j]), bf(ffn_wu[j]), bf(ffn_wd[j]))
        else:
            x = _moe(x, norm_ffn[l], moe_router[j], moe_bias[j], bf(moe_wg[j]), bf(moe_wu[j]), bf(moe_wd[j]))
    return x.reshape(b, s, d)
```

```python
import functools

import jax
import jax.numpy as jnp
import numpy as np
from jax import lax
from jax.experimental import pallas as pl
from jax.experimental.pallas import tpu as pltpu

F32 = jnp.float32
BF16 = jnp.bfloat16
HIGHEST = lax.Precision.HIGHEST

NORM_EPS = 1e-6
NEG_INF = -1e30
ROPE_THETA = 10000.0
LOG2_E = float(np.log2(np.e))

CHUNK = 64
Q_BLOCK = 128
DSA_TOPK_MAX = 256
DSA_KEY_CLASSES = 8
DSA_ROW_GROUPS = 4
DSA_RADIX4_MAX_KEYS = 768
HEAD_DIM = 64
N_HEADS = 8
MIX_DIM = N_HEADS * HEAD_DIM
RWKV_LNX_EPS = 1e-5 * HEAD_DIM
MEM_HEADS = 4
MEM_HEAD_DIM = 128
N_EXPERTS = 8

RWKV_CHUNK = 64
RWKV_BLOCK = 256
ROW_TILE = 512
WIDE_ROW_TILE = 1024
FFN_ROW_TILE = 1024
MOE_SUB = 288
VMEM_LIMIT = 56 * 1024 * 1024


def _params(*sem):
    return pltpu.CompilerParams(dimension_semantics=sem, vmem_limit_bytes=VMEM_LIMIT)


def _full(shape):
    nd = len(shape)
    return pl.BlockSpec(shape, lambda *_: (0,) * nd)


def _dot(a, b, precision=None):
    return jnp.dot(a, b, preferred_element_type=F32, precision=precision)


def _dot_nt(a, b, precision=None):
    return lax.dot_general(a, b, (((1,), (1,)), ((), ())), preferred_element_type=F32,
                           precision=precision)


def _dot_tn(a, b, precision=None):
    return lax.dot_general(a, b, (((0,), (0,)), ((), ())), preferred_element_type=F32,
                           precision=precision)


def _bdot(a, b):
    return _dot(a.astype(BF16), b.astype(BF16))


def _split(x):
    hi = x.astype(BF16)
    return hi, (x - hi.astype(F32)).astype(BF16)


def _dot_x2(x, m):
    hi, lo = _split(x)
    return _dot(hi, m) + _dot(lo, m)


def _dot_x3(x, w_hi, w_lo):
    hi, lo = _split(x)
    return _dot(hi, w_hi) + (_dot(lo, w_hi) + _dot(hi, w_lo))


def _rms(x, g):
    return x * lax.rsqrt(jnp.mean(x * x, axis=-1, keepdims=True) + NORM_EPS) * g


def _sigmoid(x):
    return 1.0 / (1.0 + jnp.exp(-x))


def _head_ones(width, head):
    r = lax.broadcasted_iota(jnp.int32, (width, width), 0) // head
    c = lax.broadcasted_iota(jnp.int32, (width, width), 1) // head
    return (r == c).astype(BF16)


def _norm_matmul_kernel(x_ref, g_ref, *refs):
    n = len(refs) // 2
    h = _rms(x_ref[...], g_ref[...]).astype(BF16)
    for w_ref, o_ref in zip(refs[:n], refs[n:]):
        o_ref[...] = _dot(h, w_ref[...])


def _norm_matmul(x, g, ws, tm=ROW_TILE):
    t, d = x.shape
    tm = min(tm, t)
    return pl.pallas_call(
        _norm_matmul_kernel,
        grid=(t // tm,),
        in_specs=[pl.BlockSpec((tm, d), lambda i: (i, 0)), _full((1, d))] + [_full(w.shape) for w in ws],
        out_specs=tuple(pl.BlockSpec((tm, w.shape[1]), lambda i: (i, 0)) for w in ws),
        out_shape=tuple(jax.ShapeDtypeStruct((t, w.shape[1]), F32) for w in ws),
        compiler_params=_params("parallel"),
        name="norm_matmul",
    )(x, g.reshape(1, d), *ws)


def _rwkv_kernel(has_vmix, *refs):
    if has_vmix:
        (cols_ref, vf_ref, mu_ref, w0_ref, a0_ref, wah_ref, wal_ref, g2_ref, kk_ref, ka_ref,
         rk_ref, lg_ref, lb_ref, v0_ref, v1_ref, v2_ref,
         y_ref, st_ref, prev_ref, at_ref, rt_ref, bt_ref, kt_ref, bh_ref, kh_ref, vv_ref,
         gdec_ref, bonus_ref, gate_ref, yy_ref, y0_ref, p_ref, rb_ref, z_ref, rhs_ref, dl_ref) = refs
        vout_ref = None
    else:
        (cols_ref, mu_ref, w0_ref, a0_ref, wah_ref, wal_ref, g2_ref, kk_ref, ka_ref,
         rk_ref, lg_ref, lb_ref,
         y_ref, vout_ref, st_ref, prev_ref, at_ref, rt_ref, bt_ref, kt_ref, bh_ref, kh_ref, vv_ref,
         gdec_ref, bonus_ref, gate_ref, yy_ref, y0_ref, p_ref, rb_ref, z_ref, rhs_ref, dl_ref) = refs

    tb = cols_ref.shape[1]
    c = RWKV_CHUNK
    n_chunks = tb // c
    hd = HEAD_DIM

    @pl.when(pl.program_id(1) == 0)
    def _():
        st_ref[...] = jnp.zeros_like(st_ref)
        prev_ref[...] = jnp.zeros_like(prev_ref)

    cols = cols_ref[0]
    row = lax.broadcasted_iota(jnp.int32, cols.shape, 0)
    shifted = jnp.where(row == 0, prev_ref[...], pltpu.roll(cols, 1, 0))
    prev_ref[...] = cols[tb - 1:tb, :]
    mixed = cols + (shifted - cols) * mu_ref[...]

    r = mixed[:, 0:MIX_DIM]
    k = mixed[:, MIX_DIM:2 * MIX_DIM]
    v = mixed[:, 2 * MIX_DIM:3 * MIX_DIM]
    wa = mixed[:, 3 * MIX_DIM:3 * MIX_DIM + 128]
    gd = mixed[:, 3 * MIX_DIM + 128:3 * MIX_DIM + 256]
    lane = lax.broadcasted_iota(jnp.int32, wa.shape, 1)
    wa = jnp.where(lane < 64, jnp.tanh(wa), wa)
    wa_out = _dot_x3(wa, wah_ref[...], wal_ref[...])
    u_dec = w0_ref[...] + wa_out[:, 0:MIX_DIM]
    a = _sigmoid(a0_ref[...] + wa_out[:, MIX_DIM:2 * MIX_DIM])
    gate_ref[...] = _bdot(_sigmoid(gd), g2_ref[...])
    if has_vmix:
        vm = _bdot(_bdot(v, v1_ref[...]), v2_ref[...])
        v = v + (vf_ref[0] - v) * _sigmoid(v0_ref[...] + vm)
    else:
        vout_ref[0] = v

    hb = _head_ones(MIX_DIM, hd)
    kk = k * kk_ref[...]
    kk = kk / jnp.maximum(jnp.sqrt(_dot_x2(kk * kk, hb)), 1e-12)
    k = k * (1.0 + (a - 1.0) * ka_ref[...])
    bonus_ref[...] = _bdot(r * k * rk_ref[...], hb) * v
    ll = -_sigmoid(u_dec) * float(np.exp(-0.5))

    ri = lax.broadcasted_iota(jnp.int32, (tb, tb), 0)
    ci = lax.broadcasted_iota(jnp.int32, (tb, tb), 1)
    same = (ri // c) == (ci // c)
    strict_u = (same & (ri < ci)).astype(F32)
    incl_u = (same & (ri <= ci)).astype(F32)
    ll_hi, ll_lo = _split(ll)
    tri = (same & (ri >= ci)).astype(BF16)
    cum = _dot(tri, ll_hi) + _dot(tri, ll_lo)
    tot_rows = [cum[(ch + 1) * c - 1:(ch + 1) * c, :] for ch in range(n_chunks)]
    tot = jnp.concatenate([jnp.broadcast_to(t_row, (c, MIX_DIM)) for t_row in tot_rows], axis=0)
    for ch in range(n_chunks):
        gdec_ref[ch:ch + 1, :] = jnp.exp(tot_rows[ch])
    g_inv = jnp.exp(-cum)
    g_tail = jnp.exp(tot - cum)
    beta = kk * a
    at_ref[...] = (-kk * jnp.exp(cum - ll)).T.astype(BF16)
    rt_ref[...] = (r * jnp.exp(cum)).T.astype(BF16)
    vv_ref[...] = v.T.astype(BF16)
    bt_ref[...] = (beta * g_inv).astype(BF16)
    kt_ref[...] = (k * g_inv).astype(BF16)
    bh_ref[...] = (beta * g_tail).astype(BF16)
    kh_ref[...] = (k * g_tail).astype(BF16)

    n_levels = int(np.log2(c))
    heads = [slice(h * hd, (h + 1) * hd) for h in range(N_HEADS)]

    for h, cs in enumerate(heads):
        b_t = bt_ref[:, cs]
        k_t = kt_ref[:, cs]
        a_tr = at_ref[cs, :]
        r_tr = rt_ref[cs, :]
        v_tr = vv_ref[cs, :]
        p_ref[h] = (_dot(b_t, a_tr) * strict_u).astype(BF16)
        m_tr = (_dot(k_t, a_tr) * strict_u).astype(BF16)
        rb_ref[h] = (_dot(b_t, r_tr) * incl_u).astype(BF16)
        rk_tr = (_dot(k_t, r_tr) * incl_u).astype(BF16)
        y0_ref[cs, :] = _dot(v_tr, rk_tr)
        z_ref[h] = jnp.concatenate([a_tr.astype(F32), _dot(v_tr, m_tr)], axis=0)

    for lev in range(n_levels):
        for h in range(N_HEADS):
            p = p_ref[h]
            z = z_ref[h]
            z_ref[h] = z + _dot(z.astype(BF16), p)
            if lev < n_levels - 1:
                p_ref[h] = _dot(p, p).astype(BF16)

    for h, cs in enumerate(heads):
        z_b = z_ref[h].astype(BF16)
        w = _dot(z_b, rb_ref[h])
        r_bar = (rt_ref[cs, :].astype(F32) + w[0:hd]).astype(BF16)
        y0_ref[cs, :] = y0_ref[cs, :] + w[hd:2 * hd]
        for ch in range(n_chunks):
            cols = slice(ch * c, (ch + 1) * c)
            pd = _dot(z_b[:, cols], bh_ref[cols, cs])
            d2 = _dot(vv_ref[cs, cols], kh_ref[cols, cs])
            rhs_ref[h, ch] = jnp.concatenate([pd[0:hd].astype(BF16), r_bar[:, cols]], axis=1)
            dl_ref[h, ch] = pd[hd:2 * hd] + d2

    for ch in range(n_chunks):
        cols = slice(ch * c, (ch + 1) * c)
        for h, cs in enumerate(heads):
            st = st_ref[h]
            res = _dot(st.astype(BF16), rhs_ref[h, ch])
            yy_ref[cs, cols] = res[:, hd:hd + c] + y0_ref[cs, cols]
            st_ref[h] = st * gdec_ref[ch:ch + 1, cs] + res[:, 0:hd] + dl_ref[h, ch]

    y = yy_ref[...].T
    mean = _bdot(y, hb) * (1.0 / hd)
    yc = y - mean
    var = _bdot(yc * yc, hb) * (1.0 / hd)
    yn = yc * lax.rsqrt(var + RWKV_LNX_EPS) * lg_ref[...] + lb_ref[...]
    y_ref[0] = ((yn + bonus_ref[...]) * gate_ref[...]).astype(y_ref.dtype)


def _rwkv_params(w, l):
    p = {k: w["rwkv_" + k][l] for k in ("mu", "w0", "w2", "a0", "a2", "g2", "kk", "ka", "rk", "lnx_g", "lnx_b")}
    if l > 0:
        p.update({k: w["rwkv_" + k][l - 1] for k in ("v0", "v1", "v2")})
    return p


def _rwkv_branch(cols, v_first, p):
    b, s, nc = cols.shape
    tb = min(RWKV_BLOCK, s)
    has_vmix = v_first is not None
    row = lambda a: a.reshape(1, -1)
    pair = lambda a: list(_split(a))
    tok = lambda w: pl.BlockSpec((1, tb, w), lambda i, j: (i, j, 0))
    wa2 = jnp.zeros((128, 2 * MIX_DIM), F32)
    wa2 = wa2.at[0:64, 0:MIX_DIM].set(p["w2"]).at[64:128, MIX_DIM:].set(p["a2"])
    ins = [cols] + ([v_first] if has_vmix else []) + [
        row(p["mu"]), row(p["w0"]), row(p["a0"])] + pair(wa2) + [p["g2"].astype(BF16)] + [
        row(p["kk"]), row(p["ka"]), row(p["rk"]), row(p["lnx_g"]), row(p["lnx_b"])]
    if has_vmix:
        ins += [row(p["v0"]), p["v1"].astype(BF16), p["v2"].astype(BF16)]
    in_specs = [tok(nc)] + ([tok(MIX_DIM)] if has_vmix else []) + [_full(a.shape) for a in ins[1 + has_vmix:]]
    y_shape = jax.ShapeDtypeStruct((b, s, MIX_DIM), BF16)
    out_shape = y_shape if has_vmix else (y_shape, jax.ShapeDtypeStruct((b, s, MIX_DIM), F32))
    out_specs = tok(MIX_DIM) if has_vmix else (tok(MIX_DIM), tok(MIX_DIM))
    n_ch = tb // RWKV_CHUNK
    tok_bf = pltpu.VMEM((tb, MIX_DIM), BF16)
    chan_bf = pltpu.VMEM((MIX_DIM, tb), BF16)
    tok_f32 = pltpu.VMEM((tb, MIX_DIM), F32)
    chan_f32 = pltpu.VMEM((MIX_DIM, tb), F32)
    scratch = [pltpu.VMEM((N_HEADS, HEAD_DIM, HEAD_DIM), F32), pltpu.VMEM((1, nc), F32),
               chan_bf, chan_bf, tok_bf, tok_bf, tok_bf, tok_bf, chan_bf,
               pltpu.VMEM((8, MIX_DIM), F32), tok_f32, tok_f32, chan_f32, chan_f32,
               pltpu.VMEM((N_HEADS, tb, tb), BF16), pltpu.VMEM((N_HEADS, tb, tb), BF16),
               pltpu.VMEM((N_HEADS, 2 * HEAD_DIM, tb), F32),
               pltpu.VMEM((N_HEADS, n_ch, HEAD_DIM, 2 * RWKV_CHUNK), BF16),
               pltpu.VMEM((N_HEADS, n_ch, HEAD_DIM, HEAD_DIM), F32)]
    out = pl.pallas_call(
        functools.partial(_rwkv_kernel, has_vmix),
        grid=(b, s // tb),
        in_specs=in_specs,
        out_specs=out_specs,
        out_shape=out_shape,
        scratch_shapes=scratch,
        compiler_params=_params("parallel", "arbitrary"),
        name="rwkv7_vmix" if has_vmix else "rwkv7_first",
    )(*ins)
    return (out, v_first) if has_vmix else out


DSA_COLS_PAD = 1280
_DSA_Q, _DSA_IQ, _DSA_KIK, _DSA_VW = 0, 512, 1024, 1152


def _rope_table_kernel(pos_ref, freq_ref, cos_ref, sin_ref):
    ang = pos_ref[...] * freq_ref[...]
    lane = lax.broadcasted_iota(jnp.int32, ang.shape, 1)
    cos_ref[...] = jnp.cos(ang)
    sin_ref[...] = jnp.where(lane % HEAD_DIM < HEAD_DIM // 2, -jnp.sin(ang), jnp.sin(ang))


def _rope_tables(positions):
    t = positions.size
    tm = min(WIDE_ROW_TILE, t)
    inv_freq = 1.0 / (ROPE_THETA ** (jnp.arange(0, HEAD_DIM, 2, dtype=F32) / HEAD_DIM))
    freq = jnp.tile(inv_freq, 4).reshape(1, 128)
    pos = positions.reshape(t, 1).astype(F32)
    spec = pl.BlockSpec((tm, 128), lambda i: (i, 0))
    return pl.pallas_call(
        _rope_table_kernel,
        grid=(t // tm,),
        in_specs=[pl.BlockSpec((tm, 1), lambda i: (i, 0)), _full((1, 128))],
        out_specs=(spec, spec),
        out_shape=(jax.ShapeDtypeStruct((t, 128), F32),) * 2,
        compiler_params=_params("parallel"),
        name="rope_tables",
    )(pos, freq)


def _rope128(x, cos, sin):
    lane = lax.broadcasted_iota(jnp.int32, x.shape, 1)
    half = HEAD_DIM // 2
    rot = jnp.where(lane % HEAD_DIM < half, pltpu.roll(x, 128 - half, 1), pltpu.roll(x, half, 1))
    return x * cos + rot * sin


def _dsa_prep_kernel(cols_ref, cos_ref, sin_ref, qn_ref, kn_ref, q_ref, iq_ref, kik_ref, vw_ref):
    cos, sin = cos_ref[...], sin_ref[...]
    q = cols_ref[:, _DSA_Q:_DSA_Q + MIX_DIM]
    ss = _dot_x2(q * q, _head_ones(MIX_DIM, HEAD_DIM)) * (1.0 / HEAD_DIM)
    q = q * lax.rsqrt(ss + NORM_EPS) * qn_ref[...]
    kik = cols_ref[:, _DSA_KIK:_DSA_KIK + 128]
    ss = _dot_x2(kik * kik, _head_ones(128, HEAD_DIM)) * (1.0 / HEAD_DIM)
    kik_ref[...] = _rope128(kik * lax.rsqrt(ss + NORM_EPS) * kn_ref[...], cos, sin).astype(BF16)
    for j in range(MIX_DIM // 128):
        sl = slice(j * 128, (j + 1) * 128)
        q_ref[j] = (_rope128(q[:, sl], cos, sin) * (HEAD_DIM ** -0.5 * LOG2_E)).astype(BF16)
        iq = _rope128(cols_ref[:, _DSA_IQ + j * 128:_DSA_IQ + (j + 1) * 128], cos, sin) * (HEAD_DIM ** -0.5)
        iq_ref[:, sl] = iq.astype(BF16)
    vw = cols_ref[:, _DSA_VW:_DSA_VW + 128]
    vw_ref[...] = jnp.where(lax.broadcasted_iota(jnp.int32, vw.shape, 1) == 127, 1.0, vw)


def _dsa_prep(cols, cos, sin, q_norm, k_norm, idx_k_norm):
    t = cols.shape[0]
    tm = min(WIDE_ROW_TILE, t)
    qn = jnp.tile(q_norm, N_HEADS).reshape(1, MIX_DIM)
    kn = jnp.concatenate([k_norm, idx_k_norm]).reshape(1, 128)
    tok = lambda w: pl.BlockSpec((tm, w), lambda i: (i, 0))
    sds = lambda w: jax.ShapeDtypeStruct((t, w), F32)
    return pl.pallas_call(
        _dsa_prep_kernel,
        grid=(t // tm,),
        in_specs=[tok(DSA_COLS_PAD), tok(128), tok(128), _full((1, MIX_DIM)), _full((1, 128))],
        out_specs=(pl.BlockSpec((N_HEADS // 2, tm, 128), lambda i: (0, i, 0)), tok(MIX_DIM), tok(128), tok(128)),
        out_shape=(jax.ShapeDtypeStruct((N_HEADS // 2, t, 128), BF16), jax.ShapeDtypeStruct((t, MIX_DIM), BF16),
                   jax.ShapeDtypeStruct((t, 128), BF16), sds(128)),
        compiler_params=_params("parallel"),
        name="dsa_prep",
    )(cols, cos, sin, qn, kn)


def _count(mask):
    return jnp.sum(jnp.where(mask, 1.0, 0.0), axis=1, keepdims=True)


def _dsa_block(top_k, n_keys, q_ref, iq_ref, vwq_ref, kik_ref, vw_ref, o_ref, key_ref, s_ref):
    qb = iq_ref.shape[1]
    start = pl.program_id(1) * qb
    k = kik_ref[0, 0:n_keys, 0:HEAD_DIM].astype(BF16)
    ik = kik_ref[0, 0:n_keys, HEAD_DIM:2 * HEAD_DIM].astype(BF16)
    v_ones = vw_ref[0, 0:n_keys, :].astype(BF16)
    iw = vwq_ref[0, :, HEAD_DIM:HEAD_DIM + N_HEADS] * (N_HEADS ** -0.5)

    score = jnp.zeros((qb, n_keys), F32)
    for h in range(N_HEADS):
        lg = _dot_nt(iq_ref[0, :, h * HEAD_DIM:(h + 1) * HEAD_DIM].astype(BF16), ik)
        score = score + jnp.maximum(lg, 0.0) * iw[:, h:h + 1]
    key_pos = lax.broadcasted_iota(jnp.int32, (qb, n_keys), 1)
    q_pos = start + lax.broadcasted_iota(jnp.int32, (qb, 1), 0)
    limit = (q_pos // CHUNK + 1) * CHUNK
    adm = key_pos < limit
    score = jnp.where(adm, score, NEG_INF)
    score = jnp.where(score == 0.0, 0.0, score)

    bits = pltpu.bitcast(score, jnp.int32)
    key = bits ^ ((bits >> 31) & jnp.int32(0x7FFFFFFF))
    kf = float(top_k)
    key_ref[:, 0:n_keys] = key
    gr = qb // DSA_ROW_GROUPS
    digit_bits = 2 if n_keys <= DSA_RADIX4_MAX_KEYS else 1
    n_steps = 32 // digit_bits

    def theta_body(i, ths):
        unit = jnp.int32(1) << (32 - digit_bits - digit_bits * i)
        out = []
        for g in range(DSA_ROW_GROUPS):
            keys = key_ref[g * gr:(g + 1) * gr, 0:n_keys]
            passed = jnp.zeros((gr, 1), jnp.int32)
            for d in range(1, 2 ** digit_bits):
                cnt = _count(keys >= ths[g] + unit * d)
                passed = passed + jnp.where(cnt >= kf, 1, 0)
            out.append(ths[g] + unit * passed)
        return tuple(out)

    n_pairs = N_HEADS // 2
    per_pair = n_steps // n_pairs

    def pair_step(hp, ths):
        q2 = q_ref[hp].astype(BF16)
        s_ref[2 * hp, :, 0:n_keys] = _dot_nt(q2[:, 0:HEAD_DIM], k)
        s_ref[2 * hp + 1, :, 0:n_keys] = _dot_nt(q2[:, HEAD_DIM:2 * HEAD_DIM], k)
        for j in range(per_pair):
            ths = theta_body(hp * per_pair + j, ths)
        return ths

    lowest = jnp.full((gr, 1), -2 ** 31, jnp.int32)
    ths = lax.fori_loop(0, n_pairs, pair_step, (lowest,) * DSA_ROW_GROUPS)
    theta = jnp.concatenate(ths, axis=0)
    gt = key > theta
    eq = key == theta
    n_gt = _count(gt)
    need = kf - n_gt
    n_bits = int(np.ceil(np.log2(n_keys))) + 1

    def tie_search():
        def tie_body(i, jb):
            cand = jb + (jnp.int32(1) << (n_bits - 1 - i))
            ok = (cand <= n_keys) & (_count(eq & (key_pos < cand)) <= need)
            return jnp.where(ok, cand, jb)

        return lax.fori_loop(0, n_bits, tie_body, jnp.zeros((qb, 1), jnp.int32))

    has_tie = jnp.max(n_gt + _count(eq & adm)) > kf
    bound = lax.cond(has_tie, tie_search, lambda: jnp.full((qb, 1), n_keys, jnp.int32))
    bias = jnp.where((gt | (eq & (key_pos < bound))) & adm, 0.0, NEG_INF)

    tops = []
    for h in range(N_HEADS):
        s = s_ref[h, :, 0:n_keys] + bias
        s_ref[h, :, 0:n_keys] = s
        tops.append(jnp.max(s, axis=1, keepdims=True))
    for h in range(N_HEADS):
        p = jnp.exp2(s_ref[h, :, 0:n_keys] - tops[h])
        o = _dot(p.astype(BF16), v_ones)
        o_ref[0, :, h * HEAD_DIM:(h + 1) * HEAD_DIM] = (o[:, 0:HEAD_DIM] / o[:, 127:128]).astype(o_ref.dtype)


def _dsa_attn_kernel(top_k, n_classes, q_ref, iq_ref, vwq_ref, kik_ref, vw_ref, o_ref, key_ref, s_ref):
    qb = iq_ref.shape[1]
    per_class = (kik_ref.shape[1] // qb) // n_classes
    cls = pl.program_id(1) // per_class
    for n in range(n_classes):
        pl.when(cls == n)(functools.partial(
            _dsa_block, top_k, (n + 1) * per_class * qb, q_ref, iq_ref, vwq_ref, kik_ref, vw_ref, o_ref, key_ref, s_ref))


def _dsa_attention(q, iq, kik, vw, b, s):
    top_k = min(DSA_TOPK_MAX, s // 4)
    qb = min(Q_BLOCK, s)
    n_classes = min(DSA_KEY_CLASSES, s // qb)
    r3 = lambda a: a.reshape(b, s, a.shape[-1])
    blk = lambda w: pl.BlockSpec((1, qb, w), lambda i, j: (i, j, 0))
    seq = lambda w: pl.BlockSpec((1, s, w), lambda i, j: (i, 0, 0))
    q_heads = pl.BlockSpec((N_HEADS // 2, qb, 128), lambda i, j: (0, i * (s // qb) + j, 0))
    return pl.pallas_call(
        functools.partial(_dsa_attn_kernel, top_k, n_classes),
        grid=(b, s // qb),
        in_specs=[q_heads, blk(MIX_DIM), blk(128), seq(128), seq(128)],
        out_specs=blk(MIX_DIM),
        out_shape=jax.ShapeDtypeStruct((b, s, MIX_DIM), BF16),
        scratch_shapes=[pltpu.VMEM((qb, s), jnp.int32), pltpu.VMEM((N_HEADS, qb, s), F32)],
        compiler_params=_params("parallel", "arbitrary"),
        name="dsa_attention",
    )(q, r3(iq), r3(vw), r3(kik), r3(vw))


def _dsa_weight(w_dsa):
    d = w_dsa.shape[0]
    q, k, v, iq, ik, iw = jnp.split(w_dsa, np.cumsum([512, 64, 64, 512, 64, 8])[:-1].tolist(), axis=1)
    pad = jnp.zeros((d, DSA_COLS_PAD - 1224), w_dsa.dtype)
    return jnp.concatenate([q, iq, k, ik, v, iw, pad], axis=1)


def _merge_kernel(x_ref, g_ref, ya_ref, yb_ref, wa_ref, wb_ref, wo_ref, o_ref):
    d = x_ref.shape[1]
    ga = _sigmoid(g_ref[:, 0:d])
    gb = _sigmoid(g_ref[:, d:2 * d])
    merged = ga * _bdot(ya_ref[...], wa_ref[...]) + gb * _bdot(yb_ref[...], wb_ref[...])
    o_ref[...] = x_ref[...] + _bdot(merged, wo_ref[...])


def _merge(x, gates, ya, yb, wa, wb, wo):
    t, d = x.shape
    tm = min(WIDE_ROW_TILE, t)
    tok = lambda w: pl.BlockSpec((tm, w), lambda i: (i, 0))
    return pl.pallas_call(
        _merge_kernel,
        grid=(t // tm,),
        in_specs=[tok(d), tok(2 * d), tok(MIX_DIM), tok(MIX_DIM), _full(wa.shape), _full(wb.shape), _full(wo.shape)],
        out_specs=tok(d),
        out_shape=jax.ShapeDtypeStruct((t, d), F32),
        compiler_params=_params("parallel"),
        name="merge_out",
    )(x, gates, ya, yb, wa, wb, wo)


def _mem_kv_kernel(mem_ref, g_ref, wkv_ref, kn_ref, k_ref, v_ref):
    dm = k_ref.shape[2]
    kv = _dot(_rms(mem_ref[0], g_ref[...]).astype(BF16), wkv_ref[...])
    k = kv[:, 0:dm]
    ss = _dot_x2(k * k, _head_ones(dm, MEM_HEAD_DIM)) * (1.0 / MEM_HEAD_DIM)
    k_ref[0] = k * lax.rsqrt(ss + NORM_EPS) * kn_ref[...]
    v_ref[0] = kv[:, dm:2 * dm]


def _mem_kv(mem, g, wkv, k_norm):
    b, m, d = mem.shape
    dm = MEM_HEADS * MEM_HEAD_DIM
    kn = jnp.tile(k_norm, MEM_HEADS).reshape(1, dm)
    spec = pl.BlockSpec((1, m, dm), lambda i: (i, 0, 0))
    return pl.pallas_call(
        _mem_kv_kernel,
        grid=(b,),
        in_specs=[pl.BlockSpec((1, m, d), lambda i: (i, 0, 0)), _full((1, d)), _full(wkv.shape), _full((1, dm))],
        out_specs=(spec, spec),
        out_shape=(jax.ShapeDtypeStruct((b, m, dm), F32),) * 2,
        compiler_params=_params("parallel"),
        name="mem_kv",
    )(mem, g.reshape(1, d), wkv, kn)


def _mem_attn_kernel(x_ref, g_ref, wq_ref, qn_ref, k_ref, v_ref, wo_ref, o_ref):
    x = x_ref[0]
    dm = wq_ref.shape[1]
    q = _dot(_rms(x, g_ref[...]).astype(BF16), wq_ref[...])
    ss = _dot_x2(q * q, _head_ones(dm, MEM_HEAD_DIM)) * (1.0 / MEM_HEAD_DIM)
    q = q * lax.rsqrt(ss + NORM_EPS) * qn_ref[...] * (MEM_HEAD_DIM ** -0.5)
    outs = []
    for h in range(MEM_HEADS):
        sl = slice(h * MEM_HEAD_DIM, (h + 1) * MEM_HEAD_DIM)
        s = _dot_nt(q[:, sl].astype(BF16), k_ref[0, :, sl].astype(BF16))
        p = jnp.exp(s - jnp.max(s, axis=1, keepdims=True))
        p = p / jnp.sum(p, axis=1, keepdims=True)
        outs.append(_dot(p.astype(BF16), v_ref[0, :, sl].astype(BF16)))
    o = jnp.concatenate(outs, axis=1)
    o_ref[0] = x + _bdot(o, wo_ref[...])


def _mem_attention(x, g, wq, q_norm, k, v, wo):
    b, s, d = x.shape
    m, dm = k.shape[1], k.shape[2]
    tm = min(WIDE_ROW_TILE, s)
    qn = jnp.tile(q_norm, MEM_HEADS).reshape(1, dm)
    tok = pl.BlockSpec((1, tm, d), lambda i, j: (i, j, 0))
    kv = pl.BlockSpec((1, m, dm), lambda i, j: (i, 0, 0))
    return pl.pallas_call(
        _mem_attn_kernel,
        grid=(b, s // tm),
        in_specs=[tok, _full((1, d)), _full(wq.shape), _full((1, dm)), kv, kv, _full(wo.shape)],
        out_specs=tok,
        out_shape=jax.ShapeDtypeStruct((b, s, d), F32),
        compiler_params=_params("parallel", "parallel"),
        name="mem_attention",
    )(x, g.reshape(1, d), wq, qn, k, v, wo)


def _silu(x):
    return x * _sigmoid(x)


def _ffn_kernel(x_ref, g_ref, wg_ref, wu_ref, wd_ref, o_ref, h_ref):
    j = pl.program_id(1)

    @pl.when(j == 0)
    def _():
        x = x_ref[...]
        h_ref[...] = _rms(x, g_ref[...]).astype(BF16)
        o_ref[...] = x

    h = h_ref[...]
    act = _silu(_dot(h, wg_ref[...])) * _dot(h, wu_ref[...])
    o_ref[...] += _bdot(act, wd_ref[...])


def _ffn(x, g, wg, wu, wd, n_f=2):
    t, d = x.shape
    f = wg.shape[1]
    tm = min(FFN_ROW_TILE, t)
    tf = f // n_f
    tok = pl.BlockSpec((tm, d), lambda i, j: (i, 0))
    return pl.pallas_call(
        _ffn_kernel,
        grid=(t // tm, n_f),
        in_specs=[tok, _full((1, d)), pl.BlockSpec((d, tf), lambda i, j: (0, j)),
                  pl.BlockSpec((d, tf), lambda i, j: (0, j)), pl.BlockSpec((tf, d), lambda i, j: (j, 0))],
        out_specs=tok,
        out_shape=jax.ShapeDtypeStruct((t, d), F32),
        scratch_shapes=[pltpu.VMEM((tm, d), BF16)],
        compiler_params=_params("parallel", "arbitrary"),
        name="ffn_swiglu",
    )(x, g.reshape(1, d), wg, wu, wd)


def _moe_kernel(x_ref, g_ref, rh_ref, rl_ref, b_ref, wg_ref, wu_ref, wd_ref, o_ref, h_ref, gate_ref, rank_ref,
                gate_t_ref, rank_t_ref):
    e = pl.program_id(1)
    tm = x_ref.shape[0]

    @pl.when(e == 0)
    def _():
        x = x_ref[...]
        hn = _rms(x, g_ref[...])
        h_ref[...] = hn.astype(BF16)
        o_ref[...] = x
        logits = _dot_x3(hn, rh_ref[...], rl_ref[...]) + b_ref[...]
        lane = lax.broadcasted_iota(jnp.int32, logits.shape, 1)
        logits = jnp.where(lane < N_EXPERTS, logits, -jnp.inf)
        m1 = jnp.max(logits, axis=1, keepdims=True)
        i1 = jnp.min(jnp.where(logits == m1, lane, 128), axis=1, keepdims=True)
        rest = jnp.where(lane == i1, -jnp.inf, logits)
        m2 = jnp.max(rest, axis=1, keepdims=True)
        i2 = jnp.min(jnp.where(rest == m2, lane, 128), axis=1, keepdims=True)
        e2 = jnp.exp(m2 - m1)
        w1 = 1.0 / (1.0 + e2)
        w2 = e2 / (1.0 + e2)
        gates = jnp.where(lane == i1, w1, 0.0) + jnp.where(lane == i2, w2, 0.0)
        gate_ref[...] = gates
        member = jnp.where(gates != 0.0, 1.0, 0.0).astype(BF16)
        earlier = jnp.where(lax.broadcasted_iota(jnp.int32, (tm, tm), 1)
                            < lax.broadcasted_iota(jnp.int32, (tm, tm), 0), 1.0, 0.0).astype(BF16)
        rank = _dot(earlier, member)
        rank_ref[...] = rank
        gate_t_ref[...] = gates.T
        rank_t_ref[...] = rank.T

    lane = lax.broadcasted_iota(jnp.int32, gate_ref.shape, 1)
    gate_col = jnp.sum(jnp.where(lane == e, gate_ref[...], 0.0), axis=1, keepdims=True)
    rank_col = jnp.sum(jnp.where(lane == e, rank_ref[...], 0.0), axis=1, keepdims=True)
    gate_row = gate_t_ref[pl.ds(e, 1), :]
    rank_row = rank_t_ref[pl.ds(e, 1), :]
    n_tok = jnp.sum(jnp.where(gate_col != 0.0, 1.0, 0.0)).astype(jnp.int32)
    sub = min(MOE_SUB, tm)

    def sub_tile(s_idx, carry):
        base = (s_idx * sub).astype(F32)
        slot_col = base + lax.broadcasted_iota(jnp.int32, (sub, 1), 0).astype(F32)
        slot_row = base + lax.broadcasted_iota(jnp.int32, (1, sub), 1).astype(F32)
        pick = (gate_row != 0.0) & (rank_row == slot_col)
        place = (gate_col != 0.0) & (rank_col == slot_row)
        hs = _dot(jnp.where(pick, 1.0, 0.0).astype(BF16), h_ref[...]).astype(BF16)
        act = _silu(_dot(hs, wg_ref[0])) * _dot(hs, wu_ref[0])
        out = _bdot(act, wd_ref[0]) * jnp.sum(jnp.where(pick, gate_row, 0.0), axis=1, keepdims=True)
        o_ref[...] += _dot(jnp.where(place, 1.0, 0.0).astype(BF16), out.astype(BF16))
        return carry

    lax.fori_loop(0, (n_tok + sub - 1) // sub, sub_tile, 0)


def _moe(x, g, router, bias, wg, wu, wd):
    t, d = x.shape
    n_e, _, f = wg.shape
    tm = min(FFN_ROW_TILE, t)
    r_pad = jnp.zeros((d, 128), F32).at[:, 0:n_e].set(router)
    b_pad = jnp.zeros((1, 128), F32).at[0, 0:n_e].set(bias)
    tok = pl.BlockSpec((tm, d), lambda i, j: (i, 0))
    return pl.pallas_call(
        _moe_kernel,
        grid=(t // tm, n_e),
        in_specs=[tok, _full((1, d)), _full((d, 128)), _full((d, 128)), _full((1, 128)),
                  pl.BlockSpec((1, d, f), lambda i, j: (j, 0, 0)), pl.BlockSpec((1, d, f), lambda i, j: (j, 0, 0)),
                  pl.BlockSpec((1, f, d), lambda i, j: (j, 0, 0))],
        out_specs=tok,
        out_shape=jax.ShapeDtypeStruct((t, d), F32),
        scratch_shapes=[pltpu.VMEM((tm, d), BF16), pltpu.VMEM((tm, 128), F32), pltpu.VMEM((tm, 128), F32),
                        pltpu.VMEM((128, tm), F32), pltpu.VMEM((128, tm), F32)],
        compiler_params=_params("parallel", "arbitrary"),
        name="moe_swiglu",
    )(x, g.reshape(1, d), *_split(r_pad), b_pad, wg, wu, wd)


def kernel(x, mem, positions, norm_mix, w_in, rwkv_mu, rwkv_w0, rwkv_w2, rwkv_a0, rwkv_a2, rwkv_g2, rwkv_v0, rwkv_v1, rwkv_v2, rwkv_kk, rwkv_ka, rwkv_rk, rwkv_lnx_g, rwkv_lnx_b, dsa_q_norm, dsa_k_norm, idx_k_norm, w_branch_a, w_branch_b, w_out, norm_mem, mem_tok_norm, mem_wq, mem_wkv, mem_q_norm, mem_k_norm, mem_wo, norm_ffn, ffn_wg, ffn_wu, ffn_wd, moe_router, moe_bias, moe_wg, moe_wu, moe_wd):
    w = dict(rwkv_mu=rwkv_mu, rwkv_w0=rwkv_w0, rwkv_w2=rwkv_w2, rwkv_a0=rwkv_a0, rwkv_a2=rwkv_a2,
             rwkv_g2=rwkv_g2, rwkv_v0=rwkv_v0, rwkv_v1=rwkv_v1, rwkv_v2=rwkv_v2, rwkv_kk=rwkv_kk,
             rwkv_ka=rwkv_ka, rwkv_rk=rwkv_rk, rwkv_lnx_g=rwkv_lnx_g, rwkv_lnx_b=rwkv_lnx_b)
    b, s, d = x.shape
    t = b * s
    depth = w_in.shape[0]
    n_a = w_in.shape[2] - 1224 - 2 * d
    bf = lambda a: a.astype(BF16)
    cos, sin = _rope_tables(positions)
    x = x.reshape(t, d)
    v_first = None
    for l in range(depth):
        w_a = bf(w_in[l, :, 0:n_a])
        w_b = bf(_dsa_weight(w_in[l, :, n_a:n_a + 1224]))
        w_g = bf(w_in[l, :, n_a + 1224:])
        cols_a, cols_b, gates = _norm_matmul(x, norm_mix[l], (w_a, w_b, w_g))
        y_a, v_first = _rwkv_branch(cols_a.reshape(b, s, n_a), v_first, _rwkv_params(w, l))
        q, iq, kik, vw = _dsa_prep(cols_b, cos, sin, dsa_q_norm[l], dsa_k_norm[l], idx_k_norm[l])
        y_b = _dsa_attention(q, iq, kik, vw, b, s)
        x = _merge(x, gates, y_a.reshape(t, MIX_DIM), y_b.reshape(t, MIX_DIM),
                   bf(w_branch_a[l]), bf(w_branch_b[l]), bf(w_out[l]))
        mk, mv = _mem_kv(mem, mem_tok_norm[l], bf(mem_wkv[l]), mem_k_norm[l])
        x = _mem_attention(x.reshape(b, s, d), norm_mem[l], bf(mem_wq[l]), mem_q_norm[l], mk, mv,
                           bf(mem_wo[l])).reshape(t, d)
        j = l // 2
        if l % 2 == 0:
            x = _ffn(x, norm_ffn[l], bf(ffn_wg[j]), bf(ffn_wu[j]), bf(ffn_wd[j]))
        else:
            x = _moe(x, norm_ffn[l], moe_router[j], moe_bias[j], bf(moe_wg[j]), bf(moe_wu[j]), bf(moe_wd[j]))
    return x.reshape(b, s, d)
```
